```python
import math
import jax, jax.numpy as jnp
from jax import lax
import numpy as np

D_MODEL = 4096
BATCH = 32
SEQ = 256
DEPTH = 4
DEC_BATCH = 2
DEC_SEQ = 4096
PAST_LEN = 512

GRID_W = 64
HEAD_DIM = 128
ATTN_WIDTH = D_MODEL // 2
N_HEADS = ATTN_WIDTH // HEAD_DIM
N_KV_HEADS = N_HEADS // 4
KV_WIDTH = N_KV_HEADS * HEAD_DIM
SSM_WIDTH = D_MODEL // 4
SSM_GROUP = 16
N_SSM_GROUPS = SSM_WIDTH // SSM_GROUP
SSM_STATE = 64
FFT_WIDTH = D_MODEL // 4
N_FFT_GROUPS = 4
FFT_GROUP = FFT_WIDTH // N_FFT_GROUPS
MIX_WIDTH = ATTN_WIDTH + SSM_WIDTH + FFT_WIDTH
IN_SIZES = (ATTN_WIDTH, KV_WIDTH, KV_WIDTH, ATTN_WIDTH, SSM_WIDTH, SSM_WIDTH, FFT_WIDTH, FFT_WIDTH)
IN_WIDTH = 2 * ATTN_WIDTH + 2 * KV_WIDTH + 2 * SSM_WIDTH + 2 * FFT_WIDTH
Q_BLOCK = 128
ROPE_THETA = 10000.0
ROPE_PAIRS = HEAD_DIM // 4
NORM_EPS = 1e-6

kernel_name = 'hymba_style_flow_backbone_step'


def rms_norm(x, g):
    xf = x.astype(jnp.float32)
    y = xf * lax.rsqrt(jnp.mean(xf * xf, axis=-1, keepdims=True) + NORM_EPS)
    return (y * g.astype(jnp.float32)).astype(x.dtype)


def grid_angles(n_tokens):
    rows = n_tokens // GRID_W
    row = jnp.repeat(jnp.arange(rows, dtype=jnp.float32), GRID_W)
    col = jnp.tile(jnp.arange(GRID_W, dtype=jnp.float32), rows)
    inv = ROPE_THETA ** (-jnp.arange(ROPE_PAIRS, dtype=jnp.float32) / ROPE_PAIRS)
    return row[:, None] * inv[None, :], col[:, None] * inv[None, :]


def rope_1d(x, ang):
    r = ang.shape[-1]
    x1, x2 = x[..., :r], x[..., r:]
    cos = jnp.cos(ang)[None, :, None, :]
    sin = jnp.sin(ang)[None, :, None, :]
    return jnp.concatenate([x1 * cos - x2 * sin, x1 * sin + x2 * cos], axis=-1)


def axial_rope(x, row_ang, col_ang):
    half = HEAD_DIM // 2
    xf = x.astype(jnp.float32)
    out = jnp.concatenate([rope_1d(xf[..., :half], row_ang), rope_1d(xf[..., half:], col_ang)], axis=-1)
    return out.astype(x.dtype)


def blocked_attention(q, k, v):
    bsz, lq = q.shape[0], q.shape[1]
    nb = lq // Q_BLOCK
    grp = N_HEADS // N_KV_HEADS
    qb = q.reshape(bsz, nb, Q_BLOCK, N_KV_HEADS, grp, HEAD_DIM).transpose(1, 0, 2, 3, 4, 5)
    scale = HEAD_DIM ** -0.5

    def one_block(qblk):
        s = jnp.einsum('bqkgd,bskd->bkgqs', qblk, k, preferred_element_type=jnp.float32) * scale
        p = jax.nn.softmax(s, axis=-1)
        return jnp.einsum('bkgqs,bskd->bqkgd', p.astype(v.dtype), v)

    o = lax.map(one_block, qb)
    return o.transpose(1, 0, 2, 3, 4, 5).reshape(bsz, lq, N_HEADS * HEAD_DIM)


def zoh(lam_re, lam_im, log_step, b_re, b_im):
    lr = lam_re.astype(jnp.float32)
    li = lam_im.astype(jnp.float32)
    dt = jnp.exp(log_step.astype(jnp.float32))[:, None]
    mag = jnp.exp(lr * dt)
    ang = li * dt
    ab_re, ab_im = mag * jnp.cos(ang), mag * jnp.sin(ang)
    nr, ni = ab_re - 1.0, ab_im
    den = lr * lr + li * li
    f_re = (nr * lr + ni * li) / den
    f_im = (ni * lr - nr * li) / den
    br, bi = b_re.astype(jnp.float32), b_im.astype(jnp.float32)
    bb_re = f_re[..., None] * br - f_im[..., None] * bi
    bb_im = f_re[..., None] * bi + f_im[..., None] * br
    return ab_re, ab_im, bb_re, bb_im


def linrec_combine(e1, e2):
    a1r, a1i, b1r, b1i = e1
    a2r, a2i, b2r, b2i = e2
    return (a2r * a1r - a2i * a1i,
            a2r * a1i + a2i * a1r,
            a2r * b1r - a2i * b1i + b2r,
            a2r * b1i + a2i * b1r + b2i)


def ssm_direction(u, lam_re, lam_im, log_step, b_re, b_im, c_re, c_im, h0_re, h0_im, reverse):
    ab_re, ab_im, bb_re, bb_im = zoh(lam_re, lam_im, log_step, b_re, b_im)
    if reverse:
        u = jnp.flip(u, axis=1)
    bu_re = jnp.einsum('blgc,gpc->blgp', u, bb_re)
    bu_im = jnp.einsum('blgc,gpc->blgp', u, bb_im)
    a_re = jnp.broadcast_to(ab_re, bu_re.shape)
    a_im = jnp.broadcast_to(ab_im, bu_im.shape)
    cum_r, cum_i, s_r, s_i = lax.associative_scan(linrec_combine, (a_re, a_im, bu_re, bu_im), axis=1)
    h0r = h0_re.astype(jnp.float32)[:, None]
    h0i = h0_im.astype(jnp.float32)[:, None]
    h_r = cum_r * h0r - cum_i * h0i + s_r
    h_i = cum_r * h0i + cum_i * h0r + s_i
    y = (jnp.einsum('blgp,gcp->blgc', h_r, c_re.astype(jnp.float32))
         - jnp.einsum('blgp,gcp->blgc', h_i, c_im.astype(jnp.float32)))
    if reverse:
        y = jnp.flip(y, axis=1)
    return y, h_r[:, -1], h_i[:, -1]


def ssm_branch(u, p, h0):
    bsz, length = u.shape[0], u.shape[1]
    uf = u.astype(jnp.float32).reshape(bsz, length, N_SSM_GROUPS, SSM_GROUP)
    y_f, fr, fi = ssm_direction(uf, p['lam_re'][0], p['lam_im'][0], p['log_step'][0], p['b_re'][0], p['b_im'][0],
                                p['c_re'][0], p['c_im'][0], h0[0][0], h0[0][1], False)
    y_b, br, bi = ssm_direction(uf, p['lam_re'][1], p['lam_im'][1], p['log_step'][1], p['b_re'][1], p['b_im'][1],
                                p['c_re'][1], p['c_im'][1], h0[1][0], h0[1][1], True)
    y = ((y_f + y_b).reshape(bsz, length, SSM_WIDTH)
         + p['d_skip'].astype(jnp.float32) * uf.reshape(bsz, length, SSM_WIDTH))
    z = y.astype(u.dtype) @ p['w_glu']
    a, g = jnp.split(z, 2, axis=-1)
    return a * jax.nn.sigmoid(g), ((fr, fi), (br, bi))


def fourier_branch(f, w_fft):
    bsz, length = f.shape[0], f.shape[1]
    fg = f.astype(jnp.float32).reshape(bsz, length, N_FFT_GROUPS, FFT_GROUP)
    mixed = jnp.real(jnp.fft.fft2(fg, axes=(1, 3), norm='ortho'))
    return mixed.reshape(bsz, length, FFT_WIDTH).astype(f.dtype) @ w_fft


def modulated_input(x, cvec, p):
    mod = jax.nn.silu(cvec) @ p['w_mod'] + p['b_mod']
    shift, scale, gate = jnp.split(mod[:, None, :], 3, axis=-1)
    h = rms_norm(x, p['norm_g']) * (1 + scale) + shift
    return h @ p['w_in'], gate


def split_proj(proj):
    idx = np.cumsum(IN_SIZES)[:-1].tolist()
    return jnp.split(proj, idx, axis=-1)


def mixer_layer(x, cvec, p, ctx_k=None, ctx_v=None, h0=None, angles=None):
    bsz, length = x.shape[0], x.shape[1]
    proj, gate = modulated_input(x, cvec, p)
    q, k, v, g_attn, u, g_ssm, f, g_fft = split_proj(proj)
    q = rms_norm(q.reshape(bsz, length, N_HEADS, HEAD_DIM), p['q_norm'])
    k = rms_norm(k.reshape(bsz, length, N_KV_HEADS, HEAD_DIM), p['k_norm'])
    v = v.reshape(bsz, length, N_KV_HEADS, HEAD_DIM)
    if angles is None:
        keys, vals = k, v
        zeros = jnp.zeros((bsz, N_SSM_GROUPS, SSM_STATE), jnp.float32)
        h0 = ((zeros, zeros), (zeros, zeros))
    else:
        q = axial_rope(q, angles[0], angles[1])
        k = axial_rope(k, angles[0], angles[1])
        keys = jnp.concatenate([ctx_k.astype(k.dtype), k], axis=1)
        vals = jnp.concatenate([ctx_v.astype(v.dtype), v], axis=1)
    attn = blocked_attention(q, keys, vals) * jax.nn.silu(g_attn)
    ssm, finals = ssm_branch(u, p, h0)
    ssm = ssm * jax.nn.silu(g_ssm)
    four = fourier_branch(f, p['w_fft']) * jax.nn.silu(g_fft)
    out = jnp.concatenate([attn, ssm.astype(attn.dtype), four.astype(attn.dtype)], axis=-1) @ p['w_out']
    return x + gate * out, k, v, finals


def layer_params(l, norm_g, w_mod, b_mod, w_in, q_norm, k_norm, lam_re, lam_im, log_step,
                 b_re, b_im, c_re, c_im, d_skip, w_glu, w_fft, w_out):
    return dict(norm_g=norm_g[l], w_mod=w_mod[l], b_mod=b_mod[l], w_in=w_in[l],
                q_norm=q_norm[l], k_norm=k_norm[l], lam_re=lam_re[l], lam_im=lam_im[l],
                log_step=log_step[l], b_re=b_re[l], b_im=b_im[l], c_re=c_re[l], c_im=c_im[l],
                d_skip=d_skip[l], w_glu=w_glu[l], w_fft=w_fft[l], w_out=w_out[l])


def setup_inputs(seed: int = 0) -> dict:
    key = jax.random.key(seed)
    ks = jax.random.split(key, 32)
    f32 = jnp.float32

    def nrm(k, shape, s):
        return s * jax.random.normal(k, shape, f32)

    n_idx = jnp.arange(SSM_STATE, dtype=f32)
    ssm_shape = (DEPTH, 2, N_SSM_GROUPS, SSM_STATE)
    state_shape = (DEC_BATCH, DEPTH, N_SSM_GROUPS, SSM_STATE)
    kv_shape = (DEC_BATCH, DEPTH, PAST_LEN, N_KV_HEADS, HEAD_DIM)
    return {
        'x_prompt': nrm(ks[0], (BATCH, SEQ, D_MODEL), 1.0),
        'x_sample': nrm(ks[1], (DEC_BATCH, DEC_SEQ, D_MODEL), 1.0),
        'cache_k': nrm(ks[2], kv_shape, 1.0),
        'cache_v': nrm(ks[3], kv_shape, 1.0),
        'state_fwd_re': nrm(ks[4], state_shape, 0.5),
        'state_fwd_im': nrm(ks[5], state_shape, 0.5),
        'state_bwd_re': nrm(ks[6], state_shape, 0.5),
        'state_bwd_im': nrm(ks[7], state_shape, 0.5),
        'c': nrm(ks[8], (DEC_BATCH, D_MODEL), 1.0),
        'c_ctx': nrm(ks[9], (D_MODEL,), 1.0),
        'norm_g': 1.0 + nrm(ks[10], (DEPTH, D_MODEL), 0.02),
        'w_mod': nrm(ks[11], (DEPTH, D_MODEL, 3 * D_MODEL), 0.5 * D_MODEL ** -0.5),
        'b_mod': nrm(ks[12], (DEPTH, 3 * D_MODEL), 0.01),
        'w_in': nrm(ks[13], (DEPTH, D_MODEL, IN_WIDTH), D_MODEL ** -0.5),
        'q_norm': 1.0 + nrm(ks[14], (DEPTH, HEAD_DIM), 0.02),
        'k_norm': 1.0 + nrm(ks[15], (DEPTH, HEAD_DIM), 0.02),
        'lam_re': -0.5 + nrm(ks[16], ssm_shape, 0.01),
        'lam_im': math.pi * n_idx + nrm(ks[17], ssm_shape, 0.01),
        'log_step': jax.random.uniform(ks[18], (DEPTH, 2, N_SSM_GROUPS), f32,
                                       minval=math.log(1e-3), maxval=math.log(1e-1)),
        'b_re': nrm(ks[19], (DEPTH, 2, N_SSM_GROUPS, SSM_STATE, SSM_GROUP), (2 * SSM_GROUP) ** -0.5),
        'b_im': nrm(ks[20], (DEPTH, 2, N_SSM_GROUPS, SSM_STATE, SSM_GROUP), (2 * SSM_GROUP) ** -0.5),
        'c_re': nrm(ks[21], (DEPTH, 2, N_SSM_GROUPS, SSM_GROUP, SSM_STATE), (2 * SSM_STATE) ** -0.5),
        'c_im': nrm(ks[22], (DEPTH, 2, N_SSM_GROUPS, SSM_GROUP, SSM_STATE), (2 * SSM_STATE) ** -0.5),
        'd_skip': nrm(ks[23], (DEPTH, SSM_WIDTH), 0.5),
        'w_glu': nrm(ks[24], (DEPTH, SSM_WIDTH, 2 * SSM_WIDTH), SSM_WIDTH ** -0.5),
        'w_fft': nrm(ks[25], (DEPTH, FFT_WIDTH, FFT_WIDTH), FFT_WIDTH ** -0.5),
        'w_out': nrm(ks[26], (DEPTH, MIX_WIDTH, D_MODEL), MIX_WIDTH ** -0.5),
        'final_norm_g': 1.0 + nrm(ks[27], (D_MODEL,), 0.02),
    }


def reference(x_prompt, x_sample, cache_k, cache_v, state_fwd_re, state_fwd_im, state_bwd_re, state_bwd_im,
              c, c_ctx, norm_g, w_mod, b_mod, w_in, q_norm, k_norm, lam_re, lam_im, log_step,
              b_re, b_im, c_re, c_im, d_skip, w_glu, w_fft, w_out, final_norm_g):
    params = [layer_params(l, norm_g, w_mod, b_mod, w_in, q_norm, k_norm, lam_re, lam_im, log_step,
                           b_re, b_im, c_re, c_im, d_skip, w_glu, w_fft, w_out) for l in range(DEPTH)]

    ctx_vec = c_ctx[None, :]
    xp = x_prompt
    ks_, vs_, fr_, fi_, br_, bi_ = [], [], [], [], [], []
    for l in range(DEPTH):
        xp, k_l, v_l, fin = mixer_layer(xp, ctx_vec, params[l])
        ks_.append(k_l)
        vs_.append(v_l)
        fr_.append(fin[0][0])
        fi_.append(fin[0][1])
        br_.append(fin[1][0])
        bi_.append(fin[1][1])
    y_prompt = rms_norm(xp, final_norm_g)

    angles = grid_angles(x_sample.shape[1])
    xs = x_sample
    for l in range(DEPTH):
        h0 = ((state_fwd_re[:, l], state_fwd_im[:, l]), (state_bwd_re[:, l], state_bwd_im[:, l]))
        xs, _, _, _ = mixer_layer(xs, c, params[l], cache_k[:, l], cache_v[:, l], h0, angles)
    y_sample = rms_norm(xs, final_norm_g)

    return (y_prompt, y_sample,
            jnp.stack(ks_, axis=1), jnp.stack(vs_, axis=1),
            jnp.stack(fr_, axis=1), jnp.stack(fi_, axis=1),
            jnp.stack(br_, axis=1), jnp.stack(bi_, axis=1))
```

```python
import functools
import math
from typing import NamedTuple

import jax
import jax.numpy as jnp
from jax import lax
from jax.experimental import pallas as pl
from jax.experimental.pallas import tpu as pltpu

F32 = jnp.float32
BF16 = jnp.bfloat16
NORM_EPS = 1e-6
ROPE_THETA = 10000.0
LANES = 128
SSM_CHUNK = 16
VMEM_LIMIT = 56 * 1024 * 1024


class Cfg(NamedTuple):
    d_model: int = 4096
    batch: int = 32
    seq: int = 256
    depth: int = 4
    dec_batch: int = 2
    dec_seq: int = 4096
    past_len: int = 512
    grid_w: int = 64
    head_dim: int = 128
    n_heads: int = 16
    n_kv: int = 4
    ssm_width: int = 1024
    ssm_group: int = 16
    ssm_state: int = 64
    fft_width: int = 1024
    fft_group: int = 256
    tm_in: int = 512
    tm_out: int = 512
    tn_out: int = 1024
    tn_mod: int = 512
    tq: int = 256
    tm_fft: int = 256
    rc_fft: int = 512
    tm_post: int = 512
    n_k: int = 16

    @property
    def attn_width(self):
        return self.n_heads * self.head_dim

    @property
    def kv_width(self):
        return self.n_kv * self.head_dim

    @property
    def q_per_kv(self):
        return self.n_heads // self.n_kv

    @property
    def n_groups(self):
        return self.ssm_width // self.ssm_group

    @property
    def tn_in(self):
        return 2 * self.kv_width

    @property
    def main_width(self):
        return 2 * self.attn_width + 2 * self.ssm_width + 2 * self.fft_width

    @property
    def p_tokens(self):
        return self.batch * self.seq

    @property
    def s_tokens(self):
        return self.dec_batch * self.dec_seq

    @property
    def tokens(self):
        return self.p_tokens + self.s_tokens


def _params(*sem):
    return pltpu.CompilerParams(dimension_semantics=sem, vmem_limit_bytes=VMEM_LIMIT)


def _silu(x):
    return x / (1.0 + jnp.exp(-x))


def _row_group(i, cfg, tm):
    npb = cfg.p_tokens // tm
    per = cfg.dec_seq // tm
    return jnp.where(i < npb, 0, 1 + (i - npb) // per)


def _mod_kernel(c_ref, w_ref, b_ref, o_ref):
    s = _silu(c_ref[...]).astype(BF16)
    o_ref[0] = jnp.dot(s, w_ref[0].astype(BF16), preferred_element_type=F32) + b_ref[0]


def _modulation(cvecs, w_mod, b_mod, cfg):
    d, n = cfg.d_model, 3 * cfg.d_model
    nrow = cvecs.shape[0]
    return pl.pallas_call(
        _mod_kernel,
        grid=(cfg.depth, n // cfg.tn_mod),
        in_specs=[
            pl.BlockSpec((nrow, d), lambda l, j: (0, 0)),
            pl.BlockSpec((1, d, cfg.tn_mod), lambda l, j: (l, 0, j)),
            pl.BlockSpec((1, 1, cfg.tn_mod), lambda l, j: (l, 0, j)),
        ],
        out_specs=pl.BlockSpec((1, nrow, cfg.tn_mod), lambda l, j: (l, 0, j)),
        out_shape=jax.ShapeDtypeStruct((cfg.depth, nrow, n), F32),
        compiler_params=_params("parallel", "parallel"),
        name="modulation",
    )(cvecs, w_mod, b_mod.reshape(cfg.depth, 1, n))


def _in_proj_kernel(x_ref, mod_ref, ng_ref, w_ref, gains_ref, cos_ref, sa_ref, sb_ref,
                    kv_ref, main_ref, h_scr, *, cfg, kinds):
    j = pl.program_id(1)
    tm = x_ref.shape[0]
    hd = cfg.head_dim
    rc = min(tm, 64)

    @pl.when(j == 0)
    def _():
        shift = mod_ref[0, 0:1, :]
        scale1 = 1.0 + mod_ref[0, 1:2, :]
        g = ng_ref[...]

        def body(r, carry):
            rows = pl.ds(pl.multiple_of(r * rc, rc), rc)
            x = x_ref[rows, :]
            ms = jnp.mean(x * x, axis=-1, keepdims=True)
            y = x * lax.rsqrt(ms + NORM_EPS) * g
            h_scr[rows, :] = (y * scale1 + shift).astype(BF16)
            return carry

        lax.fori_loop(0, tm // rc, body, 0)

    acc = jnp.dot(h_scr[...], w_ref[...], preferred_element_type=F32)

    def head_norm_rope(a, gain):
        ms = jnp.mean(a * a, axis=-1, keepdims=True)
        y = a * lax.rsqrt(ms + NORM_EPS) * gain
        return (y * cos_ref[...] + pltpu.roll(y, hd - hd // 4, 1) * sa_ref[...]
                + pltpu.roll(y, hd // 4, 1) * sb_ref[...])

    tn = acc.shape[1]

    @pl.when(j == 0)
    def _():
        kg = gains_ref[0:1, :]
        for h in range(cfg.n_kv):
            kv_ref[:, h * hd:(h + 1) * hd] = head_norm_rope(acc[:, h * hd:(h + 1) * hd], kg)
        kv_ref[:, cfg.kv_width:] = acc[:, cfg.kv_width:]

    q_lo, q_hi = kinds["q"]

    @pl.when((j >= q_lo) & (j < q_hi))
    def _():
        qg = gains_ref[1:2, :]
        for h in range(tn // hd):
            main_ref[:, h * hd:(h + 1) * hd] = head_norm_rope(
                acc[:, h * hd:(h + 1) * hd], qg).astype(BF16)

    silu_pred = functools.reduce(
        lambda a, b: a | b, [(j >= lo) & (j < hi) for lo, hi in kinds["silu"]])
    plain_pred = functools.reduce(
        lambda a, b: a | b, [(j >= lo) & (j < hi) for lo, hi in kinds["plain"]])

    @pl.when(silu_pred)
    def _():
        main_ref[...] = _silu(acc).astype(BF16)

    @pl.when(plain_pred)
    def _():
        main_ref[...] = acc.astype(BF16)


def _tile_kinds(cfg):
    tn = cfg.tn_in
    widths = [("q", cfg.attn_width), ("silu", cfg.attn_width), ("plain", cfg.ssm_width),
              ("silu", cfg.ssm_width), ("plain", cfg.fft_width), ("silu", cfg.fft_width)]
    kinds = {"q": None, "silu": [], "plain": []}
    j = 1
    for name, w in widths:
        assert w % tn == 0
        rng = (j, j + w // tn)
        j += w // tn
        if name == "q":
            kinds["q"] = rng
        else:
            kinds[name].append(rng)
    return kinds


def _in_proj(x, mod_l, norm_g_l, w_in_l, gains_l, rope_tabs, cfg):
    tm, tn, d = cfg.tm_in, cfg.tn_in, cfg.d_model
    nt = cfg.tokens
    nj = 1 + cfg.main_width // tn
    npb = cfg.p_tokens // tm
    per = cfg.dec_seq // tm
    cos_t, sa_t, sb_t = rope_tabs

    def tab_map(i, j):
        return (jnp.where(i < npb, 0, 1 + (i - npb) % per), 0)

    tab_spec = pl.BlockSpec((tm, cfg.head_dim), tab_map)
    return pl.pallas_call(
        functools.partial(_in_proj_kernel, cfg=cfg, kinds=_tile_kinds(cfg)),
        grid=(nt // tm, nj),
        in_specs=[
            pl.BlockSpec((tm, d), lambda i, j: (i, 0)),
            pl.BlockSpec((1, 3, d), lambda i, j: (_row_group(i, cfg, tm), 0, 0)),
            pl.BlockSpec((1, d), lambda i, j: (0, 0)),
            pl.BlockSpec((d, tn), lambda i, j: (0, j)),
            pl.BlockSpec((2, cfg.head_dim), lambda i, j: (0, 0)),
            tab_spec, tab_spec, tab_spec,
        ],
        out_specs=[
            pl.BlockSpec((tm, tn), lambda i, j: (i, 0)),
            pl.BlockSpec((tm, tn), lambda i, j: (i, jnp.maximum(j - 1, 0))),
        ],
        out_shape=[
            jax.ShapeDtypeStruct((nt, tn), F32),
            jax.ShapeDtypeStruct((nt, cfg.main_width), BF16),
        ],
        scratch_shapes=[pltpu.VMEM((tm, d), BF16)],
        compiler_params=_params("parallel", "arbitrary"),
        name="in_proj",
    )(x, mod_l, norm_g_l, w_in_l, gains_l, cos_t, sa_t, sb_t)


def _softmax_attend(q, k, v, scale):
    s = lax.dot_general(q, k, (((1,), (1,)), ((), ())), preferred_element_type=F32) * scale
    m = jnp.max(s, axis=-1, keepdims=True)
    p = jnp.exp(s - m)
    l = jnp.sum(p, axis=-1, keepdims=True)
    o = jnp.dot(p.astype(BF16), v, preferred_element_type=F32)
    return o / l


def _attn_ctx_kernel(q_ref, k_ref, v_ref, g_ref, o_ref, *, cfg):
    hd = cfg.head_dim
    scale = hd ** -0.5
    k = k_ref[...].astype(BF16)
    v = v_ref[...].astype(BF16)
    for h in range(cfg.q_per_kv):
        sl = slice(h * hd, (h + 1) * hd)
        o = _softmax_attend(q_ref[:, sl], k, v, scale)
        o_ref[:, sl] = (o * g_ref[:, sl].astype(F32)).astype(BF16)


def _attn_context(main, kv, cfg):
    hd, nkv = cfg.head_dim, cfg.n_kv
    qw = cfg.q_per_kv * hd
    return pl.pallas_call(
        functools.partial(_attn_ctx_kernel, cfg=cfg),
        grid=(cfg.batch, nkv),
        in_specs=[
            pl.BlockSpec((cfg.seq, qw), lambda b, h: (b, h)),
            pl.BlockSpec((cfg.seq, hd), lambda b, h: (b, h)),
            pl.BlockSpec((cfg.seq, hd), lambda b, h: (b, nkv + h)),
            pl.BlockSpec((cfg.seq, qw), lambda b, h: (b, nkv + h)),
        ],
        out_specs=pl.BlockSpec((cfg.seq, qw), lambda b, h: (b, h)),
        out_shape=jax.ShapeDtypeStruct((cfg.tokens, cfg.attn_width), BF16),
        compiler_params=_params("parallel", "parallel"),
        name="attn_context",
    )(main, kv, kv, main)


def _attn_lat_kernel(prev_ref, q_ref, ck_ref, cv_ref, kn_ref, vn_ref, g_ref, o_ref,
                     k_scr, v_scr, *, cfg):
    del prev_ref
    hd, past = cfg.head_dim, cfg.past_len
    scale = hd ** -0.5

    @pl.when(pl.program_id(2) == 0)
    def _():
        k_scr[0:past, :] = ck_ref[...].astype(BF16)
        k_scr[past:, :] = kn_ref[...].astype(BF16)
        v_scr[0:past, :] = cv_ref[...].astype(BF16)
        v_scr[past:, :] = vn_ref[...].astype(BF16)

    k = k_scr[...]
    v = v_scr[...]
    for h in range(cfg.q_per_kv):
        sl = slice(h * hd, (h + 1) * hd)
        o = _softmax_attend(q_ref[:, sl], k, v, scale)
        o_ref[:, sl] = (o * g_ref[:, sl].astype(F32)).astype(BF16)


def _attn_latent(attn_prev, main, kv, cache_k, cache_v, layer, cfg):
    hd, nkv, tq = cfg.head_dim, cfg.n_kv, cfg.tq
    qw = cfg.q_per_kv * hd
    nqb = cfg.dec_seq // tq
    row0 = cfg.p_tokens // tq
    kvrow0 = cfg.p_tokens // cfg.dec_seq

    def q_map(b, h, qi):
        return (row0 + b * nqb + qi, h)

    def g_map(b, h, qi):
        return (row0 + b * nqb + qi, nkv + h)

    cache_spec = pl.BlockSpec((None, None, cfg.past_len, hd), lambda b, h, qi: (b, layer, 0, h))
    return pl.pallas_call(
        functools.partial(_attn_lat_kernel, cfg=cfg),
        grid=(cfg.dec_batch, nkv, nqb),
        in_specs=[
            pl.BlockSpec(memory_space=pl.ANY),
            pl.BlockSpec((tq, qw), q_map),
            cache_spec, cache_spec,
            pl.BlockSpec((cfg.dec_seq, hd), lambda b, h, qi: (kvrow0 + b, h)),
            pl.BlockSpec((cfg.dec_seq, hd), lambda b, h, qi: (kvrow0 + b, nkv + h)),
            pl.BlockSpec((tq, qw), g_map),
        ],
        out_specs=pl.BlockSpec((tq, qw), q_map),
        out_shape=jax.ShapeDtypeStruct((cfg.tokens, cfg.attn_width), BF16),
        scratch_shapes=[pltpu.VMEM((cfg.past_len + cfg.dec_seq, hd), BF16),
                        pltpu.VMEM((cfg.past_len + cfg.dec_seq, hd), BF16)],
        input_output_aliases={0: 0},
        compiler_params=_params("parallel", "parallel", "arbitrary"),
        name="attn_latent",
    )(attn_prev, main, cache_k, cache_v, kv, kv, main)


def _ssm_kernel(u_ref, w1_ref, w2_ref, av_ref, pw_ref, dsk_ref, h0_ref, y_ref, fin_ref,
                s_scr, hin_scr, f_scr, g_scr, *, cfg):
    tc = SSM_CHUNK * cfg.ssm_group
    st = 2 * cfg.ssm_state
    n_k = cfg.n_k
    u = u_ref[0]
    r = jnp.dot(u, w1_ref[0], preferred_element_type=F32)
    y_intra = r[:, :tc]
    s_scr[...] = r[:, tc:]
    av = av_ref[0]
    a1 = (av[0:1], av[2:3])
    a2 = (av[1:2], av[3:4])
    b1 = (av[4:5], av[6:7])
    b2 = (av[5:6], av[7:8])

    def cmul(h, hs, c1, c2):
        return h * c1 + hs * c2, hs * c1 - h * c2

    def run_path(row0, nb, n_sc, h0):
        nch = nb * n_sc

        def level1(k, carry):
            hf, hfs, hb, hbs = carry
            rf = pl.ds(pl.multiple_of(row0 + k * nch, nch), nch)
            rb = pl.ds(pl.multiple_of(row0 + (n_k - 1 - k) * nch, nch), nch)
            hin_scr[rf, 0:st] = hf
            hin_scr[rf, st:2 * st] = hfs
            hin_scr[rb, 2 * st:3 * st] = hb
            hin_scr[rb, 3 * st:4 * st] = hbs
            nf, nfs = cmul(hf, hfs, a1[0], a2[0])
            nbk, nbs = cmul(hb, hbs, a1[1], a2[1])
            return (nf + s_scr[rf, 0:st], nfs + s_scr[rf, st:2 * st],
                    nbk + s_scr[rb, 2 * st:3 * st], nbs + s_scr[rb, 3 * st:4 * st])

        z = jnp.zeros((nch, st), F32)
        ff, ffs, fb, fbs = lax.fori_loop(0, n_k, level1, (z, z, z, z))
        if h0 is None and n_sc == 1:
            return ff, fb
        for idx, val in enumerate((ff, ffs, fb, fbs)):
            f_scr[idx, 0:nch, :] = val
        gf, gfs = h0[0], h0[1]
        gb, gbs = h0[2], h0[3]
        for sc in range(n_sc):
            rows = slice(sc * nb, (sc + 1) * nb)
            g_scr[0, rows, :] = gf
            g_scr[1, rows, :] = gfs
            nf, nfs = cmul(gf, gfs, b1[0], b2[0])
            gf = nf + f_scr[0, rows, :]
            gfs = nfs + f_scr[1, rows, :]
        for sc in range(n_sc - 1, -1, -1):
            rows = slice(sc * nb, (sc + 1) * nb)
            g_scr[2, rows, :] = gb
            g_scr[3, rows, :] = gbs
            nbk, nbs = cmul(gb, gbs, b1[1], b2[1])
            gb = nbk + f_scr[2, rows, :]
            gbs = nbs + f_scr[3, rows, :]
        gfa, gfsa = g_scr[0, 0:nch, :], g_scr[1, 0:nch, :]
        gba, gbsa = g_scr[2, 0:nch, :], g_scr[3, 0:nch, :]
        for k in range(n_k):
            rows = slice(row0 + k * nch, row0 + (k + 1) * nch)
            pf = pw_ref[0, k]
            cf, cfs = cmul(gfa, gfsa, pf[0:1], pf[1:2])
            cb, cbs = cmul(gba, gbsa, pf[2:3], pf[3:4])
            hin_scr[rows, 0:st] += cf
            hin_scr[rows, st:2 * st] += cfs
            hin_scr[rows, 2 * st:3 * st] += cb
            hin_scr[rows, 3 * st:4 * st] += cbs
        return gf, gb

    fin_f, fin_b = run_path(0, cfg.batch, cfg.seq // (SSM_CHUNK * n_k), None)
    fin_ref[0, 0] = fin_f
    fin_ref[0, 1] = fin_b
    h0 = h0_ref[0]
    run_path(cfg.p_tokens // SSM_CHUNK, cfg.dec_batch, cfg.dec_seq // (SSM_CHUNK * n_k),
             (h0[0], h0[1], h0[2], h0[3]))

    hin = hin_scr[...]
    hsel = jnp.concatenate([hin[:, 0:st], hin[:, 2 * st:3 * st]], axis=1).astype(BF16)
    y = y_intra + jnp.dot(hsel, w2_ref[0], preferred_element_type=F32)
    y = y + dsk_ref[0] * u.astype(F32)
    y_ref[0] = y.astype(BF16)


def _ssm_scan(u_r, ssm_w, h0, cfg):
    w1, w2, av, pw, dsk = ssm_w
    g = cfg.n_groups
    rows = cfg.tokens // SSM_CHUNK
    tc = SSM_CHUNK * cfg.ssm_group
    st = 2 * cfg.ssm_state
    assert cfg.seq == SSM_CHUNK * cfg.n_k
    nch = max(cfg.batch, cfg.dec_batch * (cfg.dec_seq // (SSM_CHUNK * cfg.n_k)))
    return pl.pallas_call(
        functools.partial(_ssm_kernel, cfg=cfg),
        grid=(g,),
        in_specs=[
            pl.BlockSpec((1, rows, tc), lambda i: (i, 0, 0)),
            pl.BlockSpec((1, tc, tc + 4 * st), lambda i: (i, 0, 0)),
            pl.BlockSpec((1, 2 * st, tc), lambda i: (i, 0, 0)),
            pl.BlockSpec((1, 8, st), lambda i: (i, 0, 0)),
            pl.BlockSpec((1, cfg.n_k, 4, st), lambda i: (i, 0, 0, 0)),
            pl.BlockSpec((1, 1, tc), lambda i: (i, 0, 0)),
            pl.BlockSpec((1, 4, cfg.dec_batch, st), lambda i: (i, 0, 0, 0)),
        ],
        out_specs=[
            pl.BlockSpec((1, rows, tc), lambda i: (i, 0, 0)),
            pl.BlockSpec((1, 2, cfg.batch, st), lambda i: (i, 0, 0, 0)),
        ],
        out_shape=[
            jax.ShapeDtypeStruct((g, rows, tc), BF16),
            jax.ShapeDtypeStruct((g, 2, cfg.batch, st), F32),
        ],
        scratch_shapes=[pltpu.VMEM((rows, 4 * st), F32), pltpu.VMEM((rows, 4 * st), F32),
                        pltpu.VMEM((4, nch, st), F32), pltpu.VMEM((4, nch, st), F32)],
        compiler_params=_params("parallel"),
        name="ssm_scan",
    )(u_r, w1, w2, av, pw, dsk, h0)


def _ssm_weights(lam_re, lam_im, log_step, b_re, b_im, c_re, c_im, d_skip, cfg):
    T, C, P, G, nk = SSM_CHUNK, cfg.ssm_group, cfg.ssm_state, cfg.n_groups, cfg.n_k
    hi = lax.Precision.HIGHEST
    lr, li = lam_re.astype(F32), lam_im.astype(F32)
    dt = jnp.exp(log_step.astype(F32))[..., None]

    def powers(js):
        jj = jnp.asarray(js, F32)[:, None]
        mag = jnp.exp(lr[..., None, :] * dt[..., None, :] * jj)
        ang = li[..., None, :] * dt[..., None, :] * jj
        return mag * jnp.cos(ang), mag * jnp.sin(ang)

    pr, pi = powers(range(T + 1))
    ab_re, ab_im = pr[..., 1, :], pi[..., 1, :]
    nr, ni = ab_re - 1.0, ab_im
    den = lr * lr + li * li
    f_re = (nr * lr + ni * li) / den
    f_im = (ni * lr - nr * li) / den
    br, bi = b_re.astype(F32), b_im.astype(F32)
    bb_re = f_re[..., None] * br - f_im[..., None] * bi
    bb_im = f_re[..., None] * bi + f_im[..., None] * br
    cr, ci = c_re.astype(F32), c_im.astype(F32)
    ca_re = cr[..., None, :, :] * pr[..., :, None, :] - ci[..., None, :, :] * pi[..., :, None, :]
    ca_im = cr[..., None, :, :] * pi[..., :, None, :] + ci[..., None, :, :] * pr[..., :, None, :]
    kk = (jnp.einsum("...jcp,...pk->...jck", ca_re, bb_re, precision=hi)
          - jnp.einsum("...jcp,...pk->...jck", ca_im, bb_im, precision=hi))
    ii = jnp.arange(T)
    lag_f = ii[None, :] - ii[:, None]
    kf = kk[:, 0][:, :, jnp.clip(lag_f, 0, T)]
    kf = jnp.where((lag_f >= 0)[None, None, :, :, None, None], kf, 0.0)
    kb = kk[:, 1][:, :, jnp.clip(-lag_f, 0, T)]
    kb = jnp.where((lag_f <= 0)[None, None, :, :, None, None], kb, 0.0)
    m = (kf + kb).transpose(0, 1, 2, 5, 3, 4).reshape(-1, G, T * C, T * C)

    def state_in(d, pidx):
        wr = pr[:, d][:, :, pidx][..., None] * bb_re[:, d][:, :, None] \
            - pi[:, d][:, :, pidx][..., None] * bb_im[:, d][:, :, None]
        wi = pr[:, d][:, :, pidx][..., None] * bb_im[:, d][:, :, None] \
            + pi[:, d][:, :, pidx][..., None] * bb_re[:, d][:, :, None]
        wr = wr.transpose(0, 1, 2, 4, 3).reshape(-1, G, T * C, P)
        wi = wi.transpose(0, 1, 2, 4, 3).reshape(-1, G, T * C, P)
        return jnp.concatenate([wr, wi, wi, wr], axis=-1)

    w1 = jnp.concatenate([m, state_in(0, T - 1 - ii), state_in(1, ii)], axis=-1).astype(BF16)

    def state_out(d, pidx):
        wr = ca_re[:, d][:, :, pidx].transpose(0, 1, 4, 2, 3).reshape(-1, G, P, T * C)
        wi = -ca_im[:, d][:, :, pidx].transpose(0, 1, 4, 2, 3).reshape(-1, G, P, T * C)
        return jnp.concatenate([wr, wi], axis=2)

    w2 = jnp.concatenate([state_out(0, ii + 1), state_out(1, T - ii)], axis=2).astype(BF16)

    def lanes(re, im):
        return jnp.concatenate([re, re], axis=-1), jnp.concatenate([-im, im], axis=-1)

    qr, qi = powers([T * k for k in range(nk + 1)])
    rows = []
    for src_r, src_i, idx in ((pr, pi, T), (qr, qi, nk)):
        for d in range(2):
            rows.extend(lanes(src_r[:, d, :, idx], src_i[:, d, :, idx]))
    av = jnp.stack(rows, axis=2)
    kidx = jnp.arange(nk)
    p1f, p2f = lanes(qr[:, 0][:, :, kidx], qi[:, 0][:, :, kidx])
    p1b, p2b = lanes(qr[:, 1][:, :, nk - 1 - kidx], qi[:, 1][:, :, nk - 1 - kidx])
    pw = jnp.stack([p1f, p2f, p1b, p2b], axis=3)
    dsk = jnp.tile(d_skip.astype(F32).reshape(-1, G, 1, C), (1, 1, 1, T))
    return w1, w2, av, pw, dsk


def _ssm_rows_in(u, cfg):
    T, C, G, nk = SSM_CHUNK, cfg.ssm_group, cfg.n_groups, cfg.n_k
    up = u[:cfg.p_tokens].reshape(cfg.batch, nk, T, G, C).transpose(3, 1, 0, 2, 4)
    n_sc = cfg.dec_seq // (T * nk)
    us = u[cfg.p_tokens:].reshape(cfg.dec_batch, n_sc, nk, T, G, C).transpose(4, 2, 1, 0, 3, 5)
    return jnp.concatenate([up.reshape(G, -1, T * C), us.reshape(G, -1, T * C)], axis=1)


def _ssm_rows_out(y_r, cfg):
    T, C, G, nk = SSM_CHUNK, cfg.ssm_group, cfg.n_groups, cfg.n_k
    n_sc = cfg.dec_seq // (T * nk)
    prow = cfg.p_tokens // T
    yp = y_r[:, :prow].reshape(G, nk, cfg.batch, T, C).transpose(2, 1, 3, 0, 4)
    ys = y_r[:, prow:].reshape(G, nk, n_sc, cfg.dec_batch, T, C).transpose(3, 2, 1, 4, 0, 5)
    return jnp.concatenate([yp.reshape(cfg.p_tokens, G * C), ys.reshape(cfg.s_tokens, G * C)], axis=0)


def _glu_kernel(y_ref, w_ref, g_ref, o_ref, *, cfg):
    z = jnp.dot(y_ref[...], w_ref[...], preferred_element_type=F32)
    sw = cfg.ssm_width
    a, g = z[:, :sw], z[:, sw:]
    o_ref[...] = (a / (1.0 + jnp.exp(-g)) * g_ref[...].astype(F32)).astype(BF16)


def _ssm_glu(y, w_glu_l, main, cfg):
    tm, sw = cfg.tm_post, cfg.ssm_width
    gcol = (2 * cfg.attn_width + sw) // sw
    assert (2 * cfg.attn_width) % sw == 0
    return pl.pallas_call(
        functools.partial(_glu_kernel, cfg=cfg),
        grid=(cfg.tokens // tm,),
        in_specs=[
            pl.BlockSpec((tm, sw), lambda i: (i, 0)),
            pl.BlockSpec((sw, 2 * sw), lambda i: (0, 0)),
            pl.BlockSpec((tm, sw), lambda i: (i, gcol)),
        ],
        out_specs=pl.BlockSpec((tm, sw), lambda i: (i, 0)),
        out_shape=jax.ShapeDtypeStruct((cfg.tokens, sw), BF16),
        compiler_params=_params("parallel"),
        name="ssm_glu",
    )(y, w_glu_l, main)


def _fourier_kernel(prev_ref, x_ref, dl_ref, cs_ref, w_ref, g_ref, o_ref, z_scr, *, cfg, length):
    del prev_ref
    fg = cfg.fft_group
    rc = min(cfg.rc_fft, length)

    @pl.when(pl.program_id(1) == 0)
    def _():
        def body(ci, carry):
            rows = pl.ds(pl.multiple_of(ci * rc, rc), rc)
            rows2 = pl.ds(pl.multiple_of(length + ci * rc, rc), rc)
            for g in range(cfg.fft_width // fg):
                cols = slice(g * fg, (g + 1) * fg)
                t = jnp.dot(x_ref[rows, cols], cs_ref[...], preferred_element_type=F32)
                z_scr[rows, cols] = t[:, :fg].astype(BF16)
                z_scr[rows2, cols] = t[:, fg:].astype(BF16)
            return carry

        lax.fori_loop(0, length // rc, body, 0)

    mixed = jnp.dot(dl_ref[...], z_scr[...], preferred_element_type=F32).astype(BF16)
    four = jnp.dot(mixed, w_ref[...], preferred_element_type=F32)
    o_ref[...] = (four * g_ref[...].astype(F32)).astype(BF16)


def _fourier(prev, main, dl, cs, w_fft_l, cfg, *, length, nbatch, row0):
    fw = cfg.fft_width
    tm = min(cfg.tm_fft, length)
    nr = length // tm
    xcol = (2 * cfg.attn_width + 2 * cfg.ssm_width) // fw
    assert (2 * cfg.attn_width + 2 * cfg.ssm_width) % fw == 0 and row0 % length == 0
    b0 = row0 // length
    t0 = row0 // tm

    def tile_map(b, r):
        return (t0 + b * nr + r, 0)

    args = [main, dl, cs, w_fft_l, main]
    in_specs = [
        pl.BlockSpec((length, fw), lambda b, r: (b0 + b, xcol)),
        pl.BlockSpec((tm, 2 * length), lambda b, r: (r, 0)),
        pl.BlockSpec((cfg.fft_group, 2 * cfg.fft_group), lambda b, r: (0, 0)),
        pl.BlockSpec((fw, fw), lambda b, r: (0, 0)),
        pl.BlockSpec((tm, fw), lambda b, r: (t0 + b * nr + r, xcol + 1)),
    ]
    kern = functools.partial(_fourier_kernel, cfg=cfg, length=length)
    aliases = {}
    if prev is None:
        kern = functools.partial(kern, None)
    else:
        args = [prev] + args
        in_specs = [pl.BlockSpec(memory_space=pl.ANY)] + in_specs
        aliases = {0: 0}
    return pl.pallas_call(
        kern,
        grid=(nbatch, nr),
        in_specs=in_specs,
        out_specs=pl.BlockSpec((tm, fw), tile_map),
        out_shape=jax.ShapeDtypeStruct((cfg.tokens, fw), BF16),
        scratch_shapes=[pltpu.VMEM((2 * length, fw), BF16)],
        input_output_aliases=aliases,
        compiler_params=_params("parallel", "arbitrary"),
        name=f"fourier_{length}",
    )(*args)


def _dft_cos_sin(n):
    k = jnp.arange(n, dtype=jnp.int32)
    w = 2.0 * math.pi / n

    def cs(j):
        ang = ((j[:, None] * k[None, :]) % n).astype(F32) * w
        return jnp.cos(ang), jnp.sin(ang)

    s = 1
    while s * s < n:
        s *= 2
    if n <= 512 or n % s:
        c, sn = cs(k)
    else:
        c1, s1 = cs(jnp.arange(n // s, dtype=jnp.int32) * s)
        c0, s0 = cs(jnp.arange(s, dtype=jnp.int32))
        c = (c1[:, None] * c0[None] - s1[:, None] * s0[None]).reshape(n, n)
        sn = (s1[:, None] * c0[None] + c1[:, None] * s0[None]).reshape(n, n)
    scale = n ** -0.5
    return c * scale, sn * scale


def _out_proj_kernel(a_ref, s_ref, f_ref, w_ref, x_ref, mod_ref, o_ref, *, cfg):
    aw, sw = cfg.attn_width, cfg.ssm_width
    acc = jnp.dot(a_ref[...], w_ref[0:aw, :], preferred_element_type=F32)
    acc += jnp.dot(s_ref[...], w_ref[aw:aw + sw, :], preferred_element_type=F32)
    acc += jnp.dot(f_ref[...], w_ref[aw + sw:, :], preferred_element_type=F32)
    o_ref[...] = x_ref[...] + mod_ref[0, 2:3, :] * acc


def _out_proj(attn, ssm, four, w_out_l, x, mod_l, cfg):
    tm, tn, d = cfg.tm_out, cfg.tn_out, cfg.d_model
    return pl.pallas_call(
        functools.partial(_out_proj_kernel, cfg=cfg),
        grid=(d // tn, cfg.tokens // tm),
        in_specs=[
            pl.BlockSpec((tm, cfg.attn_width), lambda j, i: (i, 0)),
            pl.BlockSpec((tm, cfg.ssm_width), lambda j, i: (i, 0)),
            pl.BlockSpec((tm, cfg.fft_width), lambda j, i: (i, 0)),
            pl.BlockSpec((d, tn), lambda j, i: (0, j)),
            pl.BlockSpec((tm, tn), lambda j, i: (i, j)),
            pl.BlockSpec((1, 3, tn), lambda j, i: (_row_group(i, cfg, tm), 0, j)),
        ],
        out_specs=pl.BlockSpec((tm, tn), lambda j, i: (i, j)),
        out_shape=jax.ShapeDtypeStruct((cfg.tokens, d), F32),
        compiler_params=_params("parallel", "parallel"),
        name="out_proj",
    )(attn, ssm, four, w_out_l, x, mod_l)


def _final_norm_kernel(x_ref, g_ref, o_ref):
    x = x_ref[...]
    ms = jnp.mean(x * x, axis=-1, keepdims=True)
    o_ref[...] = x * lax.rsqrt(ms + NORM_EPS) * g_ref[...]


def _final_norm(x, g, cfg, *, row0, nrows):
    tm, d = 256, cfg.d_model
    tm = min(tm, nrows)
    assert row0 % tm == 0
    return pl.pallas_call(
        _final_norm_kernel,
        grid=(nrows // tm,),
        in_specs=[pl.BlockSpec((tm, d), lambda i: (row0 // tm + i, 0)),
                  pl.BlockSpec((1, d), lambda i: (0, 0))],
        out_specs=pl.BlockSpec((tm, d), lambda i: (i, 0)),
        out_shape=jax.ShapeDtypeStruct((nrows, d), F32),
        compiler_params=_params("parallel"),
        name="final_norm",
    )(x, g)


def _rope_tables(cfg):
    hd = cfg.head_dim
    pairs = hd // 4
    t = jnp.arange(cfg.dec_seq)
    inv = ROPE_THETA ** (-jnp.arange(pairs, dtype=F32) / pairs)
    row_ang = (t // cfg.grid_w).astype(F32)[:, None] * inv[None, :]
    col_ang = (t % cfg.grid_w).astype(F32)[:, None] * inv[None, :]
    zeros = jnp.zeros_like(row_ang)
    cos = jnp.concatenate([jnp.cos(row_ang)] * 2 + [jnp.cos(col_ang)] * 2, axis=1)
    sa = jnp.concatenate([-jnp.sin(row_ang), zeros, -jnp.sin(col_ang), zeros], axis=1)
    sb = jnp.concatenate([zeros, jnp.sin(row_ang), zeros, jnp.sin(col_ang)], axis=1)
    ident = jnp.zeros((cfg.tm_in, hd), F32)
    return (jnp.concatenate([ident + 1.0, cos], axis=0),
            jnp.concatenate([ident, sa], axis=0),
            jnp.concatenate([ident, sb], axis=0))


def _forward(cfg, x_prompt, x_sample, cache_k, cache_v, state_fwd_re, state_fwd_im, state_bwd_re,
             state_bwd_im, c, c_ctx, norm_g, w_mod, b_mod, w_in, q_norm, k_norm, lam_re, lam_im,
             log_step, b_re, b_im, c_re, c_im, d_skip, w_glu, w_fft, w_out, final_norm_g):
    d, aw, kvw = cfg.d_model, cfg.attn_width, cfg.kv_width
    G, P = cfg.n_groups, cfg.ssm_state

    w_in_p = jnp.concatenate([w_in[:, :, aw:aw + 2 * kvw], w_in[:, :, :aw], w_in[:, :, aw + 2 * kvw:]],
                             axis=2).astype(BF16)
    w_out_b = w_out.astype(BF16)
    w_glu_b = w_glu.astype(BF16)
    w_fft_b = w_fft.astype(BF16)
    gains = jnp.stack([k_norm, q_norm], axis=1).astype(F32)
    ssm_w = _ssm_weights(lam_re, lam_im, log_step, b_re, b_im, c_re, c_im, d_skip, cfg)
    rope_tabs = _rope_tables(cfg)

    cc, sc = _dft_cos_sin(cfg.fft_group)
    cs = jnp.concatenate([cc, sc], axis=1).astype(BF16)
    dls = {}
    for length in {cfg.seq, cfg.dec_seq}:
        cl, sl = _dft_cos_sin(length)
        dls[length] = jnp.concatenate([cl, -sl], axis=1).astype(BF16)

    def st(re, im):
        re = re.astype(F32).transpose(1, 2, 0, 3)
        im = im.astype(F32).transpose(1, 2, 0, 3)
        return jnp.concatenate([re, im], axis=-1), jnp.concatenate([im, re], axis=-1)

    h0 = jnp.stack(st(state_fwd_re, state_fwd_im) + st(state_bwd_re, state_bwd_im), axis=2)

    nrow = 1 + cfg.dec_batch
    cvecs = jnp.concatenate([c_ctx[None, :], c], axis=0).astype(F32)
    cvecs = jnp.pad(cvecs, ((0, -nrow % 8), (0, 0)))
    mod = _modulation(cvecs, w_mod, b_mod, cfg)[:, :nrow].reshape(cfg.depth, nrow, 3, d)

    cache_k4 = cache_k.reshape(cfg.dec_batch, cfg.depth, cfg.past_len, kvw)
    cache_v4 = cache_v.reshape(cfg.dec_batch, cfg.depth, cfg.past_len, kvw)

    x = jnp.concatenate([x_prompt.reshape(cfg.p_tokens, d), x_sample.reshape(cfg.s_tokens, d)], axis=0)
    ks, vs, fins = [], [], []
    ucol = 2 * aw
    for l in range(cfg.depth):
        kv, main = _in_proj(x, mod[l], norm_g[l][None, :], w_in_p[l], gains[l], rope_tabs, cfg)
        attn = _attn_context(main, kv, cfg)
        attn = _attn_latent(attn, main, kv, cache_k4, cache_v4, l, cfg)
        u_r = _ssm_rows_in(main[:, ucol:ucol + cfg.ssm_width], cfg)
        y_r, fin = _ssm_scan(u_r, tuple(w[l] for w in ssm_w), h0[l], cfg)
        ssm = _ssm_glu(_ssm_rows_out(y_r, cfg), w_glu_b[l], main, cfg)
        four = _fourier(None, main, dls[cfg.seq], cs, w_fft_b[l], cfg,
                        length=cfg.seq, nbatch=cfg.batch, row0=0)
        four = _fourier(four, main, dls[cfg.dec_seq], cs, w_fft_b[l], cfg,
                        length=cfg.dec_seq, nbatch=cfg.dec_batch, row0=cfg.p_tokens)
        x = _out_proj(attn, ssm, four, w_out_b[l], x, mod[l], cfg)
        ks.append(kv[:cfg.p_tokens, :kvw].reshape(cfg.batch, cfg.seq, cfg.n_kv, cfg.head_dim))
        vs.append(kv[:cfg.p_tokens, kvw:].reshape(cfg.batch, cfg.seq, cfg.n_kv, cfg.head_dim))
        fins.append(fin)

    g = final_norm_g[None, :].astype(F32)
    y_prompt = _final_norm(x, g, cfg, row0=0, nrows=cfg.p_tokens).reshape(cfg.batch, cfg.seq, d)
    y_sample = _final_norm(x, g, cfg, row0=cfg.p_tokens, nrows=cfg.s_tokens).reshape(
        cfg.dec_batch, cfg.dec_seq, d)
    fin = jnp.stack(fins, axis=0)
    fin = fin.transpose(3, 0, 1, 2, 4)
    return (y_prompt, y_sample, jnp.stack(ks, axis=1), jnp.stack(vs, axis=1),
            fin[:, :, :, 0, :P], fin[:, :, :, 0, P:], fin[:, :, :, 1, :P], fin[:, :, :, 1, P:])


def kernel(x_prompt, x_sample, cache_k, cache_v, state_fwd_re, state_fwd_im, state_bwd_re, state_bwd_im,
           c, c_ctx, norm_g, w_mod, b_mod, w_in, q_norm, k_norm, lam_re, lam_im, log_step,
           b_re, b_im, c_re, c_im, d_skip, w_glu, w_fft, w_out, final_norm_g):
    return _forward(Cfg(), x_prompt, x_sample, cache_k, cache_v, state_fwd_re, state_fwd_im,
                    state_bwd_re, state_bwd_im, c, c_ctx, norm_g, w_mod, b_mod, w_in, q_norm, k_norm,
                    lam_re, lam_im, log_step, b_re, b_im, c_re, c_im, d_skip, w_glu, w_fft, w_out,
                    final_norm_g)
```

```python
import functools
import math
from typing import NamedTuple

import jax
import jax.numpy as jnp
from jax import lax
from jax.experimental import pallas as pl
from jax.experimental.pallas import tpu as pltpu

F32 = jnp.float32
BF16 = jnp.bfloat16
NORM_EPS = 1e-6
ROPE_THETA = 10000.0
LANES = 128
SSM_CHUNK = 16
VMEM_LIMIT = 56 * 1024 * 1024
LOG2E = 1.4426950408889634
HIGHEST = lax.Precision.HIGHEST


class Cfg(NamedTuple):
    d_model: int = 4096
    batch: int = 32
    seq: int = 256
    depth: int = 4
    dec_batch: int = 2
    dec_seq: int = 4096
    past_len: int = 512
    grid_w: int = 64
    head_dim: int = 128
    n_heads: int = 16
    n_kv: int = 4
    ssm_width: int = 1024
    ssm_group: int = 16
    ssm_state: int = 64
    fft_width: int = 1024
    fft_group: int = 256
    tm_in: int = 512
    tm_out: int = 512
    tn_out: int = 1024
    tn_mod: int = 512
    tq: int = 256
    tm_fft: int = 256
    rc_fft: int = 512
    tm_post: int = 512
    n_k: int = 16

    @property
    def attn_width(self):
        return self.n_heads * self.head_dim

    @property
    def kv_width(self):
        return self.n_kv * self.head_dim

    @property
    def q_per_kv(self):
        return self.n_heads // self.n_kv

    @property
    def n_groups(self):
        return self.ssm_width // self.ssm_group

    @property
    def tn_in(self):
        return 2 * self.kv_width

    @property
    def main_width(self):
        return 2 * self.attn_width + 2 * self.ssm_width + 2 * self.fft_width

    @property
    def p_tokens(self):
        return self.batch * self.seq

    @property
    def s_tokens(self):
        return self.dec_batch * self.dec_seq

    @property
    def tokens(self):
        return self.p_tokens + self.s_tokens


def _params(*sem):
    return pltpu.CompilerParams(dimension_semantics=sem, vmem_limit_bytes=VMEM_LIMIT)


def _silu(x):
    return x / (1.0 + jnp.exp(-x))


def _row_group(i, cfg, tm):
    npb = cfg.p_tokens // tm
    per = cfg.dec_seq // tm
    return jnp.where(i < npb, 0, 1 + (i - npb) // per)


def _mod_kernel(c_ref, w_ref, b_ref, o_ref):
    s = _silu(c_ref[...]).astype(BF16)
    o_ref[0] = jnp.dot(s, w_ref[0].astype(BF16), preferred_element_type=F32) + b_ref[0]


def _modulation(cvecs, w_mod, b_mod, cfg):
    d, n = cfg.d_model, 3 * cfg.d_model
    nrow = cvecs.shape[0]
    return pl.pallas_call(
        _mod_kernel,
        grid=(cfg.depth, n // cfg.tn_mod),
        in_specs=[
            pl.BlockSpec((nrow, d), lambda l, j: (0, 0)),
            pl.BlockSpec((1, d, cfg.tn_mod), lambda l, j: (l, 0, j)),
            pl.BlockSpec((1, 1, cfg.tn_mod), lambda l, j: (l, 0, j)),
        ],
        out_specs=pl.BlockSpec((1, nrow, cfg.tn_mod), lambda l, j: (l, 0, j)),
        out_shape=jax.ShapeDtypeStruct((cfg.depth, nrow, n), F32),
        compiler_params=_params("parallel", "parallel"),
        name="modulation",
    )(cvecs, w_mod, b_mod.reshape(cfg.depth, 1, n))


def _in_proj_kernel(*refs, cfg, kinds, has_prev):
    if has_prev:
        refs = refs[2:]
    (x_ref, mod_ref, ng_ref, w_ref, gains_ref, cos_ref, sa_ref, sb_ref,
     kv_ref, main_ref, h_scr) = refs
    j = pl.program_id(1)
    tm = x_ref.shape[0]
    hd = cfg.head_dim
    rc = min(tm, 16)

    @pl.when(j == 0)
    def _():
        shift = mod_ref[0:1, :]
        scale1 = 1.0 + mod_ref[1:2, :]
        g = ng_ref[...]

        def body(r, carry):
            rows = pl.ds(pl.multiple_of(r * rc, rc), rc)
            x = x_ref[rows, :]
            ms = jnp.mean(x * x, axis=-1, keepdims=True)
            y = x * lax.rsqrt(ms + NORM_EPS) * g
            h_scr[rows, :] = (y * scale1 + shift).astype(BF16)
            return carry

        lax.fori_loop(0, tm // rc, body, 0, unroll=min(4, tm // rc))

    tn = w_ref.shape[1]
    sub = 2 * hd

    def sub_dots():
        for s in range(tn // sub):
            yield s * sub, jnp.dot(h_scr[...], w_ref[:, s * sub:(s + 1) * sub],
                                   preferred_element_type=F32)

    def head_norm_rope(a, gain):
        ms = jnp.mean(a * a, axis=-1, keepdims=True)
        y = a * lax.rsqrt(ms + NORM_EPS) * gain
        return (y * cos_ref[...] + pltpu.roll(y, hd - hd // 4, 1) * sa_ref[...]
                + pltpu.roll(y, hd // 4, 1) * sb_ref[...])

    q_lo, q_hi = kinds["q"]

    @pl.when((j >= q_lo) & (j < q_hi))
    def _():
        qg = gains_ref[1:2, :]
        qs = hd ** -0.5 * LOG2E
        for c0, acc in sub_dots():
            for h in range(sub // hd):
                main_ref[:, c0 + h * hd:c0 + (h + 1) * hd] = (
                    head_norm_rope(acc[:, h * hd:(h + 1) * hd], qg) * qs).astype(BF16)

    @pl.when(j == kinds["kv"])
    def _():
        kg = gains_ref[0:1, :]
        for c0, acc in sub_dots():
            for h in range(sub // hd):
                a = acc[:, h * hd:(h + 1) * hd]
                is_key = c0 + h * hd < cfg.kv_width
                kv_ref[:, c0 + h * hd:c0 + (h + 1) * hd] = head_norm_rope(a, kg) if is_key else a

    silu_pred = functools.reduce(
        lambda a, b: a | b, [(j >= lo) & (j < hi) for lo, hi in kinds["silu"]])
    plain_pred = functools.reduce(
        lambda a, b: a | b, [(j >= lo) & (j < hi) for lo, hi in kinds["plain"]])

    @pl.when(silu_pred)
    def _():
        for c0, acc in sub_dots():
            main_ref[:, c0:c0 + sub] = _silu(acc).astype(BF16)

    @pl.when(plain_pred)
    def _():
        for c0, acc in sub_dots():
            main_ref[:, c0:c0 + sub] = acc.astype(BF16)


def _tile_kinds(cfg):
    tn = cfg.tn_in
    widths = [("q", cfg.attn_width), ("kv", 2 * cfg.kv_width), ("silu", cfg.attn_width),
              ("plain", cfg.ssm_width), ("silu", cfg.ssm_width), ("plain", cfg.fft_width),
              ("silu", cfg.fft_width)]
    kinds = {"q": None, "kv": None, "silu": [], "plain": []}
    j = 0
    for name, w in widths:
        assert w % tn == 0
        rng = (j, j + w // tn)
        j += w // tn
        if name == "q":
            kinds["q"] = rng
        elif name == "kv":
            kinds["kv"] = rng[0]
        else:
            kinds[name].append(rng)
    return kinds, j


def _in_proj(x, row0, prev, mod, norm_g, w_in_b, gains, rope_tabs, layer, cfg):
    tm, tn, d = cfg.tm_in, cfg.tn_in, cfg.d_model
    kinds, nj = _tile_kinds(cfg)
    kvj = kinds["kv"]
    npb = cfg.p_tokens // tm
    per = cfg.dec_seq // tm
    blk0 = row0 // tm
    cos_t, sa_t, sb_t = rope_tabs

    def tab_map(i, j):
        gi = i + blk0
        return (jnp.where(gi < npb, 0, 1 + (gi - npb) % per), 0)

    tab_spec = pl.BlockSpec((tm, cfg.head_dim), tab_map)
    in_specs = [
        pl.BlockSpec((tm, d), lambda i, j: (i, 0)),
        pl.BlockSpec((None, None, 3, d), lambda i, j: (layer, _row_group(i + blk0, cfg, tm), 0, 0)),
        pl.BlockSpec((None, 1, d), lambda i, j: (layer, 0, 0)),
        pl.BlockSpec((None, d, tn), lambda i, j: (layer, 0, j)),
        pl.BlockSpec((None, 2, cfg.head_dim), lambda i, j: (layer, 0, 0)),
        tab_spec, tab_spec, tab_spec,
    ]
    args = [x, mod, norm_g, w_in_b, gains, cos_t, sa_t, sb_t]
    aliases = {}
    if prev is not None:
        in_specs = [pl.BlockSpec(memory_space=pl.ANY)] * 2 + in_specs
        args = list(prev) + args
        aliases = {0: 0, 1: 1}
    return pl.pallas_call(
        functools.partial(_in_proj_kernel, cfg=cfg, kinds=kinds, has_prev=prev is not None),
        grid=(x.shape[0] // tm, nj),
        in_specs=in_specs,
        out_specs=[
            pl.BlockSpec((tm, tn), lambda i, j: (i + blk0, 0)),
            pl.BlockSpec((tm, tn), lambda i, j: (i + blk0, jnp.where(j < kvj, j, jnp.maximum(j - 1, kvj - 1)))),
        ],
        out_shape=[
            jax.ShapeDtypeStruct((cfg.tokens, tn), F32),
            jax.ShapeDtypeStruct((cfg.tokens, cfg.main_width), BF16),
        ],
        scratch_shapes=[pltpu.VMEM((tm, d), BF16)],
        input_output_aliases=aliases,
        compiler_params=_params("parallel", "arbitrary"),
        name="in_proj",
    )(*args)


def _softmax_attend(q, k, v1):
    hd = q.shape[1]
    s = lax.dot_general(q, k, (((1,), (1,)), ((), ())), preferred_element_type=F32)
    m = jnp.max(s, axis=-1, keepdims=True)
    p = jnp.exp2(s - m).astype(BF16)
    o = jnp.dot(p, v1, preferred_element_type=F32)
    return o[:, :hd] / o[:, hd:]


def _attn_ctx_kernel(q_ref, k_ref, v_ref, g_ref, o_ref, *, cfg):
    hd = cfg.head_dim
    k = k_ref[...].astype(BF16)
    v = v_ref[...].astype(BF16)
    v1 = jnp.concatenate([v, jnp.ones_like(v)], axis=1)
    for h in range(cfg.q_per_kv):
        sl = slice(h * hd, (h + 1) * hd)
        o = _softmax_attend(q_ref[:, sl], k, v1)
        o_ref[:, sl] = (o * g_ref[:, sl].astype(F32)).astype(BF16)


def _attn_context(main, kv, cfg):
    hd, nkv = cfg.head_dim, cfg.n_kv
    qw = cfg.q_per_kv * hd
    return pl.pallas_call(
        functools.partial(_attn_ctx_kernel, cfg=cfg),
        grid=(cfg.batch, nkv),
        in_specs=[
            pl.BlockSpec((cfg.seq, qw), lambda b, h: (b, h)),
            pl.BlockSpec((cfg.seq, hd), lambda b, h: (b, h)),
            pl.BlockSpec((cfg.seq, hd), lambda b, h: (b, nkv + h)),
            pl.BlockSpec((cfg.seq, qw), lambda b, h: (b, nkv + h)),
        ],
        out_specs=pl.BlockSpec((cfg.seq, qw), lambda b, h: (b, h)),
        out_shape=jax.ShapeDtypeStruct((cfg.tokens, cfg.attn_width), BF16),
        compiler_params=_params("parallel", "parallel"),
        name="attn_context",
    )(main, kv, kv, main)


def _attn_lat_kernel(prev_ref, q_ref, ck_ref, cv_ref, kn_ref, vn_ref, g_ref, o_ref,
                     k_scr, v_scr, *, cfg):
    del prev_ref
    hd, past = cfg.head_dim, cfg.past_len

    @pl.when(pl.program_id(2) == 0)
    def _():
        k_scr[0:past, :] = ck_ref[...].astype(BF16)
        k_scr[past:, :] = kn_ref[...].astype(BF16)
        v_scr[0:past, 0:hd] = cv_ref[...].astype(BF16)
        v_scr[past:, 0:hd] = vn_ref[...].astype(BF16)
        v_scr[:, hd:] = jnp.ones((v_scr.shape[0], hd), BF16)

    k = k_scr[...]
    v1 = v_scr[...]
    for h in range(cfg.q_per_kv):
        sl = slice(h * hd, (h + 1) * hd)
        o = _softmax_attend(q_ref[:, sl], k, v1)
        o_ref[:, sl] = (o * g_ref[:, sl].astype(F32)).astype(BF16)


def _attn_latent(attn_prev, main, kv, cache_k, cache_v, layer, cfg):
    hd, nkv, tq = cfg.head_dim, cfg.n_kv, cfg.tq
    qw = cfg.q_per_kv * hd
    nqb = cfg.dec_seq // tq
    row0 = cfg.p_tokens // tq
    kvrow0 = cfg.p_tokens // cfg.dec_seq
    nkeys = cfg.past_len + cfg.dec_seq

    def q_map(b, h, qi):
        return (row0 + b * nqb + qi, h)

    def g_map(b, h, qi):
        return (row0 + b * nqb + qi, nkv + h)

    cache_spec = pl.BlockSpec((None, None, cfg.past_len, hd), lambda b, h, qi: (b, layer, 0, h))
    return pl.pallas_call(
        functools.partial(_attn_lat_kernel, cfg=cfg),
        grid=(cfg.dec_batch, nkv, nqb),
        in_specs=[
            pl.BlockSpec(memory_space=pl.ANY),
            pl.BlockSpec((tq, qw), q_map),
            cache_spec, cache_spec,
            pl.BlockSpec((cfg.dec_seq, hd), lambda b, h, qi: (kvrow0 + b, h)),
            pl.BlockSpec((cfg.dec_seq, hd), lambda b, h, qi: (kvrow0 + b, nkv + h)),
            pl.BlockSpec((tq, qw), g_map),
        ],
        out_specs=pl.BlockSpec((tq, qw), q_map),
        out_shape=jax.ShapeDtypeStruct((cfg.tokens, cfg.attn_width), BF16),
        scratch_shapes=[pltpu.VMEM((nkeys, hd), BF16), pltpu.VMEM((nkeys, 2 * hd), BF16)],
        input_output_aliases={0: 0},
        compiler_params=_params("parallel", "parallel", "arbitrary"),
        name="attn_latent",
    )(attn_prev, main, cache_k, cache_v, kv, kv, main)


def _ssm_kernel(u_ref, perm_ref, catf_ref, catb_ref, bbt_ref, w1s_ref, w2_ref, av_ref,
                pw_ref, dsk_ref, h0_ref, y_ref, fin_ref,
                uf_scr, xcat_scr, u8_scr, m_scr, s_scr, hin_scr, f_scr, g_scr, *, cfg):
    T, C = SSM_CHUNK, cfg.ssm_group
    tc = T * C
    st = 2 * cfg.ssm_state
    n_k = cfg.n_k
    chain = T * n_k
    nc = cfg.p_tokens // chain
    n_sc = cfg.dec_seq // chain
    gt = LANES // C
    jl_n = LANES // C
    jh_n = T // jl_n
    is_latent = pl.program_id(1) == 1
    y8_scr = xcat_scr

    uf_scr[...] = u_ref[...].astype(F32)

    def gather(k, carry):
        for j in range(T):
            piece = uf_scr[pl.ds(k * T + j, nc, stride=chain), :].astype(BF16)
            xcat_scr[j // jl_n, pl.ds(pl.multiple_of(k * nc, nc), nc),
                     (j % jl_n) * LANES:(j % jl_n + 1) * LANES] = piece
        return carry

    lax.fori_loop(0, n_k, gather, 0)
    for jh in range(jh_n):
        t = jnp.dot(xcat_scr[jh], perm_ref[...], preferred_element_type=F32)
        for r in range(gt):
            u8_scr[r, :, jh * LANES:(jh + 1) * LANES] = t[:, r * LANES:(r + 1) * LANES].astype(BF16)

    def cmul(h, hs, c1, c2):
        return h * c1 + hs * c2, hs * c1 - h * c2

    lane = lax.broadcasted_iota(jnp.int32, (C, tc), 1)

    for r in range(gt):
        ktf = jnp.dot(bbt_ref[r, 0], catf_ref[r], preferred_element_type=F32, precision=HIGHEST)
        ktb = jnp.dot(bbt_ref[r, 1], catb_ref[r], preferred_element_type=F32, precision=HIGHEST)
        for j in range(T):
            rf = pltpu.roll(ktf, C * j, 1) if j else ktf
            rf = jnp.where(lane >= C * j, rf, 0.0)
            sh = (T - 1 - j) * C
            rb = pltpu.roll(ktb, tc - sh, 1) if sh else ktb
            rb = jnp.where(lane < tc - sh, rb, 0.0)
            m_scr[j * C:(j + 1) * C, :] = (rf + rb).astype(BF16)

        u = u8_scr[r]
        y_intra = jnp.dot(u, m_scr[...], preferred_element_type=F32)
        s_scr[...] = jnp.dot(u, w1s_ref[r], preferred_element_type=F32)
        av = av_ref[r]
        a1 = (av[0:1], av[2:3])
        a2 = (av[1:2], av[3:4])
        b1 = (av[4:5], av[6:7])
        b2 = (av[5:6], av[7:8])

        def level1(k, carry):
            hf, hfs, hb, hbs = carry
            rf_ = pl.ds(pl.multiple_of(k * nc, nc), nc)
            rb_ = pl.ds(pl.multiple_of((n_k - 1 - k) * nc, nc), nc)
            hin_scr[rf_, 0:st] = hf
            hin_scr[rf_, st:2 * st] = hfs
            hin_scr[rb_, 2 * st:3 * st] = hb
            hin_scr[rb_, 3 * st:4 * st] = hbs
            nf, nfs = cmul(hf, hfs, a1[0], a2[0])
            nbk, nbs = cmul(hb, hbs, a1[1], a2[1])
            return (nf + s_scr[rf_, 0:st], nfs + s_scr[rf_, st:2 * st],
                    nbk + s_scr[rb_, 2 * st:3 * st], nbs + s_scr[rb_, 3 * st:4 * st])

        z = jnp.zeros((nc, st), F32)
        ff, ffs, fb, fbs = lax.fori_loop(0, n_k, level1, (z, z, z, z))

        @pl.when(jnp.logical_not(is_latent))
        def _():
            fin_ref[r, 0] = ff
            fin_ref[r, 1] = fb

        @pl.when(is_latent)
        def _():
            for idx, val in enumerate((ff, ffs, fb, fbs)):
                f_scr[idx] = val
            h0 = h0_ref[r]
            gf, gfs, gb, gbs = h0[0], h0[1], h0[2], h0[3]
            for sc in range(n_sc):
                sel = pl.ds(sc, cfg.dec_batch, stride=n_sc)
                g_scr[0, sel, :] = gf
                g_scr[1, sel, :] = gfs
                nf, nfs = cmul(gf, gfs, b1[0], b2[0])
                gf = nf + f_scr[0, sel, :]
                gfs = nfs + f_scr[1, sel, :]
            for sc in range(n_sc - 1, -1, -1):
                sel = pl.ds(sc, cfg.dec_batch, stride=n_sc)
                g_scr[2, sel, :] = gb
                g_scr[3, sel, :] = gbs
                nbk, nbs = cmul(gb, gbs, b1[1], b2[1])
                gb = nbk + f_scr[2, sel, :]
                gbs = nbs + f_scr[3, sel, :]
            gfa, gfsa, gba, gbsa = g_scr[0], g_scr[1], g_scr[2], g_scr[3]
            for k in range(n_k):
                rows_k = slice(k * nc, (k + 1) * nc)
                pf = pw_ref[r, k]
                cf, cfs = cmul(gfa, gfsa, pf[0:1], pf[1:2])
                cb, cbs = cmul(gba, gbsa, pf[2:3], pf[3:4])
                hin_scr[rows_k, 0:st] += cf
                hin_scr[rows_k, st:2 * st] += cfs
                hin_scr[rows_k, 2 * st:3 * st] += cb
                hin_scr[rows_k, 3 * st:4 * st] += cbs

        hsel = jnp.concatenate([hin_scr[:, 0:st], hin_scr[:, 2 * st:3 * st]], axis=1).astype(BF16)
        y = y_intra + jnp.dot(hsel, w2_ref[r], preferred_element_type=F32)
        y = (y + dsk_ref[r] * u.astype(F32)).astype(BF16)
        for jh in range(jh_n):
            y8_scr[jh, :, r * LANES:(r + 1) * LANES] = y[:, jh * LANES:(jh + 1) * LANES]

    for jh in range(jh_n):
        t = lax.dot_general(y8_scr[jh], perm_ref[...], (((1,), (1,)), ((), ())),
                            preferred_element_type=F32)
        for k in range(n_k):
            for jl in range(jl_n):
                uf_scr[pl.ds(k * T + jh * jl_n + jl, nc, stride=chain), :] = (
                    t[k * nc:(k + 1) * nc, jl * LANES:(jl + 1) * LANES])
    y_ref[...] = uf_scr[...].astype(BF16)


def _ssm_scan(main, ssm_w, h0, layer, cfg):
    perm, catf, catb, bbt, w1s, w2, av, pw, dsk = ssm_w
    T, C = SSM_CHUNK, cfg.ssm_group
    tc = T * C
    st = 2 * cfg.ssm_state
    gt = LANES // C
    nq = cfg.ssm_width // LANES
    chain = T * cfg.n_k
    half = cfg.p_tokens
    assert cfg.seq == chain and cfg.dec_seq % chain == 0 and T % gt == 0 and cfg.s_tokens == half
    nc = half // chain
    rows = nc * cfg.n_k
    pw_n = gt * LANES
    ucol = 2 * cfg.attn_width // LANES

    def wspec(*shape):
        nd = len(shape)
        return pl.BlockSpec((None, gt) + shape, lambda q, s: (layer, q) + (0,) * nd)

    return pl.pallas_call(
        functools.partial(_ssm_kernel, cfg=cfg),
        grid=(nq, 2),
        in_specs=[
            pl.BlockSpec((half, LANES), lambda q, s: (s, ucol + q)),
            pl.BlockSpec((pw_n, pw_n), lambda q, s: (0, 0)),
            wspec(st, tc), wspec(st, tc), wspec(2, C, st), wspec(tc, 4 * st), wspec(2 * st, tc),
            wspec(8, st), wspec(cfg.n_k, 4, st), wspec(1, tc), wspec(4, cfg.dec_batch, st),
        ],
        out_specs=[
            pl.BlockSpec((half, LANES), lambda q, s: (s, q)),
            pl.BlockSpec((gt, 2, cfg.batch, st), lambda q, s: (q, 0, 0, 0)),
        ],
        out_shape=[
            jax.ShapeDtypeStruct((cfg.tokens, cfg.ssm_width), BF16),
            jax.ShapeDtypeStruct((cfg.n_groups, 2, cfg.batch, st), F32),
        ],
        scratch_shapes=[
            pltpu.VMEM((half, LANES), F32),
            pltpu.VMEM((T // gt, rows, pw_n), BF16),
            pltpu.VMEM((gt, rows, tc), BF16),
            pltpu.VMEM((tc, tc), BF16),
            pltpu.VMEM((rows, 4 * st), F32),
            pltpu.VMEM((rows, 4 * st), F32),
            pltpu.VMEM((4, nc, st), F32),
            pltpu.VMEM((4, nc, st), F32),
        ],
        compiler_params=_params("parallel", "arbitrary"),
        name="ssm_scan",
    )(main, perm, catf, catb, bbt, w1s, w2, av, pw, dsk, h0)


def _ssm_weights(lam_re, lam_im, log_step, b_re, b_im, c_re, c_im, d_skip, cfg):
    T, C, P, G, nk = SSM_CHUNK, cfg.ssm_group, cfg.ssm_state, cfg.n_groups, cfg.n_k
    lr, li = lam_re.astype(F32), lam_im.astype(F32)
    dt = jnp.exp(log_step.astype(F32))[..., None]

    def powers(js):
        jj = jnp.asarray(js, F32)[:, None]
        mag = jnp.exp(lr[..., None, :] * dt[..., None, :] * jj)
        ang = li[..., None, :] * dt[..., None, :] * jj
        return mag * jnp.cos(ang), mag * jnp.sin(ang)

    pr, pi = powers(range(T + 1))
    ab_re, ab_im = pr[..., 1, :], pi[..., 1, :]
    nr, ni = ab_re - 1.0, ab_im
    den = lr * lr + li * li
    f_re = (nr * lr + ni * li) / den
    f_im = (ni * lr - nr * li) / den
    br, bi = b_re.astype(F32), b_im.astype(F32)
    bb_re = f_re[..., None] * br - f_im[..., None] * bi
    bb_im = f_re[..., None] * bi + f_im[..., None] * br
    cr, ci = c_re.astype(F32), c_im.astype(F32)
    ca_re = cr[..., None, :, :] * pr[..., :, None, :] - ci[..., None, :, :] * pi[..., :, None, :]
    ca_im = cr[..., None, :, :] * pi[..., :, None, :] + ci[..., None, :, :] * pr[..., :, None, :]
    ii = jnp.arange(T)

    def state_out(d, lags):
        wr = ca_re[:, d][:, :, lags].transpose(0, 1, 4, 2, 3).reshape(-1, G, P, T * C)
        wi = -ca_im[:, d][:, :, lags].transpose(0, 1, 4, 2, 3).reshape(-1, G, P, T * C)
        return jnp.concatenate([wr, wi], axis=2)

    catf = state_out(0, ii)
    catb = state_out(1, T - 1 - ii)
    w2 = jnp.concatenate([state_out(0, ii + 1), state_out(1, T - ii)], axis=2).astype(BF16)
    bbt = jnp.concatenate([bb_re, bb_im], axis=3).transpose(0, 2, 1, 4, 3)

    def state_in(d, pidx):
        wr = pr[:, d][:, :, pidx][..., None] * bb_re[:, d][:, :, None] \
            - pi[:, d][:, :, pidx][..., None] * bb_im[:, d][:, :, None]
        wi = pr[:, d][:, :, pidx][..., None] * bb_im[:, d][:, :, None] \
            + pi[:, d][:, :, pidx][..., None] * bb_re[:, d][:, :, None]
        wr = wr.transpose(0, 1, 2, 4, 3).reshape(-1, G, T * C, P)
        wi = wi.transpose(0, 1, 2, 4, 3).reshape(-1, G, T * C, P)
        return jnp.concatenate([wr, wi, wi, wr], axis=-1)

    w1s = jnp.concatenate([state_in(0, T - 1 - ii), state_in(1, ii)], axis=-1).astype(BF16)

    def lanes(re, im):
        return jnp.concatenate([re, re], axis=-1), jnp.concatenate([-im, im], axis=-1)

    qr, qi = powers([T * k for k in range(nk + 1)])
    rows = []
    for src_r, src_i, idx in ((pr, pi, T), (qr, qi, nk)):
        for d in range(2):
            rows.extend(lanes(src_r[:, d, :, idx], src_i[:, d, :, idx]))
    av = jnp.stack(rows, axis=2)
    kidx = jnp.arange(nk)
    p1f, p2f = lanes(qr[:, 0][:, :, kidx], qi[:, 0][:, :, kidx])
    p1b, p2b = lanes(qr[:, 1][:, :, nk - 1 - kidx], qi[:, 1][:, :, nk - 1 - kidx])
    pw = jnp.stack([p1f, p2f, p1b, p2b], axis=3)
    dsk = jnp.tile(d_skip.astype(F32).reshape(-1, G, 1, C), (1, 1, 1, T))

    gt = LANES // C
    n = gt * LANES
    src = jnp.arange(n)
    jl, r, c = src // LANES, (src % LANES) // C, src % C
    dst = r * LANES + jl * C + c
    perm = (dst[:, None] == jnp.arange(n)[None, :]).astype(BF16)
    return perm, catf, catb, bbt, w1s, w2, av, pw, dsk


def _glu_kernel(y_ref, w_ref, g_ref, o_ref, *, cfg):
    z = jnp.dot(y_ref[...], w_ref[...], preferred_element_type=F32)
    sw = cfg.ssm_width
    a, g = z[:, :sw], z[:, sw:]
    o_ref[...] = (a / (1.0 + jnp.exp(-g)) * g_ref[...].astype(F32)).astype(BF16)


def _ssm_glu(y, w_glu_b, main, layer, cfg):
    tm, sw = cfg.tm_post, cfg.ssm_width
    gcol = (2 * cfg.attn_width + sw) // sw
    assert (2 * cfg.attn_width) % sw == 0
    return pl.pallas_call(
        functools.partial(_glu_kernel, cfg=cfg),
        grid=(cfg.tokens // tm,),
        in_specs=[
            pl.BlockSpec((tm, sw), lambda i: (i, 0)),
            pl.BlockSpec((None, sw, 2 * sw), lambda i: (layer, 0, 0)),
            pl.BlockSpec((tm, sw), lambda i: (i, gcol)),
        ],
        out_specs=pl.BlockSpec((tm, sw), lambda i: (i, 0)),
        out_shape=jax.ShapeDtypeStruct((cfg.tokens, sw), BF16),
        compiler_params=_params("parallel"),
        name="ssm_glu",
    )(y, w_glu_b, main)


def _fourier_kernel(prev_ref, x_ref, dl_ref, cs_ref, w_ref, g_ref, o_ref, z_scr, *, cfg, length):
    del prev_ref
    fg = cfg.fft_group
    rc = min(cfg.rc_fft, length)

    @pl.when(pl.program_id(1) == 0)
    def _():
        def body(ci, carry):
            rows = pl.ds(pl.multiple_of(ci * rc, rc), rc)
            rows2 = pl.ds(pl.multiple_of(length + ci * rc, rc), rc)
            for g in range(cfg.fft_width // fg):
                cols = slice(g * fg, (g + 1) * fg)
                t = jnp.dot(x_ref[rows, cols], cs_ref[...], preferred_element_type=F32)
                z_scr[rows, cols] = t[:, :fg].astype(BF16)
                z_scr[rows2, cols] = t[:, fg:].astype(BF16)
            return carry

        lax.fori_loop(0, length // rc, body, 0)

    mixed = jnp.dot(dl_ref[...], z_scr[...], preferred_element_type=F32).astype(BF16)
    four = jnp.dot(mixed, w_ref[...], preferred_element_type=F32)
    o_ref[...] = (four * g_ref[...].astype(F32)).astype(BF16)


def _fourier(prev, main, dl, cs, w_fft_b, layer, cfg, *, length, nbatch, row0):
    fw = cfg.fft_width
    tm = min(cfg.tm_fft, length)
    nr = length // tm
    xcol = (2 * cfg.attn_width + 2 * cfg.ssm_width) // fw
    assert (2 * cfg.attn_width + 2 * cfg.ssm_width) % fw == 0 and row0 % length == 0
    b0 = row0 // length
    t0 = row0 // tm

    def tile_map(b, r):
        return (t0 + b * nr + r, 0)

    args = [main, dl, cs, w_fft_b, main]
    in_specs = [
        pl.BlockSpec((length, fw), lambda b, r: (b0 + b, xcol)),
        pl.BlockSpec((tm, 2 * length), lambda b, r: (r, 0)),
        pl.BlockSpec((cfg.fft_group, 2 * cfg.fft_group), lambda b, r: (0, 0)),
        pl.BlockSpec((None, fw, fw), lambda b, r: (layer, 0, 0)),
        pl.BlockSpec((tm, fw), lambda b, r: (t0 + b * nr + r, xcol + 1)),
    ]
    kern = functools.partial(_fourier_kernel, cfg=cfg, length=length)
    aliases = {}
    if prev is None:
        kern = functools.partial(kern, None)
    else:
        args = [prev] + args
        in_specs = [pl.BlockSpec(memory_space=pl.ANY)] + in_specs
        aliases = {0: 0}
    return pl.pallas_call(
        kern,
        grid=(nbatch, nr),
        in_specs=in_specs,
        out_specs=pl.BlockSpec((tm, fw), tile_map),
        out_shape=jax.ShapeDtypeStruct((cfg.tokens, fw), BF16),
        scratch_shapes=[pltpu.VMEM((2 * length, fw), BF16)],
        input_output_aliases=aliases,
        compiler_params=_params("parallel", "arbitrary"),
        name=f"fourier_{length}",
    )(*args)


def _dft_cos_sin(n):
    k = jnp.arange(n, dtype=jnp.int32)
    w = 2.0 * math.pi / n

    def cs(j):
        ang = ((j[:, None] * k[None, :]) % n).astype(F32) * w
        return jnp.cos(ang), jnp.sin(ang)

    s = 1
    while s * s < n:
        s *= 2
    if n <= 512 or n % s:
        c, sn = cs(k)
    else:
        c1, s1 = cs(jnp.arange(n // s, dtype=jnp.int32) * s)
        c0, s0 = cs(jnp.arange(s, dtype=jnp.int32))
        c = (c1[:, None] * c0[None] - s1[:, None] * s0[None]).reshape(n, n)
        sn = (s1[:, None] * c0[None] + c1[:, None] * s0[None]).reshape(n, n)
    scale = n ** -0.5
    return c * scale, sn * scale


def _out_proj_kernel(a_ref, s_ref, f_ref, w_ref, xa_ref, xb_ref, mod_ref, o_ref, *, cfg, npb):
    aw, sw = cfg.attn_width, cfg.ssm_width
    acc = jnp.dot(a_ref[...], w_ref[0:aw, :], preferred_element_type=F32)
    acc += jnp.dot(s_ref[...], w_ref[aw:aw + sw, :], preferred_element_type=F32)
    acc += jnp.dot(f_ref[...], w_ref[aw + sw:, :], preferred_element_type=F32)
    upd = mod_ref[2:3, :] * acc
    i = pl.program_id(1)

    @pl.when(i < npb)
    def _():
        o_ref[...] = xa_ref[...] + upd

    @pl.when(i >= npb)
    def _():
        o_ref[...] = xb_ref[...] + upd


def _out_proj(attn, ssm, four, w_out_b, xa, xb, xb_blk0, mod, layer, cfg):
    tm, tn, d = cfg.tm_out, cfg.tn_out, cfg.d_model
    npb = cfg.p_tokens // tm
    return pl.pallas_call(
        functools.partial(_out_proj_kernel, cfg=cfg, npb=npb),
        grid=(d // tn, cfg.tokens // tm),
        in_specs=[
            pl.BlockSpec((tm, cfg.attn_width), lambda j, i: (i, 0)),
            pl.BlockSpec((tm, cfg.ssm_width), lambda j, i: (i, 0)),
            pl.BlockSpec((tm, cfg.fft_width), lambda j, i: (i, 0)),
            pl.BlockSpec((None, d, tn), lambda j, i: (layer, 0, j)),
            pl.BlockSpec((tm, tn), lambda j, i: (jnp.minimum(i, npb - 1), j)),
            pl.BlockSpec((tm, tn), lambda j, i: (jnp.maximum(i, npb) - npb + xb_blk0, j)),
            pl.BlockSpec((None, None, 3, tn), lambda j, i: (layer, _row_group(i, cfg, tm), 0, j)),
        ],
        out_specs=pl.BlockSpec((tm, tn), lambda j, i: (i, j)),
        out_shape=jax.ShapeDtypeStruct((cfg.tokens, d), F32),
        compiler_params=_params("parallel", "arbitrary"),
        name="out_proj",
    )(attn, ssm, four, w_out_b, xa, xb, mod)


def _final_norm_kernel(x_ref, g_ref, o_ref):
    x = x_ref[...]
    ms = jnp.mean(x * x, axis=-1, keepdims=True)
    o_ref[...] = x * lax.rsqrt(ms + NORM_EPS) * g_ref[...]


def _final_norm(x, g, cfg, *, row0, nrows):
    tm, d = 256, cfg.d_model
    tm = min(tm, nrows)
    assert row0 % tm == 0
    return pl.pallas_call(
        _final_norm_kernel,
        grid=(nrows // tm,),
        in_specs=[pl.BlockSpec((tm, d), lambda i: (row0 // tm + i, 0)),
                  pl.BlockSpec((1, d), lambda i: (0, 0))],
        out_specs=pl.BlockSpec((tm, d), lambda i: (i, 0)),
        out_shape=jax.ShapeDtypeStruct((nrows, d), F32),
        compiler_params=_params("parallel"),
        name="final_norm",
    )(x, g)


def _rope_tables(cfg):
    hd = cfg.head_dim
    pairs = hd // 4
    t = jnp.arange(cfg.dec_seq)
    inv = ROPE_THETA ** (-jnp.arange(pairs, dtype=F32) / pairs)
    row_ang = (t // cfg.grid_w).astype(F32)[:, None] * inv[None, :]
    col_ang = (t % cfg.grid_w).astype(F32)[:, None] * inv[None, :]
    zeros = jnp.zeros_like(row_ang)
    cos = jnp.concatenate([jnp.cos(row_ang)] * 2 + [jnp.cos(col_ang)] * 2, axis=1)
    sa = jnp.concatenate([-jnp.sin(row_ang), zeros, -jnp.sin(col_ang), zeros], axis=1)
    sb = jnp.concatenate([zeros, jnp.sin(row_ang), zeros, jnp.sin(col_ang)], axis=1)
    ident = jnp.zeros((cfg.tm_in, hd), F32)
    return (jnp.concatenate([ident + 1.0, cos], axis=0),
            jnp.concatenate([ident, sa], axis=0),
            jnp.concatenate([ident, sb], axis=0))


def _forward(cfg, x_prompt, x_sample, cache_k, cache_v, state_fwd_re, state_fwd_im, state_bwd_re,
             state_bwd_im, c, c_ctx, norm_g, w_mod, b_mod, w_in, q_norm, k_norm, lam_re, lam_im,
             log_step, b_re, b_im, c_re, c_im, d_skip, w_glu, w_fft, w_out, final_norm_g):
    d, kvw = cfg.d_model, cfg.kv_width
    P = cfg.ssm_state

    w_in_b = w_in.astype(BF16)
    w_out_b = w_out.astype(BF16)
    w_glu_b = w_glu.astype(BF16)
    w_fft_b = w_fft.astype(BF16)
    gains = jnp.stack([k_norm, q_norm], axis=1).astype(F32)
    norm_g3 = norm_g.astype(F32)[:, None, :]
    ssm_w = _ssm_weights(lam_re, lam_im, log_step, b_re, b_im, c_re, c_im, d_skip, cfg)
    rope_tabs = _rope_tables(cfg)

    cc, sc = _dft_cos_sin(cfg.fft_group)
    cs = jnp.concatenate([cc, sc], axis=1).astype(BF16)
    dls = {}
    for length in {cfg.seq, cfg.dec_seq}:
        cl, sl = _dft_cos_sin(length)
        dls[length] = jnp.concatenate([cl, -sl], axis=1).astype(BF16)

    def st(re, im):
        re = re.astype(F32).transpose(1, 2, 0, 3)
        im = im.astype(F32).transpose(1, 2, 0, 3)
        return jnp.concatenate([re, im], axis=-1), jnp.concatenate([im, re], axis=-1)

    h0 = jnp.stack(st(state_fwd_re, state_fwd_im) + st(state_bwd_re, state_bwd_im), axis=2)

    nrow = 1 + cfg.dec_batch
    cvecs = jnp.concatenate([c_ctx[None, :], c], axis=0).astype(F32)
    cvecs = jnp.pad(cvecs, ((0, -nrow % 8), (0, 0)))
    mod = _modulation(cvecs, w_mod, b_mod, cfg).reshape(cfg.depth, cvecs.shape[0], 3, d)

    cache_k4 = cache_k.reshape(cfg.dec_batch, cfg.depth, cfg.past_len, kvw)
    cache_v4 = cache_v.reshape(cfg.dec_batch, cfg.depth, cfg.past_len, kvw)

    xp = x_prompt.reshape(cfg.p_tokens, d)
    xs = x_sample.reshape(cfg.s_tokens, d)
    x = None
    ks, vs, fins = [], [], []
    for l in range(cfg.depth):
        ip = functools.partial(_in_proj, mod=mod, norm_g=norm_g3, w_in_b=w_in_b, gains=gains,
                               rope_tabs=rope_tabs, layer=l, cfg=cfg)
        if l == 0:
            kv, main = ip(xs, cfg.p_tokens, ip(xp, 0, None))
        else:
            kv, main = ip(x, 0, None)
        attn = _attn_context(main, kv, cfg)
        attn = _attn_latent(attn, main, kv, cache_k4, cache_v4, l, cfg)
        y, fin = _ssm_scan(main, ssm_w, h0, l, cfg)
        ssm = _ssm_glu(y, w_glu_b, main, l, cfg)
        four = _fourier(None, main, dls[cfg.seq], cs, w_fft_b, l, cfg,
                        length=cfg.seq, nbatch=cfg.batch, row0=0)
        four = _fourier(four, main, dls[cfg.dec_seq], cs, w_fft_b, l, cfg,
                        length=cfg.dec_seq, nbatch=cfg.dec_batch, row0=cfg.p_tokens)
        if l == 0:
            x = _out_proj(attn, ssm, four, w_out_b, xp, xs, 0, mod, l, cfg)
        else:
            x = _out_proj(attn, ssm, four, w_out_b, x, x, cfg.p_tokens // cfg.tm_out, mod, l, cfg)
        ks.append(kv[:cfg.p_tokens, :kvw].reshape(cfg.batch, cfg.seq, cfg.n_kv, cfg.head_dim))
        vs.append(kv[:cfg.p_tokens, kvw:].reshape(cfg.batch, cfg.seq, cfg.n_kv, cfg.head_dim))
        fins.append(fin)

    g = final_norm_g[None, :].astype(F32)
    y_prompt = _final_norm(x, g, cfg, row0=0, nrows=cfg.p_tokens).reshape(cfg.batch, cfg.seq, d)
    y_sample = _final_norm(x, g, cfg, row0=cfg.p_tokens, nrows=cfg.s_tokens).reshape(
        cfg.dec_batch, cfg.dec_seq, d)
    fin = jnp.stack(fins, axis=0)
    fin = fin.transpose(3, 0, 1, 2, 4)
    return (y_prompt, y_sample, jnp.stack(ks, axis=1), jnp.stack(vs, axis=1),
            fin[:, :, :, 0, :P], fin[:, :, :, 0, P:], fin[:, :, :, 1, :P], fin[:, :, :, 1, P:])


def kernel(x_prompt, x_sample, cache_k, cache_v, state_fwd_re, state_fwd_im, state_bwd_re, state_bwd_im,
           c, c_ctx, norm_g, w_mod, b_mod, w_in, q_norm, k_norm, lam_re, lam_im, log_step,
           b_re, b_im, c_re, c_im, d_skip, w_glu, w_fft, w_out, final_norm_g):
    return _forward(Cfg(), x_prompt, x_sample, cache_k, cache_v, state_fwd_re, state_fwd_im,
                    state_bwd_re, state_bwd_im, c, c_ctx, norm_g, w_mod, b_mod, w_in, q_norm, k_norm,
                    lam_re, lam_im, log_step, b_re, b_im, c_re, c_im, d_skip, w_glu, w_fft, w_out,
                    final_norm_g)
```

```python
import functools
import math
from typing import NamedTuple

import jax
import jax.numpy as jnp
from jax import lax
from jax.experimental import pallas as pl
from jax.experimental.pallas import tpu as pltpu

F32 = jnp.float32
BF16 = jnp.bfloat16
NORM_EPS = 1e-6
ROPE_THETA = 10000.0
LANES = 128
SSM_CHUNK = 16
SSM_ROW_PAD = 8
VMEM_LIMIT = 56 * 1024 * 1024
LOG2E = 1.4426950408889634
HIGHEST = lax.Precision.HIGHEST


class Cfg(NamedTuple):
    d_model: int = 4096
    batch: int = 32
    seq: int = 256
    depth: int = 4
    dec_batch: int = 2
    dec_seq: int = 4096
    past_len: int = 512
    grid_w: int = 64
    head_dim: int = 128
    n_heads: int = 16
    n_kv: int = 4
    ssm_width: int = 1024
    ssm_group: int = 16
    ssm_state: int = 64
    fft_width: int = 1024
    fft_group: int = 256
    tm_in: int = 512
    tm_out: int = 512
    tn_out: int = 1024
    tn_mod: int = 512
    tq: int = 512
    rq: int = 128
    tm_fft: int = 256
    rc_fft: int = 512
    tm_post: int = 512
    n_k: int = 16

    @property
    def attn_width(self):
        return self.n_heads * self.head_dim

    @property
    def kv_width(self):
        return self.n_kv * self.head_dim

    @property
    def q_per_kv(self):
        return self.n_heads // self.n_kv

    @property
    def n_groups(self):
        return self.ssm_width // self.ssm_group

    @property
    def tn_in(self):
        return 2 * self.kv_width

    @property
    def main_width(self):
        return 2 * self.attn_width + 2 * self.ssm_width + 2 * self.fft_width

    @property
    def p_tokens(self):
        return self.batch * self.seq

    @property
    def s_tokens(self):
        return self.dec_batch * self.dec_seq

    @property
    def tokens(self):
        return self.p_tokens + self.s_tokens


def _params(*sem):
    return pltpu.CompilerParams(dimension_semantics=sem, vmem_limit_bytes=VMEM_LIMIT)


def _silu(x):
    return x / (1.0 + jnp.exp(-x))


def _row_group(i, cfg, tm):
    npb = cfg.p_tokens // tm
    per = cfg.dec_seq // tm
    return jnp.where(i < npb, 0, 1 + (i - npb) // per)


def _mod_kernel(c_ref, w_ref, b_ref, o_ref):
    s = _silu(c_ref[...]).astype(BF16)
    o_ref[0] = jnp.dot(s, w_ref[0].astype(BF16), preferred_element_type=F32) + b_ref[0]


def _modulation(cvecs, w_mod, b_mod, cfg):
    d, n = cfg.d_model, 3 * cfg.d_model
    nrow = cvecs.shape[0]
    return pl.pallas_call(
        _mod_kernel,
        grid=(cfg.depth, n // cfg.tn_mod),
        in_specs=[
            pl.BlockSpec((nrow, d), lambda l, j: (0, 0)),
            pl.BlockSpec((1, d, cfg.tn_mod), lambda l, j: (l, 0, j)),
            pl.BlockSpec((1, 1, cfg.tn_mod), lambda l, j: (l, 0, j)),
        ],
        out_specs=pl.BlockSpec((1, nrow, cfg.tn_mod), lambda l, j: (l, 0, j)),
        out_shape=jax.ShapeDtypeStruct((cfg.depth, nrow, n), F32),
        compiler_params=_params("parallel", "parallel"),
        name="modulation",
    )(cvecs, w_mod, b_mod.reshape(cfg.depth, 1, n))


def _in_proj_kernel(x_ref, mod_ref, ng_ref, w_ref, gains_ref, cos_ref, sa_ref, sb_ref,
                    kv_ref, main_ref, h_scr, *, cfg, kinds):
    j = pl.program_id(1)
    tm = x_ref.shape[0]
    hd = cfg.head_dim
    rc = min(tm, 16)

    @pl.when(j == 0)
    def _():
        shift = mod_ref[0:1, :]
        scale1 = 1.0 + mod_ref[1:2, :]
        g = ng_ref[...]

        def body(r, carry):
            rows = pl.ds(pl.multiple_of(r * rc, rc), rc)
            x = x_ref[rows, :]
            ms = jnp.mean(x * x, axis=-1, keepdims=True)
            y = x * lax.rsqrt(ms + NORM_EPS) * g
            h_scr[rows, :] = (y * scale1 + shift).astype(BF16)
            return carry

        lax.fori_loop(0, tm // rc, body, 0, unroll=min(4, tm // rc))

    tn = w_ref.shape[1]
    sub = 2 * hd

    def sub_dots():
        for s in range(tn // sub):
            yield s * sub, jnp.dot(h_scr[...], w_ref[:, s * sub:(s + 1) * sub],
                                   preferred_element_type=F32)

    def head_norm_rope(a, gain):
        ms = jnp.mean(a * a, axis=-1, keepdims=True)
        y = a * lax.rsqrt(ms + NORM_EPS) * gain
        return (y * cos_ref[...] + pltpu.roll(y, hd - hd // 4, 1) * sa_ref[...]
                + pltpu.roll(y, hd // 4, 1) * sb_ref[...])

    q_lo, q_hi = kinds["q"]

    @pl.when((j >= q_lo) & (j < q_hi))
    def _():
        qg = gains_ref[1:2, :]
        qs = hd ** -0.5 * LOG2E
        for c0, acc in sub_dots():
            for h in range(sub // hd):
                main_ref[:, c0 + h * hd:c0 + (h + 1) * hd] = (
                    head_norm_rope(acc[:, h * hd:(h + 1) * hd], qg) * qs).astype(BF16)

    @pl.when(j == kinds["kv"])
    def _():
        kg = gains_ref[0:1, :]
        for c0, acc in sub_dots():
            for h in range(sub // hd):
                a = acc[:, h * hd:(h + 1) * hd]
                is_key = c0 + h * hd < cfg.kv_width
                kv_ref[:, c0 + h * hd:c0 + (h + 1) * hd] = head_norm_rope(a, kg) if is_key else a

    silu_pred = functools.reduce(
        lambda a, b: a | b, [(j >= lo) & (j < hi) for lo, hi in kinds["silu"]])
    plain_pred = functools.reduce(
        lambda a, b: a | b, [(j >= lo) & (j < hi) for lo, hi in kinds["plain"]])

    @pl.when(silu_pred)
    def _():
        for c0, acc in sub_dots():
            main_ref[:, c0:c0 + sub] = _silu(acc).astype(BF16)

    @pl.when(plain_pred)
    def _():
        for c0, acc in sub_dots():
            main_ref[:, c0:c0 + sub] = acc.astype(BF16)


def _tile_kinds(cfg):
    tn = cfg.tn_in
    widths = [("q", cfg.attn_width), ("kv", 2 * cfg.kv_width), ("silu", cfg.attn_width),
              ("plain", cfg.ssm_width), ("silu", cfg.ssm_width), ("plain", cfg.fft_width),
              ("silu", cfg.fft_width)]
    kinds = {"q": None, "kv": None, "silu": [], "plain": []}
    j = 0
    for name, w in widths:
        assert w % tn == 0
        rng = (j, j + w // tn)
        j += w // tn
        if name == "q":
            kinds["q"] = rng
        elif name == "kv":
            kinds["kv"] = rng[0]
        else:
            kinds[name].append(rng)
    return kinds, j


def _in_proj(x, mod, norm_g, w_in_b, gains, rope_tabs, layer, cfg):
    tm, tn, d = cfg.tm_in, cfg.tn_in, cfg.d_model
    kinds, nj = _tile_kinds(cfg)
    kvj = kinds["kv"]
    npb = cfg.p_tokens // tm
    per = cfg.dec_seq // tm
    cos_t, sa_t, sb_t = rope_tabs
    tab_spec = pl.BlockSpec((tm, cfg.head_dim),
                            lambda i, j: (jnp.where(i < npb, 0, 1 + (i - npb) % per), 0))
    return pl.pallas_call(
        functools.partial(_in_proj_kernel, cfg=cfg, kinds=kinds),
        grid=(cfg.tokens // tm, nj),
        in_specs=[
            pl.BlockSpec((tm, d), lambda i, j: (i, 0)),
            pl.BlockSpec((None, None, 3, d), lambda i, j: (layer, _row_group(i, cfg, tm), 0, 0)),
            pl.BlockSpec((None, 1, d), lambda i, j: (layer, 0, 0)),
            pl.BlockSpec((None, d, tn), lambda i, j: (layer, 0, j)),
            pl.BlockSpec((None, 2, cfg.head_dim), lambda i, j: (layer, 0, 0)),
            tab_spec, tab_spec, tab_spec,
        ],
        out_specs=[
            pl.BlockSpec((tm, tn), lambda i, j: (i, 0)),
            pl.BlockSpec((tm, tn), lambda i, j: (i, jnp.where(j < kvj, j, jnp.maximum(j - 1, kvj - 1)))),
        ],
        out_shape=[
            jax.ShapeDtypeStruct((cfg.tokens, tn), F32),
            jax.ShapeDtypeStruct((cfg.tokens, cfg.main_width), BF16),
        ],
        scratch_shapes=[pltpu.VMEM((tm, d), BF16)],
        compiler_params=_params("parallel", "arbitrary"),
        name="in_proj",
    )(x, mod, norm_g, w_in_b, gains, cos_t, sa_t, sb_t)


def _softmax_attend(q, k, v1):
    hd = q.shape[1]
    s = lax.dot_general(q, k, (((1,), (1,)), ((), ())), preferred_element_type=F32)
    m = jnp.max(s, axis=-1, keepdims=True)
    p = jnp.exp2(s - m).astype(BF16)
    o = jnp.dot(p, v1, preferred_element_type=F32)
    return o[:, :hd] / o[:, hd:]


def _attn_ctx_kernel(prev_ref, q_ref, kv_ref, g_ref, o_ref, *, cfg):
    del prev_ref
    hd, kvw = cfg.head_dim, cfg.kv_width
    for kh in range(cfg.n_kv):
        k = kv_ref[:, kh * hd:(kh + 1) * hd].astype(BF16)
        v = kv_ref[:, kvw + kh * hd:kvw + (kh + 1) * hd].astype(BF16)
        v1 = jnp.concatenate([v, jnp.ones_like(v)], axis=1)
        for h in range(kh * cfg.q_per_kv, (kh + 1) * cfg.q_per_kv):
            sl = slice(h * hd, (h + 1) * hd)
            o = _softmax_attend(q_ref[:, sl], k, v1)
            o_ref[:, sl] = (o * g_ref[:, sl].astype(F32)).astype(BF16)


def _attn_context(attn_buf, main, kv, cfg):
    aw = cfg.attn_width
    return pl.pallas_call(
        functools.partial(_attn_ctx_kernel, cfg=cfg),
        grid=(cfg.batch,),
        in_specs=[
            pl.BlockSpec(memory_space=pl.ANY),
            pl.BlockSpec((cfg.seq, aw), lambda b: (b, 0)),
            pl.BlockSpec((cfg.seq, 2 * cfg.kv_width), lambda b: (b, 0)),
            pl.BlockSpec((cfg.seq, aw), lambda b: (b, 1)),
        ],
        out_specs=pl.BlockSpec((cfg.seq, aw), lambda b: (b, 0)),
        out_shape=jax.ShapeDtypeStruct((cfg.tokens, aw), BF16),
        input_output_aliases={0: 0},
        compiler_params=_params("parallel"),
        name="attn_context",
    )(attn_buf, main, kv, main)


def _attn_lat_kernel(prev_ref, q_ref, ck_ref, cv_ref, kn_ref, vn_ref, g_ref, o_ref,
                     k_scr, v_scr, *, cfg):
    del prev_ref
    hd, past = cfg.head_dim, cfg.past_len

    @pl.when(pl.program_id(2) == 0)
    def _():
        k_scr[0:past, :] = ck_ref[...].astype(BF16)
        k_scr[past:, :] = kn_ref[...].astype(BF16)
        v_scr[0:past, 0:hd] = cv_ref[...].astype(BF16)
        v_scr[past:, 0:hd] = vn_ref[...].astype(BF16)
        v_scr[:, hd:] = jnp.ones((v_scr.shape[0], hd), BF16)

    k = k_scr[...]
    v1 = v_scr[...]
    rq = min(cfg.rq, q_ref.shape[0])
    for h in range(cfg.q_per_kv):
        sl = slice(h * hd, (h + 1) * hd)
        for r in range(q_ref.shape[0] // rq):
            rows = slice(r * rq, (r + 1) * rq)
            o = _softmax_attend(q_ref[rows, sl], k, v1)
            o_ref[rows, sl] = (o * g_ref[rows, sl].astype(F32)).astype(BF16)


def _attn_latent(attn_prev, main, kv, cache_k, cache_v, layer, cfg):
    hd, nkv, tq = cfg.head_dim, cfg.n_kv, cfg.tq
    qw = cfg.q_per_kv * hd
    nqb = cfg.dec_seq // tq
    row0 = cfg.p_tokens // tq
    kvrow0 = cfg.p_tokens // cfg.dec_seq
    nkeys = cfg.past_len + cfg.dec_seq

    def q_map(b, h, qi):
        return (row0 + b * nqb + qi, h)

    def g_map(b, h, qi):
        return (row0 + b * nqb + qi, nkv + h)

    cache_spec = pl.BlockSpec((None, None, cfg.past_len, hd), lambda b, h, qi: (b, layer, 0, h))
    return pl.pallas_call(
        functools.partial(_attn_lat_kernel, cfg=cfg),
        grid=(cfg.dec_batch, nkv, nqb),
        in_specs=[
            pl.BlockSpec(memory_space=pl.ANY),
            pl.BlockSpec((tq, qw), q_map),
            cache_spec, cache_spec,
            pl.BlockSpec((cfg.dec_seq, hd), lambda b, h, qi: (kvrow0 + b, h)),
            pl.BlockSpec((cfg.dec_seq, hd), lambda b, h, qi: (kvrow0 + b, nkv + h)),
            pl.BlockSpec((tq, qw), g_map),
        ],
        out_specs=pl.BlockSpec((tq, qw), q_map),
        out_shape=jax.ShapeDtypeStruct((cfg.tokens, cfg.attn_width), BF16),
        scratch_shapes=[pltpu.VMEM((nkeys, hd), BF16), pltpu.VMEM((nkeys, 2 * hd), BF16)],
        input_output_aliases={0: 0},
        compiler_params=_params("parallel", "parallel", "arbitrary"),
        name="attn_latent",
    )(attn_prev, main, cache_k, cache_v, kv, kv, main)


def _ssm_kernel(u_ref, perm_ref, catf_ref, catb_ref, bbt_ref, pa_ref, w2_ref, av_ref,
                pw_ref, dsk_ref, h0_ref, y_ref, fin_ref,
                uf_scr, xcat_scr, u8_scr, m_scr, w1s_scr, s_scr, hin_scr, f_scr, g_scr, *, cfg):
    T, C = SSM_CHUNK, cfg.ssm_group
    tc = T * C
    st = 2 * cfg.ssm_state
    n_k = cfg.n_k
    chain = T * n_k
    nc = cfg.p_tokens // chain
    n_sc = cfg.dec_seq // chain
    gt = LANES // C
    jl_n = LANES // C
    jh_n = T // jl_n
    is_latent = pl.program_id(1) == 1
    y8_scr = xcat_scr
    pitch = chain + SSM_ROW_PAD

    def stage_in(c, carry):
        src = pl.ds(pl.multiple_of(c * chain, chain), chain)
        uf_scr[pl.ds(pl.multiple_of(c * pitch, 8), chain), :] = u_ref[src, :].astype(F32)
        return carry

    lax.fori_loop(0, nc, stage_in, 0)

    def gather(k, carry):
        for j in range(T):
            piece = uf_scr[pl.ds(k * T + j, nc, stride=pitch), :].astype(BF16)
            xcat_scr[j // jl_n, pl.ds(pl.multiple_of(k * nc, nc), nc),
                     (j % jl_n) * LANES:(j % jl_n + 1) * LANES] = piece
        return carry

    lax.fori_loop(0, n_k, gather, 0)
    for jh in range(jh_n):
        t = jnp.dot(xcat_scr[jh], perm_ref[...], preferred_element_type=F32)
        for r in range(gt):
            u8_scr[r, :, jh * LANES:(jh + 1) * LANES] = t[:, r * LANES:(r + 1) * LANES].astype(BF16)

    def cmul(h, hs, c1, c2):
        return h * c1 + hs * c2, hs * c1 - h * c2

    lane = lax.broadcasted_iota(jnp.int32, (C, tc), 1)

    for r in range(gt):
        ktf = jnp.dot(bbt_ref[r, 0], catf_ref[r], preferred_element_type=F32, precision=HIGHEST)
        ktb = jnp.dot(bbt_ref[r, 1], catb_ref[r], preferred_element_type=F32, precision=HIGHEST)
        for j in range(T):
            rf = pltpu.roll(ktf, C * j, 1) if j else ktf
            rf = jnp.where(lane >= C * j, rf, 0.0)
            sh = (T - 1 - j) * C
            rb = pltpu.roll(ktb, tc - sh, 1) if sh else ktb
            rb = jnp.where(lane < tc - sh, rb, 0.0)
            m_scr[j * C:(j + 1) * C, :] = (rf + rb).astype(BF16)
        for d in range(2):
            bb = bbt_ref[r, d]
            bbs = pltpu.roll(bb, st // 2, 1)
            for j in range(T):
                w, ws = cmul(bb, bbs, pa_ref[r, d, j, 0:1, :], pa_ref[r, d, j, 1:2, :])
                w1s_scr[j * C:(j + 1) * C, 2 * d * st:(2 * d + 1) * st] = w.astype(BF16)
                w1s_scr[j * C:(j + 1) * C, (2 * d + 1) * st:(2 * d + 2) * st] = ws.astype(BF16)

        u = u8_scr[r]
        y_intra = jnp.dot(u, m_scr[...], preferred_element_type=F32)
        s_scr[...] = jnp.dot(u, w1s_scr[...], preferred_element_type=F32)
        av = av_ref[r]
        a1 = (av[0:1], av[2:3])
        a2 = (av[1:2], av[3:4])
        b1 = (av[4:5], av[6:7])
        b2 = (av[5:6], av[7:8])

        def level1(k, carry):
            hf, hfs, hb, hbs = carry
            rf_ = pl.ds(pl.multiple_of(k * nc, nc), nc)
            rb_ = pl.ds(pl.multiple_of((n_k - 1 - k) * nc, nc), nc)
            hin_scr[rf_, 0:st] = hf
            hin_scr[rf_, st:2 * st] = hfs
            hin_scr[rb_, 2 * st:3 * st] = hb
            hin_scr[rb_, 3 * st:4 * st] = hbs
            nf, nfs = cmul(hf, hfs, a1[0], a2[0])
            nbk, nbs = cmul(hb, hbs, a1[1], a2[1])
            return (nf + s_scr[rf_, 0:st], nfs + s_scr[rf_, st:2 * st],
                    nbk + s_scr[rb_, 2 * st:3 * st], nbs + s_scr[rb_, 3 * st:4 * st])

        z = jnp.zeros((nc, st), F32)
        ff, ffs, fb, fbs = lax.fori_loop(0, n_k, level1, (z, z, z, z))

        @pl.when(jnp.logical_not(is_latent))
        def _():
            fin_ref[r, 0] = ff
            fin_ref[r, 1] = fb

        @pl.when(is_latent)
        def _():
            for idx, val in enumerate((ff, ffs, fb, fbs)):
                f_scr[idx] = val
            h0 = h0_ref[r]
            gf, gfs, gb, gbs = h0[0], h0[1], h0[2], h0[3]
            for sc in range(n_sc):
                sel = pl.ds(sc, cfg.dec_batch, stride=n_sc)
                g_scr[0, sel, :] = gf
                g_scr[1, sel, :] = gfs
                nf, nfs = cmul(gf, gfs, b1[0], b2[0])
                gf = nf + f_scr[0, sel, :]
                gfs = nfs + f_scr[1, sel, :]
            for sc in range(n_sc - 1, -1, -1):
                sel = pl.ds(sc, cfg.dec_batch, stride=n_sc)
                g_scr[2, sel, :] = gb
                g_scr[3, sel, :] = gbs
                nbk, nbs = cmul(gb, gbs, b1[1], b2[1])
                gb = nbk + f_scr[2, sel, :]
                gbs = nbs + f_scr[3, sel, :]
            gfa, gfsa, gba, gbsa = g_scr[0], g_scr[1], g_scr[2], g_scr[3]
            for k in range(n_k):
                rows_k = slice(k * nc, (k + 1) * nc)
                pf = pw_ref[r, k]
                cf, cfs = cmul(gfa, gfsa, pf[0:1], pf[1:2])
                cb, cbs = cmul(gba, gbsa, pf[2:3], pf[3:4])
                hin_scr[rows_k, 0:st] += cf
                hin_scr[rows_k, st:2 * st] += cfs
                hin_scr[rows_k, 2 * st:3 * st] += cb
                hin_scr[rows_k, 3 * st:4 * st] += cbs

        hsel = jnp.concatenate([hin_scr[:, 0:st], hin_scr[:, 2 * st:3 * st]], axis=1).astype(BF16)
        y = y_intra + jnp.dot(hsel, w2_ref[r], preferred_element_type=F32)
        y = (y + dsk_ref[r] * u.astype(F32)).astype(BF16)
        for jh in range(jh_n):
            y8_scr[jh, :, r * LANES:(r + 1) * LANES] = y[:, jh * LANES:(jh + 1) * LANES]

    for jh in range(jh_n):
        t = lax.dot_general(y8_scr[jh], perm_ref[...], (((1,), (1,)), ((), ())),
                            preferred_element_type=F32)
        for k in range(n_k):
            for jl in range(jl_n):
                uf_scr[pl.ds(k * T + jh * jl_n + jl, nc, stride=pitch), :] = (
                    t[k * nc:(k + 1) * nc, jl * LANES:(jl + 1) * LANES])

    def stage_out(c, carry):
        dst = pl.ds(pl.multiple_of(c * chain, chain), chain)
        y_ref[dst, :] = uf_scr[pl.ds(pl.multiple_of(c * pitch, 8), chain), :].astype(BF16)
        return carry

    lax.fori_loop(0, nc, stage_out, 0)


def _ssm_scan(main, ssm_w, h0, layer, cfg):
    perm, catf, catb, bbt, pa, w2, av, pw, dsk = ssm_w
    T, C = SSM_CHUNK, cfg.ssm_group
    tc = T * C
    st = 2 * cfg.ssm_state
    gt = LANES // C
    nq = cfg.ssm_width // LANES
    chain = T * cfg.n_k
    half = cfg.p_tokens
    assert cfg.seq == chain and cfg.dec_seq % chain == 0 and T % gt == 0 and cfg.s_tokens == half
    nc = half // chain
    rows = nc * cfg.n_k
    pw_n = gt * LANES
    ucol = 2 * cfg.attn_width // LANES

    def wspec(*shape):
        nd = len(shape)
        return pl.BlockSpec((None, gt) + shape, lambda q, s: (layer, q) + (0,) * nd)

    return pl.pallas_call(
        functools.partial(_ssm_kernel, cfg=cfg),
        grid=(nq, 2),
        in_specs=[
            pl.BlockSpec((half, LANES), lambda q, s: (s, ucol + q)),
            pl.BlockSpec((pw_n, pw_n), lambda q, s: (0, 0)),
            wspec(st, tc), wspec(st, tc), wspec(2, C, st), wspec(2, T, 2, st), wspec(2 * st, tc),
            wspec(8, st), wspec(cfg.n_k, 4, st), wspec(1, tc), wspec(4, cfg.dec_batch, st),
        ],
        out_specs=[
            pl.BlockSpec((half, LANES), lambda q, s: (s, q)),
            pl.BlockSpec((gt, 2, cfg.batch, st), lambda q, s: (q, 0, 0, 0)),
        ],
        out_shape=[
            jax.ShapeDtypeStruct((cfg.tokens, cfg.ssm_width), BF16),
            jax.ShapeDtypeStruct((cfg.n_groups, 2, cfg.batch, st), F32),
        ],
        scratch_shapes=[
            pltpu.VMEM((nc * (chain + SSM_ROW_PAD), LANES), F32),
            pltpu.VMEM((T // gt, rows, pw_n), BF16),
            pltpu.VMEM((gt, rows, tc), BF16),
            pltpu.VMEM((tc, tc), BF16),
            pltpu.VMEM((tc, 4 * st), BF16),
            pltpu.VMEM((rows, 4 * st), F32),
            pltpu.VMEM((rows, 4 * st), F32),
            pltpu.VMEM((4, nc, st), F32),
            pltpu.VMEM((4, nc, st), F32),
        ],
        compiler_params=_params("parallel", "arbitrary"),
        name="ssm_scan",
    )(main, perm, catf, catb, bbt, pa, w2, av, pw, dsk, h0)


def _ssm_weights(lam_re, lam_im, log_step, b_re, b_im, c_re, c_im, d_skip, cfg):
    T, C, P, G, nk = SSM_CHUNK, cfg.ssm_group, cfg.ssm_state, cfg.n_groups, cfg.n_k
    lr, li = lam_re.astype(F32), lam_im.astype(F32)
    dt = jnp.exp(log_step.astype(F32))[..., None]

    def powers(js):
        jj = jnp.asarray(js, F32)[:, None]
        mag = jnp.exp(lr[..., None, :] * dt[..., None, :] * jj)
        ang = li[..., None, :] * dt[..., None, :] * jj
        return mag * jnp.cos(ang), mag * jnp.sin(ang)

    pr, pi = powers(range(T + 1))
    ab_re, ab_im = pr[..., 1, :], pi[..., 1, :]
    nr, ni = ab_re - 1.0, ab_im
    den = lr * lr + li * li
    f_re = (nr * lr + ni * li) / den
    f_im = (ni * lr - nr * li) / den
    br, bi = b_re.astype(F32), b_im.astype(F32)
    bb_re = f_re[..., None] * br - f_im[..., None] * bi
    bb_im = f_re[..., None] * bi + f_im[..., None] * br
    cr, ci = c_re.astype(F32), c_im.astype(F32)
    ca_re = cr[..., None, :, :] * pr[..., :, None, :] - ci[..., None, :, :] * pi[..., :, None, :]
    ca_im = cr[..., None, :, :] * pi[..., :, None, :] + ci[..., None, :, :] * pr[..., :, None, :]

    def state_out(d, lo, rev):
        def pick(x):
            x = x[:, d, :, lo:lo + T]
            x = jnp.flip(x, axis=2) if rev else x
            return x.transpose(0, 1, 4, 2, 3).reshape(-1, G, P, T * C)
        return jnp.concatenate([pick(ca_re), -pick(ca_im)], axis=2)

    catf = state_out(0, 0, False)
    catb = state_out(1, 0, True)
    w2 = jnp.concatenate([state_out(0, 1, False), state_out(1, 1, True)], axis=2).astype(BF16)
    bbt = jnp.concatenate([bb_re, bb_im], axis=3).transpose(0, 2, 1, 4, 3)

    def lanes(re, im):
        return jnp.concatenate([re, re], axis=-1), jnp.concatenate([-im, im], axis=-1)

    paf = lanes(jnp.flip(pr[:, 0, :, 0:T], axis=2), jnp.flip(pi[:, 0, :, 0:T], axis=2))
    pab = lanes(pr[:, 1, :, 0:T], pi[:, 1, :, 0:T])
    pa = jnp.stack([jnp.stack(paf, axis=3), jnp.stack(pab, axis=3)], axis=2)

    qr, qi = powers([T * k for k in range(nk + 1)])
    rows = []
    for src_r, src_i, idx in ((pr, pi, T), (qr, qi, nk)):
        for d in range(2):
            rows.extend(lanes(src_r[:, d, :, idx], src_i[:, d, :, idx]))
    av = jnp.stack(rows, axis=2)
    p1f, p2f = lanes(qr[:, 0, :, 0:nk], qi[:, 0, :, 0:nk])
    p1b, p2b = lanes(jnp.flip(qr[:, 1, :, 0:nk], axis=2), jnp.flip(qi[:, 1, :, 0:nk], axis=2))
    pw = jnp.stack([p1f, p2f, p1b, p2b], axis=3)
    dsk = jnp.tile(d_skip.astype(F32).reshape(-1, G, 1, C), (1, 1, 1, T))

    gt = LANES // C
    n = gt * LANES
    src = jnp.arange(n)
    jl, r, c = src // LANES, (src % LANES) // C, src % C
    dst = r * LANES + jl * C + c
    perm = (dst[:, None] == jnp.arange(n)[None, :]).astype(BF16)
    return perm, catf, catb, bbt, pa, w2, av, pw, dsk


def _glu_kernel(y_ref, w_ref, g_ref, o_ref, *, cfg):
    z = jnp.dot(y_ref[...], w_ref[...], preferred_element_type=F32)
    sw = cfg.ssm_width
    a, g = z[:, :sw], z[:, sw:]
    o_ref[...] = (a / (1.0 + jnp.exp(-g)) * g_ref[...].astype(F32)).astype(BF16)


def _ssm_glu(y, w_glu_b, main, layer, cfg):
    tm, sw = cfg.tm_post, cfg.ssm_width
    gcol = (2 * cfg.attn_width + sw) // sw
    assert (2 * cfg.attn_width) % sw == 0
    return pl.pallas_call(
        functools.partial(_glu_kernel, cfg=cfg),
        grid=(cfg.tokens // tm,),
        in_specs=[
            pl.BlockSpec((tm, sw), lambda i: (i, 0)),
            pl.BlockSpec((None, sw, 2 * sw), lambda i: (layer, 0, 0)),
            pl.BlockSpec((tm, sw), lambda i: (i, gcol)),
        ],
        out_specs=pl.BlockSpec((tm, sw), lambda i: (i, 0)),
        out_shape=jax.ShapeDtypeStruct((cfg.tokens, sw), BF16),
        compiler_params=_params("parallel"),
        name="ssm_glu",
    )(y, w_glu_b, main)


def _fourier_kernel(prev_ref, x_ref, dl_ref, cs_ref, w_ref, g_ref, o_ref, z_scr, *, cfg, length):
    del prev_ref
    fg = cfg.fft_group
    rc = min(cfg.rc_fft, length)

    @pl.when(pl.program_id(1) == 0)
    def _():
        def body(ci, carry):
            rows = pl.ds(pl.multiple_of(ci * rc, rc), rc)
            rows2 = pl.ds(pl.multiple_of(length + ci * rc, rc), rc)
            for g in range(cfg.fft_width // fg):
                cols = slice(g * fg, (g + 1) * fg)
                t = jnp.dot(x_ref[rows, cols], cs_ref[...], preferred_element_type=F32)
                z_scr[rows, cols] = t[:, :fg].astype(BF16)
                z_scr[rows2, cols] = t[:, fg:].astype(BF16)
            return carry

        lax.fori_loop(0, length // rc, body, 0)

    mixed = jnp.dot(dl_ref[...], z_scr[...], preferred_element_type=F32).astype(BF16)
    four = jnp.dot(mixed, w_ref[...], preferred_element_type=F32)
    o_ref[...] = (four * g_ref[...].astype(F32)).astype(BF16)


def _fourier(prev, main, dl, cs, w_fft_b, layer, cfg, *, length, nbatch, row0):
    fw = cfg.fft_width
    tm = min(cfg.tm_fft, length)
    nr = length // tm
    xcol = (2 * cfg.attn_width + 2 * cfg.ssm_width) // fw
    assert (2 * cfg.attn_width + 2 * cfg.ssm_width) % fw == 0 and row0 % length == 0
    b0 = row0 // length
    t0 = row0 // tm

    def tile_map(b, r):
        return (t0 + b * nr + r, 0)

    return pl.pallas_call(
        functools.partial(_fourier_kernel, cfg=cfg, length=length),
        grid=(nbatch, nr),
        in_specs=[
            pl.BlockSpec(memory_space=pl.ANY),
            pl.BlockSpec((length, fw), lambda b, r: (b0 + b, xcol)),
            pl.BlockSpec((tm, 2 * length), lambda b, r: (r, 0)),
            pl.BlockSpec((cfg.fft_group, 2 * cfg.fft_group), lambda b, r: (0, 0)),
            pl.BlockSpec((None, fw, fw), lambda b, r: (layer, 0, 0)),
            pl.BlockSpec((tm, fw), lambda b, r: (t0 + b * nr + r, xcol + 1)),
        ],
        out_specs=pl.BlockSpec((tm, fw), tile_map),
        out_shape=jax.ShapeDtypeStruct((cfg.tokens, fw), BF16),
        scratch_shapes=[pltpu.VMEM((2 * length, fw), BF16)],
        input_output_aliases={0: 0},
        compiler_params=_params("parallel", "arbitrary"),
        name=f"fourier_{length}",
    )(prev, main, dl, cs, w_fft_b, main)


def _dft_cos_sin(n):
    k = jnp.arange(n, dtype=jnp.int32)
    w = 2.0 * math.pi / n

    def cs(j):
        ang = ((j[:, None] * k[None, :]) % n).astype(F32) * w
        return jnp.cos(ang), jnp.sin(ang)

    s = 1
    while s * s < n:
        s *= 2
    if n <= 512 or n % s:
        c, sn = cs(k)
    else:
        c1, s1 = cs(jnp.arange(n // s, dtype=jnp.int32) * s)
        c0, s0 = cs(jnp.arange(s, dtype=jnp.int32))
        c = (c1[:, None] * c0[None] - s1[:, None] * s0[None]).reshape(n, n)
        sn = (s1[:, None] * c0[None] + c1[:, None] * s0[None]).reshape(n, n)
    scale = n ** -0.5
    return c * scale, sn * scale


def _out_proj_kernel(a_ref, s_ref, f_ref, w_ref, x_ref, mod_ref, o_ref, *, cfg):
    aw, sw = cfg.attn_width, cfg.ssm_width
    acc = jnp.dot(a_ref[...], w_ref[0:aw, :], preferred_element_type=F32)
    acc += jnp.dot(s_ref[...], w_ref[aw:aw + sw, :], preferred_element_type=F32)
    acc += jnp.dot(f_ref[...], w_ref[aw + sw:, :], preferred_element_type=F32)
    o_ref[...] = x_ref[...] + mod_ref[2:3, :] * acc


def _out_proj(attn, ssm, four, w_out_b, x, mod, layer, cfg):
    tm, tn, d = cfg.tm_out, cfg.tn_out, cfg.d_model
    return pl.pallas_call(
        functools.partial(_out_proj_kernel, cfg=cfg),
        grid=(d // tn, cfg.tokens // tm),
        in_specs=[
            pl.BlockSpec((tm, cfg.attn_width), lambda j, i: (i, 0)),
            pl.BlockSpec((tm, cfg.ssm_width), lambda j, i: (i, 0)),
            pl.BlockSpec((tm, cfg.fft_width), lambda j, i: (i, 0)),
            pl.BlockSpec((None, d, tn), lambda j, i: (layer, 0, j)),
            pl.BlockSpec((tm, tn), lambda j, i: (i, j)),
            pl.BlockSpec((None, None, 3, tn), lambda j, i: (layer, _row_group(i, cfg, tm), 0, j)),
        ],
        out_specs=pl.BlockSpec((tm, tn), lambda j, i: (i, j)),
        out_shape=jax.ShapeDtypeStruct((cfg.tokens, d), F32),
        compiler_params=_params("parallel", "parallel"),
        name="out_proj",
    )(attn, ssm, four, w_out_b, x, mod)


def _final_norm_kernel(x_ref, g_ref, o_ref):
    x = x_ref[...]
    ms = jnp.mean(x * x, axis=-1, keepdims=True)
    o_ref[...] = x * lax.rsqrt(ms + NORM_EPS) * g_ref[...]


def _final_norm(x, g, cfg, *, row0, nrows):
    tm, d = 256, cfg.d_model
    tm = min(tm, nrows)
    assert row0 % tm == 0
    return pl.pallas_call(
        _final_norm_kernel,
        grid=(nrows // tm,),
        in_specs=[pl.BlockSpec((tm, d), lambda i: (row0 // tm + i, 0)),
                  pl.BlockSpec((1, d), lambda i: (0, 0))],
        out_specs=pl.BlockSpec((tm, d), lambda i: (i, 0)),
        out_shape=jax.ShapeDtypeStruct((nrows, d), F32),
        compiler_params=_params("parallel"),
        name="final_norm",
    )(x, g)


def _rope_tables(cfg):
    hd = cfg.head_dim
    pairs = hd // 4
    t = jnp.arange(cfg.dec_seq)
    inv = ROPE_THETA ** (-jnp.arange(pairs, dtype=F32) / pairs)
    row_ang = (t // cfg.grid_w).astype(F32)[:, None] * inv[None, :]
    col_ang = (t % cfg.grid_w).astype(F32)[:, None] * inv[None, :]
    zeros = jnp.zeros_like(row_ang)
    cos = jnp.concatenate([jnp.cos(row_ang)] * 2 + [jnp.cos(col_ang)] * 2, axis=1)
    sa = jnp.concatenate([-jnp.sin(row_ang), zeros, -jnp.sin(col_ang), zeros], axis=1)
    sb = jnp.concatenate([zeros, jnp.sin(row_ang), zeros, jnp.sin(col_ang)], axis=1)
    ident = jnp.zeros((cfg.tm_in, hd), F32)
    return (jnp.concatenate([ident + 1.0, cos], axis=0),
            jnp.concatenate([ident, sa], axis=0),
            jnp.concatenate([ident, sb], axis=0))


def _forward(cfg, x_prompt, x_sample, cache_k, cache_v, state_fwd_re, state_fwd_im, state_bwd_re,
             state_bwd_im, c, c_ctx, norm_g, w_mod, b_mod, w_in, q_norm, k_norm, lam_re, lam_im,
             log_step, b_re, b_im, c_re, c_im, d_skip, w_glu, w_fft, w_out, final_norm_g):
    d, kvw = cfg.d_model, cfg.kv_width
    P = cfg.ssm_state

    w_in_b = w_in.astype(BF16)
    w_out_b = w_out.astype(BF16)
    w_glu_b = w_glu.astype(BF16)
    w_fft_b = w_fft.astype(BF16)
    gains = jnp.stack([k_norm, q_norm], axis=1).astype(F32)
    norm_g3 = norm_g.astype(F32)[:, None, :]
    ssm_w = _ssm_weights(lam_re, lam_im, log_step, b_re, b_im, c_re, c_im, d_skip, cfg)
    rope_tabs = _rope_tables(cfg)

    cc, sc = _dft_cos_sin(cfg.fft_group)
    cs = jnp.concatenate([cc, sc], axis=1).astype(BF16)
    dls = {}
    for length in {cfg.seq, cfg.dec_seq}:
        cl, sl = _dft_cos_sin(length)
        dls[length] = jnp.concatenate([cl, -sl], axis=1).astype(BF16)

    def st(re, im):
        re = re.astype(F32).transpose(1, 2, 0, 3)
        im = im.astype(F32).transpose(1, 2, 0, 3)
        return jnp.concatenate([re, im], axis=-1), jnp.concatenate([im, re], axis=-1)

    h0 = jnp.stack(st(state_fwd_re, state_fwd_im) + st(state_bwd_re, state_bwd_im), axis=2)

    nrow = 1 + cfg.dec_batch
    cvecs = jnp.concatenate([c_ctx[None, :], c], axis=0).astype(F32)
    cvecs = jnp.pad(cvecs, ((0, -nrow % 8), (0, 0)))
    mod = _modulation(cvecs, w_mod, b_mod, cfg).reshape(cfg.depth, cvecs.shape[0], 3, d)

    cache_k4 = cache_k.reshape(cfg.dec_batch, cfg.depth, cfg.past_len, kvw)
    cache_v4 = cache_v.reshape(cfg.dec_batch, cfg.depth, cfg.past_len, kvw)

    x = jnp.concatenate([x_prompt.reshape(cfg.p_tokens, d), x_sample.reshape(cfg.s_tokens, d)], axis=0)
    attn = jnp.zeros((cfg.tokens, cfg.attn_width), BF16)
    four = jnp.zeros((cfg.tokens, cfg.fft_width), BF16)
    ks, vs, fins = [], [], []
    for l in range(cfg.depth):
        kv, main = _in_proj(x, mod, norm_g3, w_in_b, gains, rope_tabs, l, cfg)
        attn = _attn_context(attn, main, kv, cfg)
        attn = _attn_latent(attn, main, kv, cache_k4, cache_v4, l, cfg)
        y, fin = _ssm_scan(main, ssm_w, h0, l, cfg)
        ssm = _ssm_glu(y, w_glu_b, main, l, cfg)
        four = _fourier(four, main, dls[cfg.seq], cs, w_fft_b, l, cfg,
                        length=cfg.seq, nbatch=cfg.batch, row0=0)
        four = _fourier(four, main, dls[cfg.dec_seq], cs, w_fft_b, l, cfg,
                        length=cfg.dec_seq, nbatch=cfg.dec_batch, row0=cfg.p_tokens)
        x = _out_proj(attn, ssm, four, w_out_b, x, mod, l, cfg)
        ks.append(kv[:cfg.p_tokens, :kvw].reshape(cfg.batch, cfg.seq, cfg.n_kv, cfg.head_dim))
        vs.append(kv[:cfg.p_tokens, kvw:].reshape(cfg.batch, cfg.seq, cfg.n_kv, cfg.head_dim))
        fins.append(fin)

    g = final_norm_g[None, :].astype(F32)
    y_prompt = _final_norm(x, g, cfg, row0=0, nrows=cfg.p_tokens).reshape(cfg.batch, cfg.seq, d)
    y_sample = _final_norm(x, g, cfg, row0=cfg.p_tokens, nrows=cfg.s_tokens).reshape(
        cfg.dec_batch, cfg.dec_seq, d)
    fin = jnp.stack(fins, axis=0)
    fin = fin.transpose(3, 0, 1, 2, 4)
    return (y_prompt, y_sample, jnp.stack(ks, axis=1), jnp.stack(vs, axis=1),
            fin[:, :, :, 0, :P], fin[:, :, :, 0, P:], fin[:, :, :, 1, :P], fin[:, :, :, 1, P:])


def kernel(x_prompt, x_sample, cache_k, cache_v, state_fwd_re, state_fwd_im, state_bwd_re, state_bwd_im,
           c, c_ctx, norm_g, w_mod, b_mod, w_in, q_norm, k_norm, lam_re, lam_im, log_step,
           b_re, b_im, c_re, c_im, d_skip, w_glu, w_fft, w_out, final_norm_g):
    return _forward(Cfg(), x_prompt, x_sample, cache_k, cache_v, state_fwd_re, state_fwd_im,
                    state_bwd_re, state_bwd_im, c, c_ctx, norm_g, w_mod, b_mod, w_in, q_norm, k_norm,
                    lam_re, lam_im, log_step, b_re, b_im, c_re, c_im, d_skip, w_glu, w_fft, w_out,
                    final_norm_g)
```

```python
import functools
import math
from typing import NamedTuple

import jax
import jax.numpy as jnp
from jax import lax
from jax.experimental import pallas as pl
from jax.experimental.pallas import tpu as pltpu

F32 = jnp.float32
BF16 = jnp.bfloat16
NORM_EPS = 1e-6
ROPE_THETA = 10000.0
LANES = 128
SSM_CHUNK = 16
SSM_ROW_PAD = 8
VMEM_LIMIT = 56 * 1024 * 1024
LOG2E = 1.4426950408889634
HIGHEST = lax.Precision.HIGHEST


class Cfg(NamedTuple):
    d_model: int = 4096
    batch: int = 32
    seq: int = 256
    depth: int = 4
    dec_batch: int = 2
    dec_seq: int = 4096
    past_len: int = 512
    grid_w: int = 64
    head_dim: int = 128
    n_heads: int = 16
    n_kv: int = 4
    ssm_width: int = 1024
    ssm_group: int = 16
    ssm_state: int = 64
    fft_width: int = 1024
    fft_group: int = 256
    tm_in: int = 512
    tm_out: int = 512
    tn_out: int = 1024
    tn_mod: int = 512
    tq: int = 512
    rq: int = 128
    tm_fft: int = 256
    rc_fft: int = 512
    tm_post: int = 512
    n_k: int = 16

    @property
    def attn_width(self):
        return self.n_heads * self.head_dim

    @property
    def kv_width(self):
        return self.n_kv * self.head_dim

    @property
    def q_per_kv(self):
        return self.n_heads // self.n_kv

    @property
    def n_groups(self):
        return self.ssm_width // self.ssm_group

    @property
    def tn_in(self):
        return 2 * self.kv_width

    @property
    def main_width(self):
        return 2 * self.attn_width + 2 * self.ssm_width + 2 * self.fft_width

    @property
    def p_tokens(self):
        return self.batch * self.seq

    @property
    def s_tokens(self):
        return self.dec_batch * self.dec_seq

    @property
    def tokens(self):
        return self.p_tokens + self.s_tokens


def _params(*sem):
    return pltpu.CompilerParams(dimension_semantics=sem, vmem_limit_bytes=VMEM_LIMIT)


def _silu(x):
    return x / (1.0 + jnp.exp(-x))


def _row_group(i, cfg, tm):
    npb = cfg.p_tokens // tm
    per = cfg.dec_seq // tm
    return jnp.where(i < npb, 0, 1 + (i - npb) // per)


def _mod_kernel(c_ref, w_ref, b_ref, o_ref):
    s = _silu(c_ref[...]).astype(BF16)
    o_ref[0] = jnp.dot(s, w_ref[0].astype(BF16), preferred_element_type=F32) + b_ref[0]


def _modulation(cvecs, w_mod, b_mod, cfg):
    d, n = cfg.d_model, 3 * cfg.d_model
    nrow = cvecs.shape[0]
    return pl.pallas_call(
        _mod_kernel,
        grid=(cfg.depth, n // cfg.tn_mod),
        in_specs=[
            pl.BlockSpec((nrow, d), lambda l, j: (0, 0)),
            pl.BlockSpec((1, d, cfg.tn_mod), lambda l, j: (l, 0, j)),
            pl.BlockSpec((1, 1, cfg.tn_mod), lambda l, j: (l, 0, j)),
        ],
        out_specs=pl.BlockSpec((1, nrow, cfg.tn_mod), lambda l, j: (l, 0, j)),
        out_shape=jax.ShapeDtypeStruct((cfg.depth, nrow, n), F32),
        compiler_params=_params("parallel", "parallel"),
        name="modulation",
    )(cvecs, w_mod, b_mod.reshape(cfg.depth, 1, n))


def _in_proj_kernel(x_ref, mod_ref, ng_ref, w_ref, gains_ref, cos_ref, sa_ref, sb_ref,
                    kv_ref, main_ref, h_scr, hn_scr, raw0_scr, raw1_scr, *, cfg, kinds):
    i = pl.program_id(0)
    j = pl.program_id(1)
    tm = x_ref.shape[0]
    hd = cfg.head_dim
    tn = w_ref.shape[1]
    nj = len(kinds)
    raw = (raw0_scr, raw1_scr)

    def norm_chunk(rows, dst_ref):
        x = x_ref[rows, :]
        ms = jnp.mean(x * x, axis=-1, keepdims=True)
        y = x * lax.rsqrt(ms + NORM_EPS) * ng_ref[...]
        dst_ref[rows, :] = (y * (1.0 + mod_ref[1:2, :]) + mod_ref[0:1, :]).astype(BF16)

    def head_norm_rope(a, gain):
        ms = jnp.mean(a * a, axis=-1, keepdims=True)
        y = a * lax.rsqrt(ms + NORM_EPS) * gain
        return (y * cos_ref[...] + pltpu.roll(y, hd - hd // 4, 1) * sa_ref[...]
                + pltpu.roll(y, hd // 4, 1) * sb_ref[...])

    def epilogue(kind, acc_ref):
        if kind == "q":
            qs = hd ** -0.5 * LOG2E
            for h in range(tn // hd):
                sl = slice(h * hd, (h + 1) * hd)
                main_ref[:, sl] = (head_norm_rope(acc_ref[:, sl], gains_ref[1:2, :]) * qs).astype(BF16)
        elif kind == "kv":
            for h in range(tn // hd):
                sl = slice(h * hd, (h + 1) * hd)
                a = acc_ref[:, sl]
                kv_ref[:, sl] = head_norm_rope(a, gains_ref[0:1, :]) if h * hd < cfg.kv_width else a
        elif kind == "silu":
            main_ref[...] = _silu(acc_ref[...]).astype(BF16)
        else:
            main_ref[...] = acc_ref[...].astype(BF16)

    pr = tm // (nj - 1)
    rc = min(pr, 16)

    def step(jj):
        prev_kind = kinds[jj - 1] if jj else None
        if prev_kind:
            epilogue(prev_kind, raw[(jj + 1) % 2])
        if jj == nj:
            h_scr[...] = hn_scr[...]
            return
        if jj == 0:
            @pl.when(i == 0)
            def _():
                def body(r, carry):
                    norm_chunk(pl.ds(pl.multiple_of(r * rc, rc), rc), h_scr)
                    return carry

                lax.fori_loop(0, tm // rc, body, 0, unroll=min(4, tm // rc))

        raw[jj % 2][...] = jnp.dot(h_scr[...], w_ref[...], preferred_element_type=F32)
        if jj:
            for r in range(pr // rc):
                norm_chunk(pl.ds(pl.multiple_of((j - 1) * pr + r * rc, rc), rc), hn_scr)

    groups = {}
    for jj in range(nj + 1):
        key = (kinds[jj - 1] if jj else None, jj % 2, jj == nj)
        groups.setdefault(key, []).append(jj)
    for js in groups.values():
        pred = functools.reduce(lambda a, b: a | b, [j == v for v in js])
        pl.when(pred)(functools.partial(step, js[0]))


def _tile_kinds(cfg):
    tn = cfg.tn_in
    widths = [("q", cfg.attn_width), ("kv", 2 * cfg.kv_width), ("silu", cfg.attn_width),
              ("plain", cfg.ssm_width), ("silu", cfg.ssm_width), ("plain", cfg.fft_width),
              ("silu", cfg.fft_width)]
    kinds = []
    for name, w in widths:
        assert w % tn == 0
        kinds += [name] * (w // tn)
    return tuple(kinds)


def _in_proj(x, mod, norm_g, w_in_b, gains, rope_tabs, layer, cfg):
    tm, tn, d = cfg.tm_in, cfg.tn_in, cfg.d_model
    kinds = _tile_kinds(cfg)
    nj = len(kinds)
    kvj = kinds.index("kv")
    npb = cfg.p_tokens // tm
    per = cfg.dec_seq // tm
    cos_t, sa_t, sb_t = rope_tabs
    nb = cfg.tokens // tm
    assert tm % (nj - 1) == 0
    tab_spec = pl.BlockSpec((tm, cfg.head_dim),
                            lambda i, j: (jnp.where(i < npb, 0, 1 + (i - npb) % per), 0))

    def x_blk(i, j):
        return jnp.minimum(i + jnp.minimum(j, 1), nb - 1)

    def main_blk(i, j):
        t = jnp.maximum(j - 1, 0)
        return (i, jnp.where(t < kvj, t, jnp.maximum(t - 1, kvj - 1)))

    return pl.pallas_call(
        functools.partial(_in_proj_kernel, cfg=cfg, kinds=kinds),
        grid=(nb, nj + 1),
        in_specs=[
            pl.BlockSpec((tm, d), lambda i, j: (x_blk(i, j), 0)),
            pl.BlockSpec((None, None, 3, d),
                         lambda i, j: (layer, _row_group(x_blk(i, j), cfg, tm), 0, 0)),
            pl.BlockSpec((None, 1, d), lambda i, j: (layer, 0, 0)),
            pl.BlockSpec((None, d, tn), lambda i, j: (layer, 0, jnp.minimum(j, nj - 1))),
            pl.BlockSpec((None, 2, cfg.head_dim), lambda i, j: (layer, 0, 0)),
            tab_spec, tab_spec, tab_spec,
        ],
        out_specs=[
            pl.BlockSpec((tm, tn), lambda i, j: (i, 0)),
            pl.BlockSpec((tm, tn), main_blk),
        ],
        out_shape=[
            jax.ShapeDtypeStruct((cfg.tokens, tn), F32),
            jax.ShapeDtypeStruct((cfg.tokens, cfg.main_width), BF16),
        ],
        scratch_shapes=[pltpu.VMEM((tm, d), BF16), pltpu.VMEM((tm, d), BF16),
                        pltpu.VMEM((tm, tn), F32), pltpu.VMEM((tm, tn), F32)],
        compiler_params=_params("arbitrary", "arbitrary"),
        name="in_proj",
    )(x, mod, norm_g, w_in_b, gains, cos_t, sa_t, sb_t)


def _softmax_attend(q, k, v1):
    hd = q.shape[1]
    s = lax.dot_general(q, k, (((1,), (1,)), ((), ())), preferred_element_type=F32)
    m = jnp.max(s, axis=-1, keepdims=True)
    p = jnp.exp2(s - m).astype(BF16)
    o = jnp.dot(p, v1, preferred_element_type=F32)
    return o[:, :hd] / o[:, hd:]


def _attn_ctx_kernel(prev_ref, q_ref, kv_ref, g_ref, o_ref, *, cfg):
    del prev_ref
    hd, kvw = cfg.head_dim, cfg.kv_width
    for kh in range(cfg.n_kv):
        k = kv_ref[:, kh * hd:(kh + 1) * hd].astype(BF16)
        v = kv_ref[:, kvw + kh * hd:kvw + (kh + 1) * hd].astype(BF16)
        v1 = jnp.concatenate([v, jnp.ones_like(v)], axis=1)
        for h in range(kh * cfg.q_per_kv, (kh + 1) * cfg.q_per_kv):
            sl = slice(h * hd, (h + 1) * hd)
            o = _softmax_attend(q_ref[:, sl], k, v1)
            o_ref[:, sl] = (o * g_ref[:, sl].astype(F32)).astype(BF16)


def _attn_context(attn_buf, main, kv, cfg):
    aw = cfg.attn_width
    return pl.pallas_call(
        functools.partial(_attn_ctx_kernel, cfg=cfg),
        grid=(cfg.batch,),
        in_specs=[
            pl.BlockSpec(memory_space=pl.ANY),
            pl.BlockSpec((cfg.seq, aw), lambda b: (b, 0)),
            pl.BlockSpec((cfg.seq, 2 * cfg.kv_width), lambda b: (b, 0)),
            pl.BlockSpec((cfg.seq, aw), lambda b: (b, 1)),
        ],
        out_specs=pl.BlockSpec((cfg.seq, aw), lambda b: (b, 0)),
        out_shape=jax.ShapeDtypeStruct((cfg.tokens, aw), BF16),
        input_output_aliases={0: 0},
        compiler_params=_params("parallel"),
        name="attn_context",
    )(attn_buf, main, kv, main)


def _attn_lat_kernel(prev_ref, q_ref, ck_ref, cv_ref, kn_ref, vn_ref, g_ref, o_ref,
                     k_scr, v_scr, *, cfg):
    del prev_ref
    hd, past = cfg.head_dim, cfg.past_len

    @pl.when(pl.program_id(2) == 0)
    def _():
        k_scr[0:past, :] = ck_ref[...].astype(BF16)
        k_scr[past:, :] = kn_ref[...].astype(BF16)
        v_scr[0:past, 0:hd] = cv_ref[...].astype(BF16)
        v_scr[past:, 0:hd] = vn_ref[...].astype(BF16)
        v_scr[:, hd:] = jnp.ones((v_scr.shape[0], hd), BF16)

    k = k_scr[...]
    v1 = v_scr[...]
    rq = min(cfg.rq, q_ref.shape[0])
    for h in range(cfg.q_per_kv):
        sl = slice(h * hd, (h + 1) * hd)
        for r in range(q_ref.shape[0] // rq):
            rows = slice(r * rq, (r + 1) * rq)
            o = _softmax_attend(q_ref[rows, sl], k, v1)
            o_ref[rows, sl] = (o * g_ref[rows, sl].astype(F32)).astype(BF16)


def _attn_latent(attn_prev, main, kv, cache_k, cache_v, layer, cfg):
    hd, nkv, tq = cfg.head_dim, cfg.n_kv, cfg.tq
    qw = cfg.q_per_kv * hd
    nqb = cfg.dec_seq // tq
    row0 = cfg.p_tokens // tq
    kvrow0 = cfg.p_tokens // cfg.dec_seq
    nkeys = cfg.past_len + cfg.dec_seq

    def q_map(b, h, qi):
        return (row0 + b * nqb + qi, h)

    def g_map(b, h, qi):
        return (row0 + b * nqb + qi, nkv + h)

    cache_spec = pl.BlockSpec((None, None, cfg.past_len, hd), lambda b, h, qi: (b, layer, 0, h))
    return pl.pallas_call(
        functools.partial(_attn_lat_kernel, cfg=cfg),
        grid=(cfg.dec_batch, nkv, nqb),
        in_specs=[
            pl.BlockSpec(memory_space=pl.ANY),
            pl.BlockSpec((tq, qw), q_map),
            cache_spec, cache_spec,
            pl.BlockSpec((cfg.dec_seq, hd), lambda b, h, qi: (kvrow0 + b, h)),
            pl.BlockSpec((cfg.dec_seq, hd), lambda b, h, qi: (kvrow0 + b, nkv + h)),
            pl.BlockSpec((tq, qw), g_map),
        ],
        out_specs=pl.BlockSpec((tq, qw), q_map),
        out_shape=jax.ShapeDtypeStruct((cfg.tokens, cfg.attn_width), BF16),
        scratch_shapes=[pltpu.VMEM((nkeys, hd), BF16), pltpu.VMEM((nkeys, 2 * hd), BF16)],
        input_output_aliases={0: 0},
        compiler_params=_params("parallel", "parallel", "arbitrary"),
        name="attn_latent",
    )(attn_prev, main, cache_k, cache_v, kv, kv, main)


def _ssm_kernel(u_ref, perm_ref, catf_ref, catb_ref, bbt_ref, pa_ref, w2_ref, av_ref,
                pw_ref, dsk_ref, h0_ref, y_ref, fin_ref,
                uf_scr, xcat_scr, u8_scr, m_scr, w1s_scr, s_scr, hin_scr, f_scr, g_scr, *, cfg):
    T, C = SSM_CHUNK, cfg.ssm_group
    tc = T * C
    st = 2 * cfg.ssm_state
    n_k = cfg.n_k
    chain = T * n_k
    nc = cfg.p_tokens // chain
    n_sc = cfg.dec_seq // chain
    gt = LANES // C
    jl_n = LANES // C
    jh_n = T // jl_n
    is_latent = pl.program_id(1) == 1
    y8_scr = xcat_scr
    pitch = chain + SSM_ROW_PAD

    def stage_in(c, carry):
        src = pl.ds(pl.multiple_of(c * chain, chain), chain)
        uf_scr[pl.ds(pl.multiple_of(c * pitch, 8), chain), :] = u_ref[src, :].astype(F32)
        return carry

    lax.fori_loop(0, nc, stage_in, 0)

    def gather(k, carry):
        for j in range(T):
            piece = uf_scr[pl.ds(k * T + j, nc, stride=pitch), :].astype(BF16)
            xcat_scr[j // jl_n, pl.ds(pl.multiple_of(k * nc, nc), nc),
                     (j % jl_n) * LANES:(j % jl_n + 1) * LANES] = piece
        return carry

    lax.fori_loop(0, n_k, gather, 0)
    for jh in range(jh_n):
        t = jnp.dot(xcat_scr[jh], perm_ref[...], preferred_element_type=F32)
        for r in range(gt):
            u8_scr[r, :, jh * LANES:(jh + 1) * LANES] = t[:, r * LANES:(r + 1) * LANES].astype(BF16)

    def cmul(h, hs, c1, c2):
        return h * c1 + hs * c2, hs * c1 - h * c2

    lane = lax.broadcasted_iota(jnp.int32, (C, tc), 1)

    for r in range(gt):
        ktf = jnp.dot(bbt_ref[r, 0], catf_ref[r], preferred_element_type=F32, precision=HIGHEST)
        ktb = jnp.dot(bbt_ref[r, 1], catb_ref[r], preferred_element_type=F32, precision=HIGHEST)
        for j in range(T):
            rf = pltpu.roll(ktf, C * j, 1) if j else ktf
            rf = jnp.where(lane >= C * j, rf, 0.0)
            sh = (T - 1 - j) * C
            rb = pltpu.roll(ktb, tc - sh, 1) if sh else ktb
            rb = jnp.where(lane < tc - sh, rb, 0.0)
            m_scr[j * C:(j + 1) * C, :] = (rf + rb).astype(BF16)
        for d in range(2):
            bb = bbt_ref[r, d]
            bbs = pltpu.roll(bb, st // 2, 1)
            for j in range(T):
                w, ws = cmul(bb, bbs, pa_ref[r, d, j, 0:1, :], pa_ref[r, d, j, 1:2, :])
                w1s_scr[j * C:(j + 1) * C, 2 * d * st:(2 * d + 1) * st] = w.astype(BF16)
                w1s_scr[j * C:(j + 1) * C, (2 * d + 1) * st:(2 * d + 2) * st] = ws.astype(BF16)

        u = u8_scr[r]
        y_intra = jnp.dot(u, m_scr[...], preferred_element_type=F32)
        s_scr[...] = jnp.dot(u, w1s_scr[...], preferred_element_type=F32)
        av = av_ref[r]
        a1 = (av[0:1], av[2:3])
        a2 = (av[1:2], av[3:4])
        b1 = (av[4:5], av[6:7])
        b2 = (av[5:6], av[7:8])

        def level1(k, carry):
            hf, hfs, hb, hbs = carry
            rf_ = pl.ds(pl.multiple_of(k * nc, nc), nc)
            rb_ = pl.ds(pl.multiple_of((n_k - 1 - k) * nc, nc), nc)
            hin_scr[rf_, 0:st] = hf
            hin_scr[rf_, st:2 * st] = hfs
            hin_scr[rb_, 2 * st:3 * st] = hb
            hin_scr[rb_, 3 * st:4 * st] = hbs
            nf, nfs = cmul(hf, hfs, a1[0], a2[0])
            nbk, nbs = cmul(hb, hbs, a1[1], a2[1])
            return (nf + s_scr[rf_, 0:st], nfs + s_scr[rf_, st:2 * st],
                    nbk + s_scr[rb_, 2 * st:3 * st], nbs + s_scr[rb_, 3 * st:4 * st])

        z = jnp.zeros((nc, st), F32)
        ff, ffs, fb, fbs = lax.fori_loop(0, n_k, level1, (z, z, z, z))

        @pl.when(jnp.logical_not(is_latent))
        def _():
            fin_ref[r, 0] = ff
            fin_ref[r, 1] = fb

        @pl.when(is_latent)
        def _():
            for idx, val in enumerate((ff, ffs, fb, fbs)):
                f_scr[idx] = val
            h0 = h0_ref[r]
            gf, gfs, gb, gbs = h0[0], h0[1], h0[2], h0[3]
            for sc in range(n_sc):
                sel = pl.ds(sc, cfg.dec_batch, stride=n_sc)
                g_scr[0, sel, :] = gf
                g_scr[1, sel, :] = gfs
                nf, nfs = cmul(gf, gfs, b1[0], b2[0])
                gf = nf + f_scr[0, sel, :]
                gfs = nfs + f_scr[1, sel, :]
            for sc in range(n_sc - 1, -1, -1):
                sel = pl.ds(sc, cfg.dec_batch, stride=n_sc)
                g_scr[2, sel, :] = gb
                g_scr[3, sel, :] = gbs
                nbk, nbs = cmul(gb, gbs, b1[1], b2[1])
                gb = nbk + f_scr[2, sel, :]
                gbs = nbs + f_scr[3, sel, :]
            gfa, gfsa, gba, gbsa = g_scr[0], g_scr[1], g_scr[2], g_scr[3]
            for k in range(n_k):
                rows_k = slice(k * nc, (k + 1) * nc)
                pf = pw_ref[r, k]
                cf, cfs = cmul(gfa, gfsa, pf[0:1], pf[1:2])
                cb, cbs = cmul(gba, gbsa, pf[2:3], pf[3:4])
                hin_scr[rows_k, 0:st] += cf
                hin_scr[rows_k, st:2 * st] += cfs
                hin_scr[rows_k, 2 * st:3 * st] += cb
                hin_scr[rows_k, 3 * st:4 * st] += cbs

        hsel = jnp.concatenate([hin_scr[:, 0:st], hin_scr[:, 2 * st:3 * st]], axis=1).astype(BF16)
        y = y_intra + jnp.dot(hsel, w2_ref[r], preferred_element_type=F32)
        y = (y + dsk_ref[r] * u.astype(F32)).astype(BF16)
        for jh in range(jh_n):
            y8_scr[jh, :, r * LANES:(r + 1) * LANES] = y[:, jh * LANES:(jh + 1) * LANES]

    for jh in range(jh_n):
        t = lax.dot_general(y8_scr[jh], perm_ref[...], (((1,), (1,)), ((), ())),
                            preferred_element_type=F32)
        for k in range(n_k):
            for jl in range(jl_n):
                uf_scr[pl.ds(k * T + jh * jl_n + jl, nc, stride=pitch), :] = (
                    t[k * nc:(k + 1) * nc, jl * LANES:(jl + 1) * LANES])

    def stage_out(c, carry):
        dst = pl.ds(pl.multiple_of(c * chain, chain), chain)
        y_ref[dst, :] = uf_scr[pl.ds(pl.multiple_of(c * pitch, 8), chain), :].astype(BF16)
        return carry

    lax.fori_loop(0, nc, stage_out, 0)


def _ssm_scan(main, ssm_w, h0, layer, cfg):
    perm, catf, catb, bbt, pa, w2, av, pw, dsk = ssm_w
    T, C = SSM_CHUNK, cfg.ssm_group
    tc = T * C
    st = 2 * cfg.ssm_state
    gt = LANES // C
    nq = cfg.ssm_width // LANES
    chain = T * cfg.n_k
    half = cfg.p_tokens
    assert cfg.seq == chain and cfg.dec_seq % chain == 0 and T % gt == 0 and cfg.s_tokens == half
    nc = half // chain
    rows = nc * cfg.n_k
    pw_n = gt * LANES
    ucol = 2 * cfg.attn_width // LANES

    def wspec(*shape):
        nd = len(shape)
        return pl.BlockSpec((None, gt) + shape, lambda q, s: (layer, q) + (0,) * nd)

    return pl.pallas_call(
        functools.partial(_ssm_kernel, cfg=cfg),
        grid=(nq, 2),
        in_specs=[
            pl.BlockSpec((half, LANES), lambda q, s: (s, ucol + q)),
            pl.BlockSpec((pw_n, pw_n), lambda q, s: (0, 0)),
            wspec(st, tc), wspec(st, tc), wspec(2, C, st), wspec(2, T, 2, st), wspec(2 * st, tc),
            wspec(8, st), wspec(cfg.n_k, 4, st), wspec(1, tc), wspec(4, cfg.dec_batch, st),
        ],
        out_specs=[
            pl.BlockSpec((half, LANES), lambda q, s: (s, q)),
            pl.BlockSpec((gt, 2, cfg.batch, st), lambda q, s: (q, 0, 0, 0)),
        ],
        out_shape=[
            jax.ShapeDtypeStruct((cfg.tokens, cfg.ssm_width), BF16),
            jax.ShapeDtypeStruct((cfg.n_groups, 2, cfg.batch, st), F32),
        ],
        scratch_shapes=[
            pltpu.VMEM((nc * (chain + SSM_ROW_PAD), LANES), F32),
            pltpu.VMEM((T // gt, rows, pw_n), BF16),
            pltpu.VMEM((gt, rows, tc), BF16),
            pltpu.VMEM((tc, tc), BF16),
            pltpu.VMEM((tc, 4 * st), BF16),
            pltpu.VMEM((rows, 4 * st), F32),
            pltpu.VMEM((rows, 4 * st), F32),
            pltpu.VMEM((4, nc, st), F32),
            pltpu.VMEM((4, nc, st), F32),
        ],
        compiler_params=_params("parallel", "arbitrary"),
        name="ssm_scan",
    )(main, perm, catf, catb, bbt, pa, w2, av, pw, dsk, h0)


def _ssm_weights(lam_re, lam_im, log_step, b_re, b_im, c_re, c_im, d_skip, cfg):
    T, C, P, G, nk = SSM_CHUNK, cfg.ssm_group, cfg.ssm_state, cfg.n_groups, cfg.n_k
    lr, li = lam_re.astype(F32), lam_im.astype(F32)
    dt = jnp.exp(log_step.astype(F32))[..., None]

    def powers(js):
        jj = jnp.asarray(js, F32)[:, None]
        mag = jnp.exp(lr[..., None, :] * dt[..., None, :] * jj)
        ang = li[..., None, :] * dt[..., None, :] * jj
        return mag * jnp.cos(ang), mag * jnp.sin(ang)

    pr, pi = powers(range(T + 1))
    ab_re, ab_im = pr[..., 1, :], pi[..., 1, :]
    nr, ni = ab_re - 1.0, ab_im
    den = lr * lr + li * li
    f_re = (nr * lr + ni * li) / den
    f_im = (ni * lr - nr * li) / den
    br, bi = b_re.astype(F32), b_im.astype(F32)
    bb_re = f_re[..., None] * br - f_im[..., None] * bi
    bb_im = f_re[..., None] * bi + f_im[..., None] * br
    cr, ci = c_re.astype(F32), c_im.astype(F32)
    ca_re = cr[..., None, :, :] * pr[..., :, None, :] - ci[..., None, :, :] * pi[..., :, None, :]
    ca_im = cr[..., None, :, :] * pi[..., :, None, :] + ci[..., None, :, :] * pr[..., :, None, :]

    def state_out(d, lo, rev):
        def pick(x):
            x = x[:, d, :, lo:lo + T]
            x = jnp.flip(x, axis=2) if rev else x
            return x.transpose(0, 1, 4, 2, 3).reshape(-1, G, P, T * C)
        return jnp.concatenate([pick(ca_re), -pick(ca_im)], axis=2)

    catf = state_out(0, 0, False)
    catb = state_out(1, 0, True)
    w2 = jnp.concatenate([state_out(0, 1, False), state_out(1, 1, True)], axis=2).astype(BF16)
    bbt = jnp.concatenate([bb_re, bb_im], axis=3).transpose(0, 2, 1, 4, 3)

    def lanes(re, im):
        return jnp.concatenate([re, re], axis=-1), jnp.concatenate([-im, im], axis=-1)

    paf = lanes(jnp.flip(pr[:, 0, :, 0:T], axis=2), jnp.flip(pi[:, 0, :, 0:T], axis=2))
    pab = lanes(pr[:, 1, :, 0:T], pi[:, 1, :, 0:T])
    pa = jnp.stack([jnp.stack(paf, axis=3), jnp.stack(pab, axis=3)], axis=2)

    qr, qi = powers([T * k for k in range(nk + 1)])
    rows = []
    for src_r, src_i, idx in ((pr, pi, T), (qr, qi, nk)):
        for d in range(2):
            rows.extend(lanes(src_r[:, d, :, idx], src_i[:, d, :, idx]))
    av = jnp.stack(rows, axis=2)
    p1f, p2f = lanes(qr[:, 0, :, 0:nk], qi[:, 0, :, 0:nk])
    p1b, p2b = lanes(jnp.flip(qr[:, 1, :, 0:nk], axis=2), jnp.flip(qi[:, 1, :, 0:nk], axis=2))
    pw = jnp.stack([p1f, p2f, p1b, p2b], axis=3)
    dsk = jnp.tile(d_skip.astype(F32).reshape(-1, G, 1, C), (1, 1, 1, T))

    gt = LANES // C
    n = gt * LANES
    src = jnp.arange(n)
    jl, r, c = src // LANES, (src % LANES) // C, src % C
    dst = r * LANES + jl * C + c
    perm = (dst[:, None] == jnp.arange(n)[None, :]).astype(BF16)
    return perm, catf, catb, bbt, pa, w2, av, pw, dsk


def _glu_kernel(y_ref, w_ref, g_ref, o_ref, *, cfg):
    z = jnp.dot(y_ref[...], w_ref[...], preferred_element_type=F32)
    sw = cfg.ssm_width
    a, g = z[:, :sw], z[:, sw:]
    o_ref[...] = (a / (1.0 + jnp.exp(-g)) * g_ref[...].astype(F32)).astype(BF16)


def _ssm_glu(y, w_glu_b, main, layer, cfg):
    tm, sw = cfg.tm_post, cfg.ssm_width
    gcol = (2 * cfg.attn_width + sw) // sw
    assert (2 * cfg.attn_width) % sw == 0
    return pl.pallas_call(
        functools.partial(_glu_kernel, cfg=cfg),
        grid=(cfg.tokens // tm,),
        in_specs=[
            pl.BlockSpec((tm, sw), lambda i: (i, 0)),
            pl.BlockSpec((None, sw, 2 * sw), lambda i: (layer, 0, 0)),
            pl.BlockSpec((tm, sw), lambda i: (i, gcol)),
        ],
        out_specs=pl.BlockSpec((tm, sw), lambda i: (i, 0)),
        out_shape=jax.ShapeDtypeStruct((cfg.tokens, sw), BF16),
        compiler_params=_params("parallel"),
        name="ssm_glu",
    )(y, w_glu_b, main)


def _fourier_kernel(prev_ref, x_ref, dl_ref, cs_ref, w_ref, g_ref, o_ref, z_scr, *, cfg, length):
    del prev_ref
    fg = cfg.fft_group
    rc = min(cfg.rc_fft, length)

    @pl.when(pl.program_id(1) == 0)
    def _():
        def body(ci, carry):
            rows = pl.ds(pl.multiple_of(ci * rc, rc), rc)
            rows2 = pl.ds(pl.multiple_of(length + ci * rc, rc), rc)
            for g in range(cfg.fft_width // fg):
                cols = slice(g * fg, (g + 1) * fg)
                t = jnp.dot(x_ref[rows, cols], cs_ref[...], preferred_element_type=F32)
                z_scr[rows, cols] = t[:, :fg].astype(BF16)
                z_scr[rows2, cols] = t[:, fg:].astype(BF16)
            return carry

        lax.fori_loop(0, length // rc, body, 0)

    mixed = jnp.dot(dl_ref[...], z_scr[...], preferred_element_type=F32).astype(BF16)
    four = jnp.dot(mixed, w_ref[...], preferred_element_type=F32)
    o_ref[...] = (four * g_ref[...].astype(F32)).astype(BF16)


def _fourier(prev, main, dl, cs, w_fft_b, layer, cfg, *, length, nbatch, row0):
    fw = cfg.fft_width
    tm = min(cfg.tm_fft, length)
    nr = length // tm
    xcol = (2 * cfg.attn_width + 2 * cfg.ssm_width) // fw
    assert (2 * cfg.attn_width + 2 * cfg.ssm_width) % fw == 0 and row0 % length == 0
    b0 = row0 // length
    t0 = row0 // tm

    def tile_map(b, r):
        return (t0 + b * nr + r, 0)

    return pl.pallas_call(
        functools.partial(_fourier_kernel, cfg=cfg, length=length),
        grid=(nbatch, nr),
        in_specs=[
            pl.BlockSpec(memory_space=pl.ANY),
            pl.BlockSpec((length, fw), lambda b, r: (b0 + b, xcol)),
            pl.BlockSpec((tm, 2 * length), lambda b, r: (r, 0)),
            pl.BlockSpec((cfg.fft_group, 2 * cfg.fft_group), lambda b, r: (0, 0)),
            pl.BlockSpec((None, fw, fw), lambda b, r: (layer, 0, 0)),
            pl.BlockSpec((tm, fw), lambda b, r: (t0 + b * nr + r, xcol + 1)),
        ],
        out_specs=pl.BlockSpec((tm, fw), tile_map),
        out_shape=jax.ShapeDtypeStruct((cfg.tokens, fw), BF16),
        scratch_shapes=[pltpu.VMEM((2 * length, fw), BF16)],
        input_output_aliases={0: 0},
        compiler_params=_params("parallel", "arbitrary"),
        name=f"fourier_{length}",
    )(prev, main, dl, cs, w_fft_b, main)


def _dft_cos_sin(n):
    k = jnp.arange(n, dtype=jnp.int32)
    w = 2.0 * math.pi / n

    ang = ((k[:, None] * k[None, :]) % n).astype(F32) * w
    scale = n ** -0.5
    return jnp.cos(ang) * scale, jnp.sin(ang) * scale


def _dft_position_table(n):
    s = 1
    while s * s < n:
        s *= 2
    if n <= 512 or n % s:
        c, sn = _dft_cos_sin(n)
        return jnp.concatenate([c, -sn], axis=1).astype(BF16)
    k = jnp.arange(n, dtype=jnp.int32)
    w = 2.0 * math.pi / n

    def cs(j):
        ang = ((j[:, None] * k[None, :]) % n).astype(F32) * w
        return jnp.cos(ang), jnp.sin(ang)

    c1, s1 = cs(jnp.arange(n // s, dtype=jnp.int32) * s)
    c0, s0 = cs(jnp.arange(s, dtype=jnp.int32))
    a1 = jnp.concatenate([c1, -s1], axis=1)[:, None, :]
    a2 = jnp.concatenate([s1, c1], axis=1)[:, None, :]
    b0 = jnp.concatenate([c0, c0], axis=1)[None]
    b1 = jnp.concatenate([s0, s0], axis=1)[None]
    return ((a1 * b0 - a2 * b1) * n ** -0.5).reshape(n, 2 * n).astype(BF16)


def _out_proj_kernel(a_ref, s_ref, f_ref, w_ref, x_ref, mod_ref, o_ref, *, cfg):
    aw, sw = cfg.attn_width, cfg.ssm_width
    acc = jnp.dot(a_ref[...], w_ref[0:aw, :], preferred_element_type=F32)
    acc += jnp.dot(s_ref[...], w_ref[aw:aw + sw, :], preferred_element_type=F32)
    acc += jnp.dot(f_ref[...], w_ref[aw + sw:, :], preferred_element_type=F32)
    o_ref[...] = x_ref[...] + mod_ref[2:3, :] * acc


def _out_proj(attn, ssm, four, w_out_b, x, mod, layer, cfg):
    tm, tn, d = cfg.tm_out, cfg.tn_out, cfg.d_model
    return pl.pallas_call(
        functools.partial(_out_proj_kernel, cfg=cfg),
        grid=(d // tn, cfg.tokens // tm),
        in_specs=[
            pl.BlockSpec((tm, cfg.attn_width), lambda j, i: (i, 0)),
            pl.BlockSpec((tm, cfg.ssm_width), lambda j, i: (i, 0)),
            pl.BlockSpec((tm, cfg.fft_width), lambda j, i: (i, 0)),
            pl.BlockSpec((None, d, tn), lambda j, i: (layer, 0, j)),
            pl.BlockSpec((tm, tn), lambda j, i: (i, j)),
            pl.BlockSpec((None, None, 3, tn), lambda j, i: (layer, _row_group(i, cfg, tm), 0, j)),
        ],
        out_specs=pl.BlockSpec((tm, tn), lambda j, i: (i, j)),
        out_shape=jax.ShapeDtypeStruct((cfg.tokens, d), F32),
        compiler_params=_params("parallel", "parallel"),
        name="out_proj",
    )(attn, ssm, four, w_out_b, x, mod)


def _final_norm_kernel(x_ref, g_ref, o_ref):
    x = x_ref[...]
    ms = jnp.mean(x * x, axis=-1, keepdims=True)
    o_ref[...] = x * lax.rsqrt(ms + NORM_EPS) * g_ref[...]


def _final_norm(x, g, cfg, *, row0, nrows):
    tm, d = 256, cfg.d_model
    tm = min(tm, nrows)
    assert row0 % tm == 0
    return pl.pallas_call(
        _final_norm_kernel,
        grid=(nrows // tm,),
        in_specs=[pl.BlockSpec((tm, d), lambda i: (row0 // tm + i, 0)),
                  pl.BlockSpec((1, d), lambda i: (0, 0))],
        out_specs=pl.BlockSpec((tm, d), lambda i: (i, 0)),
        out_shape=jax.ShapeDtypeStruct((nrows, d), F32),
        compiler_params=_params("parallel"),
        name="final_norm",
    )(x, g)


def _rope_tables(cfg):
    hd = cfg.head_dim
    pairs = hd // 4
    t = jnp.arange(cfg.dec_seq)
    inv = ROPE_THETA ** (-jnp.arange(pairs, dtype=F32) / pairs)
    row_ang = (t // cfg.grid_w).astype(F32)[:, None] * inv[None, :]
    col_ang = (t % cfg.grid_w).astype(F32)[:, None] * inv[None, :]
    zeros = jnp.zeros_like(row_ang)
    cos = jnp.concatenate([jnp.cos(row_ang)] * 2 + [jnp.cos(col_ang)] * 2, axis=1)
    sa = jnp.concatenate([-jnp.sin(row_ang), zeros, -jnp.sin(col_ang), zeros], axis=1)
    sb = jnp.concatenate([zeros, jnp.sin(row_ang), zeros, jnp.sin(col_ang)], axis=1)
    ident = jnp.zeros((cfg.tm_in, hd), F32)
    return (jnp.concatenate([ident + 1.0, cos], axis=0),
            jnp.concatenate([ident, sa], axis=0),
            jnp.concatenate([ident, sb], axis=0))


def _forward(cfg, x_prompt, x_sample, cache_k, cache_v, state_fwd_re, state_fwd_im, state_bwd_re,
             state_bwd_im, c, c_ctx, norm_g, w_mod, b_mod, w_in, q_norm, k_norm, lam_re, lam_im,
             log_step, b_re, b_im, c_re, c_im, d_skip, w_glu, w_fft, w_out, final_norm_g):
    d, kvw = cfg.d_model, cfg.kv_width
    P = cfg.ssm_state

    w_in_b = w_in.astype(BF16)
    w_out_b = w_out.astype(BF16)
    w_glu_b = w_glu.astype(BF16)
    w_fft_b = w_fft.astype(BF16)
    gains = jnp.stack([k_norm, q_norm], axis=1).astype(F32)
    norm_g3 = norm_g.astype(F32)[:, None, :]
    ssm_w = _ssm_weights(lam_re, lam_im, log_step, b_re, b_im, c_re, c_im, d_skip, cfg)
    rope_tabs = _rope_tables(cfg)

    cc, sc = _dft_cos_sin(cfg.fft_group)
    cs = jnp.concatenate([cc, sc], axis=1).astype(BF16)
    dls = {}
    for length in {cfg.seq, cfg.dec_seq}:
        dls[length] = _dft_position_table(length)

    def st(re, im):
        re = re.astype(F32).transpose(1, 2, 0, 3)
        im = im.astype(F32).transpose(1, 2, 0, 3)
        return jnp.concatenate([re, im], axis=-1), jnp.concatenate([im, re], axis=-1)

    h0 = jnp.stack(st(state_fwd_re, state_fwd_im) + st(state_bwd_re, state_bwd_im), axis=2)

    nrow = 1 + cfg.dec_batch
    cvecs = jnp.concatenate([c_ctx[None, :], c], axis=0).astype(F32)
    cvecs = jnp.pad(cvecs, ((0, -nrow % 8), (0, 0)))
    mod = _modulation(cvecs, w_mod, b_mod, cfg).reshape(cfg.depth, cvecs.shape[0], 3, d)

    cache_k4 = cache_k.reshape(cfg.dec_batch, cfg.depth, cfg.past_len, kvw)
    cache_v4 = cache_v.reshape(cfg.dec_batch, cfg.depth, cfg.past_len, kvw)

    x = jnp.concatenate([x_prompt.reshape(cfg.p_tokens, d), x_sample.reshape(cfg.s_tokens, d)], axis=0)
    attn = jnp.zeros((cfg.tokens, cfg.attn_width), BF16)
    four = jnp.zeros((cfg.tokens, cfg.fft_width), BF16)
    ks, vs, fins = [], [], []
    for l in range(cfg.depth):
        kv, main = _in_proj(x, mod, norm_g3, w_in_b, gains, rope_tabs, l, cfg)
        attn = _attn_context(attn, main, kv, cfg)
        attn = _attn_latent(attn, main, kv, cache_k4, cache_v4, l, cfg)
        y, fin = _ssm_scan(main, ssm_w, h0, l, cfg)
        ssm = _ssm_glu(y, w_glu_b, main, l, cfg)
        four = _fourier(four, main, dls[cfg.seq], cs, w_fft_b, l, cfg,
                        length=cfg.seq, nbatch=cfg.batch, row0=0)
        four = _fourier(four, main, dls[cfg.dec_seq], cs, w_fft_b, l, cfg,
                        length=cfg.dec_seq, nbatch=cfg.dec_batch, row0=cfg.p_tokens)
        x = _out_proj(attn, ssm, four, w_out_b, x, mod, l, cfg)
        ks.append(kv[:cfg.p_tokens, :kvw].reshape(cfg.batch, cfg.seq, cfg.n_kv, cfg.head_dim))
        vs.append(kv[:cfg.p_tokens, kvw:].reshape(cfg.batch, cfg.seq, cfg.n_kv, cfg.head_dim))
        fins.append(fin)

    g = final_norm_g[None, :].astype(F32)
    y_prompt = _final_norm(x, g, cfg, row0=0, nrows=cfg.p_tokens).reshape(cfg.batch, cfg.seq, d)
    y_sample = _final_norm(x, g, cfg, row0=cfg.p_tokens, nrows=cfg.s_tokens).reshape(
        cfg.dec_batch, cfg.dec_seq, d)
    fin = jnp.stack(fins, axis=0)
    fin = fin.transpose(3, 0, 1, 2, 4)
    return (y_prompt, y_sample, jnp.stack(ks, axis=1), jnp.stack(vs, axis=1),
            fin[:, :, :, 0, :P], fin[:, :, :, 0, P:], fin[:, :, :, 1, :P], fin[:, :, :, 1, P:])


def kernel(x_prompt, x_sample, cache_k, cache_v, state_fwd_re, state_fwd_im, state_bwd_re, state_bwd_im,
           c, c_ctx, norm_g, w_mod, b_mod, w_in, q_norm, k_norm, lam_re, lam_im, log_step,
           b_re, b_im, c_re, c_im, d_skip, w_glu, w_fft, w_out, final_norm_g):
    return _forward(Cfg(), x_prompt, x_sample, cache_k, cache_v, state_fwd_re, state_fwd_im,
                    state_bwd_re, state_bwd_im, c, c_ctx, norm_g, w_mod, b_mod, w_in, q_norm, k_norm,
                    lam_re, lam_im, log_step, b_re, b_im, c_re, c_im, d_skip, w_glu, w_fft, w_out,
                    final_norm_g)
```

```python
import functools
import math
from typing import NamedTuple

import jax
import jax.numpy as jnp
from jax import lax
from jax.experimental import pallas as pl
from jax.experimental.pallas import tpu as pltpu

F32 = jnp.float32
BF16 = jnp.bfloat16
NORM_EPS = 1e-6
ROPE_THETA = 10000.0
LANES = 128
SSM_CHUNK = 16
SSM_ROW_PAD = 8
VMEM_LIMIT = 56 * 1024 * 1024
LOG2E = 1.4426950408889634
HIGHEST = lax.Precision.HIGHEST


class Cfg(NamedTuple):
    d_model: int = 4096
    batch: int = 32
    seq: int = 256
    depth: int = 4
    dec_batch: int = 2
    dec_seq: int = 4096
    past_len: int = 512
    grid_w: int = 64
    head_dim: int = 128
    n_heads: int = 16
    n_kv: int = 4
    ssm_width: int = 1024
    ssm_group: int = 16
    ssm_state: int = 64
    fft_width: int = 1024
    fft_group: int = 256
    tm_in: int = 512
    tm_out: int = 512
    tn_out: int = 1024
    tn_mod: int = 512
    tq: int = 512
    rq: int = 128
    tm_fft: int = 256
    rc_fft: int = 512
    tm_post: int = 512
    n_k: int = 16

    @property
    def attn_width(self):
        return self.n_heads * self.head_dim

    @property
    def kv_width(self):
        return self.n_kv * self.head_dim

    @property
    def q_per_kv(self):
        return self.n_heads // self.n_kv

    @property
    def n_groups(self):
        return self.ssm_width // self.ssm_group

    @property
    def tn_in(self):
        return 2 * self.kv_width

    @property
    def main_width(self):
        return 2 * self.attn_width + 2 * self.ssm_width + 2 * self.fft_width

    @property
    def p_tokens(self):
        return self.batch * self.seq

    @property
    def s_tokens(self):
        return self.dec_batch * self.dec_seq

    @property
    def tokens(self):
        return self.p_tokens + self.s_tokens


def _params(*sem):
    return pltpu.CompilerParams(dimension_semantics=sem, vmem_limit_bytes=VMEM_LIMIT)


def _silu(x):
    return x / (1.0 + jnp.exp(-x))


def _row_group(i, cfg, tm):
    npb = cfg.p_tokens // tm
    per = cfg.dec_seq // tm
    return jnp.where(i < npb, 0, 1 + (i - npb) // per)


def _mod_kernel(c_ref, w_ref, b_ref, o_ref):
    s = _silu(c_ref[...]).astype(BF16)
    o_ref[0] = jnp.dot(s, w_ref[0].astype(BF16), preferred_element_type=F32) + b_ref[0]


def _modulation(cvecs, w_mod, b_mod, cfg):
    d, n = cfg.d_model, 3 * cfg.d_model
    nrow = cvecs.shape[0]
    return pl.pallas_call(
        _mod_kernel,
        grid=(cfg.depth, n // cfg.tn_mod),
        in_specs=[
            pl.BlockSpec((nrow, d), lambda l, j: (0, 0)),
            pl.BlockSpec((1, d, cfg.tn_mod), lambda l, j: (l, 0, j)),
            pl.BlockSpec((1, 1, cfg.tn_mod), lambda l, j: (l, 0, j)),
        ],
        out_specs=pl.BlockSpec((1, nrow, cfg.tn_mod), lambda l, j: (l, 0, j)),
        out_shape=jax.ShapeDtypeStruct((cfg.depth, nrow, n), F32),
        compiler_params=_params("parallel", "parallel"),
        name="modulation",
    )(cvecs, w_mod, b_mod.reshape(cfg.depth, 1, n))


def _in_proj_kernel(x_ref, mod_ref, ng_ref, w_ref, gains_ref, cos_ref, sa_ref, sb_ref,
                    kv_ref, main_ref, h_scr, hn_scr, raw0_scr, raw1_scr, *, cfg, kinds):
    i = pl.program_id(0)
    j = pl.program_id(1)
    tm = x_ref.shape[0]
    hd = cfg.head_dim
    tn = w_ref.shape[1]
    nj = len(kinds)
    raw = (raw0_scr, raw1_scr)

    def norm_chunk(rows, dst_ref):
        x = x_ref[rows, :]
        ms = jnp.mean(x * x, axis=-1, keepdims=True)
        y = x * lax.rsqrt(ms + NORM_EPS) * ng_ref[...]
        dst_ref[rows, :] = (y * (1.0 + mod_ref[1:2, :]) + mod_ref[0:1, :]).astype(BF16)

    def head_norm_rope(a, gain):
        ms = jnp.mean(a * a, axis=-1, keepdims=True)
        y = a * lax.rsqrt(ms + NORM_EPS) * gain
        return (y * cos_ref[...] + pltpu.roll(y, hd - hd // 4, 1) * sa_ref[...]
                + pltpu.roll(y, hd // 4, 1) * sb_ref[...])

    def epilogue(kind, acc_ref):
        if kind == "q":
            qs = hd ** -0.5 * LOG2E
            for h in range(tn // hd):
                sl = slice(h * hd, (h + 1) * hd)
                main_ref[:, sl] = (head_norm_rope(acc_ref[:, sl], gains_ref[1:2, :]) * qs).astype(BF16)
        elif kind == "kv":
            for h in range(tn // hd):
                sl = slice(h * hd, (h + 1) * hd)
                a = acc_ref[:, sl]
                kv_ref[:, sl] = head_norm_rope(a, gains_ref[0:1, :]) if h * hd < cfg.kv_width else a
        elif kind == "silu":
            main_ref[...] = _silu(acc_ref[...]).astype(BF16)
        else:
            main_ref[...] = acc_ref[...].astype(BF16)

    pr = tm // (nj - 1)
    rc = min(pr, 16)

    def step(jj):
        prev_kind = kinds[jj - 1] if jj else None
        if prev_kind:
            epilogue(prev_kind, raw[(jj + 1) % 2])
        if jj == nj:
            h_scr[...] = hn_scr[...]
            return
        if jj == 0:
            @pl.when(i == 0)
            def _():
                def body(r, carry):
                    norm_chunk(pl.ds(pl.multiple_of(r * rc, rc), rc), h_scr)
                    return carry

                lax.fori_loop(0, tm // rc, body, 0, unroll=min(4, tm // rc))

        raw[jj % 2][...] = jnp.dot(h_scr[...], w_ref[...], preferred_element_type=F32)
        if jj:
            for r in range(pr // rc):
                norm_chunk(pl.ds(pl.multiple_of((j - 1) * pr + r * rc, rc), rc), hn_scr)

    groups = {}
    for jj in range(nj + 1):
        key = (kinds[jj - 1] if jj else None, jj % 2, jj == nj)
        groups.setdefault(key, []).append(jj)
    for js in groups.values():
        pred = functools.reduce(lambda a, b: a | b, [j == v for v in js])
        pl.when(pred)(functools.partial(step, js[0]))


def _tile_kinds(cfg):
    tn = cfg.tn_in
    widths = [("q", cfg.attn_width), ("kv", 2 * cfg.kv_width), ("silu", cfg.attn_width),
              ("plain", cfg.ssm_width), ("silu", cfg.ssm_width), ("plain", cfg.fft_width),
              ("silu", cfg.fft_width)]
    kinds = []
    for name, w in widths:
        assert w % tn == 0
        kinds += [name] * (w // tn)
    return tuple(kinds)


def _in_proj(x, mod, norm_g, w_in_b, gains, rope_tabs, layer, cfg):
    tm, tn, d = cfg.tm_in, cfg.tn_in, cfg.d_model
    kinds = _tile_kinds(cfg)
    nj = len(kinds)
    kvj = kinds.index("kv")
    npb = cfg.p_tokens // tm
    per = cfg.dec_seq // tm
    cos_t, sa_t, sb_t = rope_tabs
    nb = cfg.tokens // tm
    assert tm % (nj - 1) == 0
    tab_spec = pl.BlockSpec((tm, cfg.head_dim),
                            lambda i, j: (jnp.where(i < npb, 0, 1 + (i - npb) % per), 0))

    def x_blk(i, j):
        return jnp.minimum(i + jnp.minimum(j, 1), nb - 1)

    def main_blk(i, j):
        t = jnp.maximum(j - 1, 0)
        return (i, jnp.where(t < kvj, t, jnp.maximum(t - 1, kvj - 1)))

    return pl.pallas_call(
        functools.partial(_in_proj_kernel, cfg=cfg, kinds=kinds),
        grid=(nb, nj + 1),
        in_specs=[
            pl.BlockSpec((tm, d), lambda i, j: (x_blk(i, j), 0)),
            pl.BlockSpec((None, None, 3, d),
                         lambda i, j: (layer, _row_group(x_blk(i, j), cfg, tm), 0, 0)),
            pl.BlockSpec((None, 1, d), lambda i, j: (layer, 0, 0)),
            pl.BlockSpec((None, None, d, tn), lambda i, j: (layer, jnp.minimum(j, nj - 1), 0, 0)),
            pl.BlockSpec((None, 2, cfg.head_dim), lambda i, j: (layer, 0, 0)),
            tab_spec, tab_spec, tab_spec,
        ],
        out_specs=[
            pl.BlockSpec((tm, tn), lambda i, j: (i, 0)),
            pl.BlockSpec((tm, tn), main_blk),
        ],
        out_shape=[
            jax.ShapeDtypeStruct((cfg.tokens, tn), F32),
            jax.ShapeDtypeStruct((cfg.tokens, cfg.main_width), BF16),
        ],
        scratch_shapes=[pltpu.VMEM((tm, d), BF16), pltpu.VMEM((tm, d), BF16),
                        pltpu.VMEM((tm, tn), F32), pltpu.VMEM((tm, tn), F32)],
        compiler_params=_params("arbitrary", "arbitrary"),
        name="in_proj",
    )(x, mod, norm_g, w_in_b, gains, cos_t, sa_t, sb_t)


def _softmax_attend(q, k, v1):
    hd = q.shape[1]
    s = lax.dot_general(q, k, (((1,), (1,)), ((), ())), preferred_element_type=F32)
    m = jnp.max(s, axis=-1, keepdims=True)
    p = jnp.exp2(s - m).astype(BF16)
    o = jnp.dot(p, v1, preferred_element_type=F32)
    return o[:, :hd] / o[:, hd:]


def _attn_ctx_kernel(prev_ref, q_ref, kv_ref, g_ref, o_ref, *, cfg):
    del prev_ref
    hd, kvw = cfg.head_dim, cfg.kv_width
    for kh in range(cfg.n_kv):
        k = kv_ref[:, kh * hd:(kh + 1) * hd].astype(BF16)
        v = kv_ref[:, kvw + kh * hd:kvw + (kh + 1) * hd].astype(BF16)
        v1 = jnp.concatenate([v, jnp.ones_like(v)], axis=1)
        for h in range(kh * cfg.q_per_kv, (kh + 1) * cfg.q_per_kv):
            sl = slice(h * hd, (h + 1) * hd)
            o = _softmax_attend(q_ref[:, sl], k, v1)
            o_ref[:, sl] = (o * g_ref[:, sl].astype(F32)).astype(BF16)


def _attn_context(attn_buf, main, kv, cfg):
    aw = cfg.attn_width
    return pl.pallas_call(
        functools.partial(_attn_ctx_kernel, cfg=cfg),
        grid=(cfg.batch,),
        in_specs=[
            pl.BlockSpec(memory_space=pl.ANY),
            pl.BlockSpec((cfg.seq, aw), lambda b: (b, 0)),
            pl.BlockSpec((cfg.seq, 2 * cfg.kv_width), lambda b: (b, 0)),
            pl.BlockSpec((cfg.seq, aw), lambda b: (b, 1)),
        ],
        out_specs=pl.BlockSpec((cfg.seq, aw), lambda b: (b, 0)),
        out_shape=jax.ShapeDtypeStruct((cfg.tokens, aw), BF16),
        input_output_aliases={0: 0},
        compiler_params=_params("parallel"),
        name="attn_context",
    )(attn_buf, main, kv, main)


def _attn_lat_kernel(prev_ref, q_ref, ck_ref, cv_ref, kn_ref, vn_ref, g_ref, o_ref,
                     k_scr, v_scr, *, cfg):
    del prev_ref
    hd, past = cfg.head_dim, cfg.past_len

    @pl.when(pl.program_id(2) == 0)
    def _():
        k_scr[0:past, :] = ck_ref[...].astype(BF16)
        k_scr[past:, :] = kn_ref[...].astype(BF16)
        v_scr[0:past, 0:hd] = cv_ref[...].astype(BF16)
        v_scr[past:, 0:hd] = vn_ref[...].astype(BF16)
        v_scr[:, hd:] = jnp.ones((v_scr.shape[0], hd), BF16)

    k = k_scr[...]
    v1 = v_scr[...]
    rq = min(cfg.rq, q_ref.shape[0])
    for h in range(cfg.q_per_kv):
        sl = slice(h * hd, (h + 1) * hd)
        for r in range(q_ref.shape[0] // rq):
            rows = slice(r * rq, (r + 1) * rq)
            o = _softmax_attend(q_ref[rows, sl], k, v1)
            o_ref[rows, sl] = (o * g_ref[rows, sl].astype(F32)).astype(BF16)


def _attn_latent(attn_prev, main, kv, cache_k, cache_v, layer, cfg):
    hd, nkv, tq = cfg.head_dim, cfg.n_kv, cfg.tq
    qw = cfg.q_per_kv * hd
    nqb = cfg.dec_seq // tq
    row0 = cfg.p_tokens // tq
    kvrow0 = cfg.p_tokens // cfg.dec_seq
    nkeys = cfg.past_len + cfg.dec_seq

    def q_map(b, h, qi):
        return (row0 + b * nqb + qi, h)

    def g_map(b, h, qi):
        return (row0 + b * nqb + qi, nkv + h)

    cache_spec = pl.BlockSpec((None, None, cfg.past_len, hd), lambda b, h, qi: (b, layer, 0, h))
    return pl.pallas_call(
        functools.partial(_attn_lat_kernel, cfg=cfg),
        grid=(cfg.dec_batch, nkv, nqb),
        in_specs=[
            pl.BlockSpec(memory_space=pl.ANY),
            pl.BlockSpec((tq, qw), q_map),
            cache_spec, cache_spec,
            pl.BlockSpec((cfg.dec_seq, hd), lambda b, h, qi: (kvrow0 + b, h)),
            pl.BlockSpec((cfg.dec_seq, hd), lambda b, h, qi: (kvrow0 + b, nkv + h)),
            pl.BlockSpec((tq, qw), g_map),
        ],
        out_specs=pl.BlockSpec((tq, qw), q_map),
        out_shape=jax.ShapeDtypeStruct((cfg.tokens, cfg.attn_width), BF16),
        scratch_shapes=[pltpu.VMEM((nkeys, hd), BF16), pltpu.VMEM((nkeys, 2 * hd), BF16)],
        input_output_aliases={0: 0},
        compiler_params=_params("parallel", "parallel", "arbitrary"),
        name="attn_latent",
    )(attn_prev, main, cache_k, cache_v, kv, kv, main)


def _ssm_kernel(u_ref, perm_ref, catf_ref, catb_ref, bbt_ref, pa_ref, w2_ref, av_ref,
                pw_ref, dsk_ref, h0_ref, y_ref, fin_ref,
                uf_scr, xcat_scr, u8_scr, m_scr, w1s_scr, s_scr, hin_scr, f_scr, g_scr, *, cfg):
    T, C = SSM_CHUNK, cfg.ssm_group
    tc = T * C
    st = 2 * cfg.ssm_state
    n_k = cfg.n_k
    chain = T * n_k
    nc = cfg.p_tokens // chain
    n_sc = cfg.dec_seq // chain
    gt = LANES // C
    jl_n = LANES // C
    jh_n = T // jl_n
    is_latent = pl.program_id(1) == 1
    y8_scr = xcat_scr
    pitch = chain + SSM_ROW_PAD

    def stage_in(c, carry):
        src = pl.ds(pl.multiple_of(c * chain, chain), chain)
        uf_scr[pl.ds(pl.multiple_of(c * pitch, 8), chain), :] = u_ref[src, :].astype(F32)
        return carry

    lax.fori_loop(0, nc, stage_in, 0)

    def gather(k, carry):
        for j in range(T):
            piece = uf_scr[pl.ds(k * T + j, nc, stride=pitch), :].astype(BF16)
            xcat_scr[j // jl_n, pl.ds(pl.multiple_of(k * nc, nc), nc),
                     (j % jl_n) * LANES:(j % jl_n + 1) * LANES] = piece
        return carry

    lax.fori_loop(0, n_k, gather, 0)
    for jh in range(jh_n):
        t = jnp.dot(xcat_scr[jh], perm_ref[...], preferred_element_type=F32)
        for r in range(gt):
            u8_scr[r, :, jh * LANES:(jh + 1) * LANES] = t[:, r * LANES:(r + 1) * LANES].astype(BF16)

    def cmul(h, hs, c1, c2):
        return h * c1 + hs * c2, hs * c1 - h * c2

    lane = lax.broadcasted_iota(jnp.int32, (C, tc), 1)

    for r in range(gt):
        ktf = jnp.dot(bbt_ref[r, 0], catf_ref[r], preferred_element_type=F32, precision=HIGHEST)
        ktb = jnp.dot(bbt_ref[r, 1], catb_ref[r], preferred_element_type=F32, precision=HIGHEST)
        for j in range(T):
            rf = pltpu.roll(ktf, C * j, 1) if j else ktf
            rf = jnp.where(lane >= C * j, rf, 0.0)
            sh = (T - 1 - j) * C
            rb = pltpu.roll(ktb, tc - sh, 1) if sh else ktb
            rb = jnp.where(lane < tc - sh, rb, 0.0)
            m_scr[j * C:(j + 1) * C, :] = (rf + rb).astype(BF16)
        for d in range(2):
            bb = bbt_ref[r, d]
            bbs = pltpu.roll(bb, st // 2, 1)
            for j in range(T):
                w, ws = cmul(bb, bbs, pa_ref[r, d, j, 0:1, :], pa_ref[r, d, j, 1:2, :])
                w1s_scr[j * C:(j + 1) * C, 2 * d * st:(2 * d + 1) * st] = w.astype(BF16)
                w1s_scr[j * C:(j + 1) * C, (2 * d + 1) * st:(2 * d + 2) * st] = ws.astype(BF16)

        u = u8_scr[r]
        y_intra = jnp.dot(u, m_scr[...], preferred_element_type=F32)
        s_scr[...] = jnp.dot(u, w1s_scr[...], preferred_element_type=F32)
        av = av_ref[r]
        a1 = (av[0:1], av[2:3])
        a2 = (av[1:2], av[3:4])
        b1 = (av[4:5], av[6:7])
        b2 = (av[5:6], av[7:8])

        def level1(k, carry):
            hf, hfs, hb, hbs = carry
            rf_ = pl.ds(pl.multiple_of(k * nc, nc), nc)
            rb_ = pl.ds(pl.multiple_of((n_k - 1 - k) * nc, nc), nc)
            hin_scr[rf_, 0:st] = hf
            hin_scr[rf_, st:2 * st] = hfs
            hin_scr[rb_, 2 * st:3 * st] = hb
            hin_scr[rb_, 3 * st:4 * st] = hbs
            nf, nfs = cmul(hf, hfs, a1[0], a2[0])
            nbk, nbs = cmul(hb, hbs, a1[1], a2[1])
            return (nf + s_scr[rf_, 0:st], nfs + s_scr[rf_, st:2 * st],
                    nbk + s_scr[rb_, 2 * st:3 * st], nbs + s_scr[rb_, 3 * st:4 * st])

        z = jnp.zeros((nc, st), F32)
        ff, ffs, fb, fbs = lax.fori_loop(0, n_k, level1, (z, z, z, z))

        @pl.when(jnp.logical_not(is_latent))
        def _():
            fin_ref[r, 0] = ff
            fin_ref[r, 1] = fb

        @pl.when(is_latent)
        def _():
            for idx, val in enumerate((ff, ffs, fb, fbs)):
                f_scr[idx] = val
            h0 = h0_ref[r]
            gf, gfs, gb, gbs = h0[0], h0[1], h0[2], h0[3]
            for sc in range(n_sc):
                sel = pl.ds(sc, cfg.dec_batch, stride=n_sc)
                g_scr[0, sel, :] = gf
                g_scr[1, sel, :] = gfs
                nf, nfs = cmul(gf, gfs, b1[0], b2[0])
                gf = nf + f_scr[0, sel, :]
                gfs = nfs + f_scr[1, sel, :]
            for sc in range(n_sc - 1, -1, -1):
                sel = pl.ds(sc, cfg.dec_batch, stride=n_sc)
                g_scr[2, sel, :] = gb
                g_scr[3, sel, :] = gbs
                nbk, nbs = cmul(gb, gbs, b1[1], b2[1])
                gb = nbk + f_scr[2, sel, :]
                gbs = nbs + f_scr[3, sel, :]
            gfa, gfsa, gba, gbsa = g_scr[0], g_scr[1], g_scr[2], g_scr[3]
            for k in range(n_k):
                rows_k = slice(k * nc, (k + 1) * nc)
                pf = pw_ref[r, k]
                cf, cfs = cmul(gfa, gfsa, pf[0:1], pf[1:2])
                cb, cbs = cmul(gba, gbsa, pf[2:3], pf[3:4])
                hin_scr[rows_k, 0:st] += cf
                hin_scr[rows_k, st:2 * st] += cfs
                hin_scr[rows_k, 2 * st:3 * st] += cb
                hin_scr[rows_k, 3 * st:4 * st] += cbs

        hsel = jnp.concatenate([hin_scr[:, 0:st], hin_scr[:, 2 * st:3 * st]], axis=1).astype(BF16)
        y = y_intra + jnp.dot(hsel, w2_ref[r], preferred_element_type=F32)
        y = (y + dsk_ref[r] * u.astype(F32)).astype(BF16)
        for jh in range(jh_n):
            y8_scr[jh, :, r * LANES:(r + 1) * LANES] = y[:, jh * LANES:(jh + 1) * LANES]

    for jh in range(jh_n):
        t = lax.dot_general(y8_scr[jh], perm_ref[...], (((1,), (1,)), ((), ())),
                            preferred_element_type=F32)
        for k in range(n_k):
            for jl in range(jl_n):
                uf_scr[pl.ds(k * T + jh * jl_n + jl, nc, stride=pitch), :] = (
                    t[k * nc:(k + 1) * nc, jl * LANES:(jl + 1) * LANES])

    def stage_out(c, carry):
        dst = pl.ds(pl.multiple_of(c * chain, chain), chain)
        y_ref[dst, :] = uf_scr[pl.ds(pl.multiple_of(c * pitch, 8), chain), :].astype(BF16)
        return carry

    lax.fori_loop(0, nc, stage_out, 0)


def _ssm_scan(main, ssm_w, h0, layer, cfg):
    perm, catf, catb, bbt, pa, w2, av, pw, dsk = ssm_w
    T, C = SSM_CHUNK, cfg.ssm_group
    tc = T * C
    st = 2 * cfg.ssm_state
    gt = LANES // C
    nq = cfg.ssm_width // LANES
    chain = T * cfg.n_k
    half = cfg.p_tokens
    assert cfg.seq == chain and cfg.dec_seq % chain == 0 and T % gt == 0 and cfg.s_tokens == half
    nc = half // chain
    rows = nc * cfg.n_k
    pw_n = gt * LANES
    ucol = 2 * cfg.attn_width // LANES

    def wspec(*shape):
        nd = len(shape)
        return pl.BlockSpec((None, gt) + shape, lambda q, s: (layer, q) + (0,) * nd)

    return pl.pallas_call(
        functools.partial(_ssm_kernel, cfg=cfg),
        grid=(nq, 2),
        in_specs=[
            pl.BlockSpec((half, LANES), lambda q, s: (s, ucol + q)),
            pl.BlockSpec((pw_n, pw_n), lambda q, s: (0, 0)),
            wspec(st, tc), wspec(st, tc), wspec(2, C, st), wspec(2, T, 2, st), wspec(2 * st, tc),
            wspec(8, st), wspec(cfg.n_k, 4, st), wspec(1, tc), wspec(4, cfg.dec_batch, st),
        ],
        out_specs=[
            pl.BlockSpec((half, LANES), lambda q, s: (s, q)),
            pl.BlockSpec((gt, 2, cfg.batch, st), lambda q, s: (q, 0, 0, 0)),
        ],
        out_shape=[
            jax.ShapeDtypeStruct((cfg.tokens, cfg.ssm_width), BF16),
            jax.ShapeDtypeStruct((cfg.n_groups, 2, cfg.batch, st), F32),
        ],
        scratch_shapes=[
            pltpu.VMEM((nc * (chain + SSM_ROW_PAD), LANES), F32),
            pltpu.VMEM((T // gt, rows, pw_n), BF16),
            pltpu.VMEM((gt, rows, tc), BF16),
            pltpu.VMEM((tc, tc), BF16),
            pltpu.VMEM((tc, 4 * st), BF16),
            pltpu.VMEM((rows, 4 * st), F32),
            pltpu.VMEM((rows, 4 * st), F32),
            pltpu.VMEM((4, nc, st), F32),
            pltpu.VMEM((4, nc, st), F32),
        ],
        compiler_params=_params("parallel", "arbitrary"),
        name="ssm_scan",
    )(main, perm, catf, catb, bbt, pa, w2, av, pw, dsk, h0)


def _ssm_weights(lam_re, lam_im, log_step, b_re, b_im, c_re, c_im, d_skip, cfg):
    T, C, P, G, nk = SSM_CHUNK, cfg.ssm_group, cfg.ssm_state, cfg.n_groups, cfg.n_k
    lr, li = lam_re.astype(F32), lam_im.astype(F32)
    dt = jnp.exp(log_step.astype(F32))[..., None]

    def powers(js):
        jj = jnp.asarray(js, F32)[:, None]
        mag = jnp.exp(lr[..., None, :] * dt[..., None, :] * jj)
        ang = li[..., None, :] * dt[..., None, :] * jj
        return mag * jnp.cos(ang), mag * jnp.sin(ang)

    pr, pi = powers(range(T + 1))
    ab_re, ab_im = pr[..., 1, :], pi[..., 1, :]
    nr, ni = ab_re - 1.0, ab_im
    den = lr * lr + li * li
    f_re = (nr * lr + ni * li) / den
    f_im = (ni * lr - nr * li) / den
    br, bi = b_re.astype(F32), b_im.astype(F32)
    bb_re = f_re[..., None] * br - f_im[..., None] * bi
    bb_im = f_re[..., None] * bi + f_im[..., None] * br
    cr, ci = c_re.astype(F32), c_im.astype(F32)
    ca_re = cr[..., None, :, :] * pr[..., :, None, :] - ci[..., None, :, :] * pi[..., :, None, :]
    ca_im = cr[..., None, :, :] * pi[..., :, None, :] + ci[..., None, :, :] * pr[..., :, None, :]

    def state_out(d, lo, rev):
        def pick(x):
            x = x[:, d, :, lo:lo + T]
            x = jnp.flip(x, axis=2) if rev else x
            return x.transpose(0, 1, 4, 2, 3).reshape(-1, G, P, T * C)
        return jnp.concatenate([pick(ca_re), -pick(ca_im)], axis=2)

    catf = state_out(0, 0, False)
    catb = state_out(1, 0, True)
    w2 = jnp.concatenate([state_out(0, 1, False), state_out(1, 1, True)], axis=2).astype(BF16)
    bbt = jnp.concatenate([bb_re, bb_im], axis=3).transpose(0, 2, 1, 4, 3)

    def lanes(re, im):
        return jnp.concatenate([re, re], axis=-1), jnp.concatenate([-im, im], axis=-1)

    paf = lanes(jnp.flip(pr[:, 0, :, 0:T], axis=2), jnp.flip(pi[:, 0, :, 0:T], axis=2))
    pab = lanes(pr[:, 1, :, 0:T], pi[:, 1, :, 0:T])
    pa = jnp.stack([jnp.stack(paf, axis=3), jnp.stack(pab, axis=3)], axis=2)

    qr, qi = powers([T * k for k in range(nk + 1)])
    rows = []
    for src_r, src_i, idx in ((pr, pi, T), (qr, qi, nk)):
        for d in range(2):
            rows.extend(lanes(src_r[:, d, :, idx], src_i[:, d, :, idx]))
    av = jnp.stack(rows, axis=2)
    p1f, p2f = lanes(qr[:, 0, :, 0:nk], qi[:, 0, :, 0:nk])
    p1b, p2b = lanes(jnp.flip(qr[:, 1, :, 0:nk], axis=2), jnp.flip(qi[:, 1, :, 0:nk], axis=2))
    pw = jnp.stack([p1f, p2f, p1b, p2b], axis=3)
    dsk = jnp.tile(d_skip.astype(F32).reshape(-1, G, 1, C), (1, 1, 1, T))

    gt = LANES // C
    n = gt * LANES
    src = jnp.arange(n)
    jl, r, c = src // LANES, (src % LANES) // C, src % C
    dst = r * LANES + jl * C + c
    perm = (dst[:, None] == jnp.arange(n)[None, :]).astype(BF16)
    return perm, catf, catb, bbt, pa, w2, av, pw, dsk


def _glu_kernel(y_ref, w_ref, g_ref, o_ref, *, cfg):
    z = jnp.dot(y_ref[...], w_ref[...], preferred_element_type=F32)
    sw = cfg.ssm_width
    a, g = z[:, :sw], z[:, sw:]
    o_ref[...] = (a / (1.0 + jnp.exp(-g)) * g_ref[...].astype(F32)).astype(BF16)


def _ssm_glu(y, w_glu_b, main, layer, cfg):
    tm, sw = cfg.tm_post, cfg.ssm_width
    gcol = (2 * cfg.attn_width + sw) // sw
    assert (2 * cfg.attn_width) % sw == 0
    return pl.pallas_call(
        functools.partial(_glu_kernel, cfg=cfg),
        grid=(cfg.tokens // tm,),
        in_specs=[
            pl.BlockSpec((tm, sw), lambda i: (i, 0)),
            pl.BlockSpec((None, sw, 2 * sw), lambda i: (layer, 0, 0)),
            pl.BlockSpec((tm, sw), lambda i: (i, gcol)),
        ],
        out_specs=pl.BlockSpec((tm, sw), lambda i: (i, 0)),
        out_shape=jax.ShapeDtypeStruct((cfg.tokens, sw), BF16),
        compiler_params=_params("parallel"),
        name="ssm_glu",
    )(y, w_glu_b, main)


def _fourier_kernel(prev_ref, x_ref, dl_ref, cs_ref, w_ref, g_ref, o_ref, z_scr, *, cfg, length):
    del prev_ref
    fg = cfg.fft_group
    rc = min(cfg.rc_fft, length)

    @pl.when(pl.program_id(1) == 0)
    def _():
        def body(ci, carry):
            rows = pl.ds(pl.multiple_of(ci * rc, rc), rc)
            rows2 = pl.ds(pl.multiple_of(length + ci * rc, rc), rc)
            for g in range(cfg.fft_width // fg):
                cols = slice(g * fg, (g + 1) * fg)
                t = jnp.dot(x_ref[rows, cols], cs_ref[...], preferred_element_type=F32)
                z_scr[rows, cols] = t[:, :fg].astype(BF16)
                z_scr[rows2, cols] = t[:, fg:].astype(BF16)
            return carry

        lax.fori_loop(0, length // rc, body, 0)

    mixed = jnp.dot(dl_ref[...], z_scr[...], preferred_element_type=F32).astype(BF16)
    four = jnp.dot(mixed, w_ref[...], preferred_element_type=F32)
    o_ref[...] = (four * g_ref[...].astype(F32)).astype(BF16)


def _fourier(prev, main, dl, cs, w_fft_b, layer, cfg, *, length, nbatch, row0):
    fw = cfg.fft_width
    tm = min(cfg.tm_fft, length)
    nr = length // tm
    xcol = (2 * cfg.attn_width + 2 * cfg.ssm_width) // fw
    assert (2 * cfg.attn_width + 2 * cfg.ssm_width) % fw == 0 and row0 % length == 0
    b0 = row0 // length
    t0 = row0 // tm

    def tile_map(b, r):
        return (t0 + b * nr + r, 0)

    return pl.pallas_call(
        functools.partial(_fourier_kernel, cfg=cfg, length=length),
        grid=(nbatch, nr),
        in_specs=[
            pl.BlockSpec(memory_space=pl.ANY),
            pl.BlockSpec((length, fw), lambda b, r: (b0 + b, xcol)),
            pl.BlockSpec((tm, 2 * length), lambda b, r: (r, 0)),
            pl.BlockSpec((cfg.fft_group, 2 * cfg.fft_group), lambda b, r: (0, 0)),
            pl.BlockSpec((None, fw, fw), lambda b, r: (layer, 0, 0)),
            pl.BlockSpec((tm, fw), lambda b, r: (t0 + b * nr + r, xcol + 1)),
        ],
        out_specs=pl.BlockSpec((tm, fw), tile_map),
        out_shape=jax.ShapeDtypeStruct((cfg.tokens, fw), BF16),
        scratch_shapes=[pltpu.VMEM((2 * length, fw), BF16)],
        input_output_aliases={0: 0},
        compiler_params=_params("parallel", "arbitrary"),
        name=f"fourier_{length}",
    )(prev, main, dl, cs, w_fft_b, main)


def _dft_cos_sin(n):
    k = jnp.arange(n, dtype=jnp.int32)
    w = 2.0 * math.pi / n

    ang = ((k[:, None] * k[None, :]) % n).astype(F32) * w
    scale = n ** -0.5
    return jnp.cos(ang) * scale, jnp.sin(ang) * scale


def _dft_position_table(n):
    s = 1
    while s * s < n:
        s *= 2
    if n <= 512 or n % s:
        c, sn = _dft_cos_sin(n)
        return jnp.concatenate([c, -sn], axis=1).astype(BF16)
    k = jnp.arange(n, dtype=jnp.int32)
    w = 2.0 * math.pi / n

    def cs(j):
        ang = ((j[:, None] * k[None, :]) % n).astype(F32) * w
        return jnp.cos(ang), jnp.sin(ang)

    c1, s1 = cs(jnp.arange(n // s, dtype=jnp.int32) * s)
    c0, s0 = cs(jnp.arange(s, dtype=jnp.int32))
    a1 = jnp.concatenate([c1, -s1], axis=1)[:, None, :]
    a2 = jnp.concatenate([s1, c1], axis=1)[:, None, :]
    b0 = jnp.concatenate([c0, c0], axis=1)[None]
    b1 = jnp.concatenate([s0, s0], axis=1)[None]
    return ((a1 * b0 - a2 * b1) * n ** -0.5).reshape(n, 2 * n).astype(BF16)


def _out_proj_kernel(a_ref, s_ref, f_ref, w_ref, x_ref, mod_ref, o_ref, *, cfg):
    aw, sw = cfg.attn_width, cfg.ssm_width
    acc = jnp.dot(a_ref[...], w_ref[0:aw, :], preferred_element_type=F32)
    acc += jnp.dot(s_ref[...], w_ref[aw:aw + sw, :], preferred_element_type=F32)
    acc += jnp.dot(f_ref[...], w_ref[aw + sw:, :], preferred_element_type=F32)
    o_ref[...] = x_ref[...] + mod_ref[2:3, :] * acc


def _out_proj(attn, ssm, four, w_out_b, x, mod, layer, cfg):
    tm, tn, d = cfg.tm_out, cfg.tn_out, cfg.d_model
    return pl.pallas_call(
        functools.partial(_out_proj_kernel, cfg=cfg),
        grid=(d // tn, cfg.tokens // tm),
        in_specs=[
            pl.BlockSpec((tm, cfg.attn_width), lambda j, i: (i, 0)),
            pl.BlockSpec((tm, cfg.ssm_width), lambda j, i: (i, 0)),
            pl.BlockSpec((tm, cfg.fft_width), lambda j, i: (i, 0)),
            pl.BlockSpec((None, d, tn), lambda j, i: (layer, 0, j)),
            pl.BlockSpec((tm, tn), lambda j, i: (i, j)),
            pl.BlockSpec((None, None, 3, tn), lambda j, i: (layer, _row_group(i, cfg, tm), 0, j)),
        ],
        out_specs=pl.BlockSpec((tm, tn), lambda j, i: (i, j)),
        out_shape=jax.ShapeDtypeStruct((cfg.tokens, d), F32),
        compiler_params=_params("parallel", "parallel"),
        name="out_proj",
    )(attn, ssm, four, w_out_b, x, mod)


def _final_norm_kernel(x_ref, g_ref, o_ref):
    x = x_ref[...]
    ms = jnp.mean(x * x, axis=-1, keepdims=True)
    o_ref[...] = x * lax.rsqrt(ms + NORM_EPS) * g_ref[...]


def _final_norm(x, g, cfg, *, row0, nrows):
    tm, d = 256, cfg.d_model
    tm = min(tm, nrows)
    assert row0 % tm == 0
    return pl.pallas_call(
        _final_norm_kernel,
        grid=(nrows // tm,),
        in_specs=[pl.BlockSpec((tm, d), lambda i: (row0 // tm + i, 0)),
                  pl.BlockSpec((1, d), lambda i: (0, 0))],
        out_specs=pl.BlockSpec((tm, d), lambda i: (i, 0)),
        out_shape=jax.ShapeDtypeStruct((nrows, d), F32),
        compiler_params=_params("parallel"),
        name="final_norm",
    )(x, g)


def _rope_tables(cfg):
    hd = cfg.head_dim
    pairs = hd // 4
    t = jnp.arange(cfg.dec_seq)
    inv = ROPE_THETA ** (-jnp.arange(pairs, dtype=F32) / pairs)
    row_ang = (t // cfg.grid_w).astype(F32)[:, None] * inv[None, :]
    col_ang = (t % cfg.grid_w).astype(F32)[:, None] * inv[None, :]
    zeros = jnp.zeros_like(row_ang)
    cos = jnp.concatenate([jnp.cos(row_ang)] * 2 + [jnp.cos(col_ang)] * 2, axis=1)
    sa = jnp.concatenate([-jnp.sin(row_ang), zeros, -jnp.sin(col_ang), zeros], axis=1)
    sb = jnp.concatenate([zeros, jnp.sin(row_ang), zeros, jnp.sin(col_ang)], axis=1)
    ident = jnp.zeros((cfg.tm_in, hd), F32)
    return (jnp.concatenate([ident + 1.0, cos], axis=0),
            jnp.concatenate([ident, sa], axis=0),
            jnp.concatenate([ident, sb], axis=0))


def _forward(cfg, x_prompt, x_sample, cache_k, cache_v, state_fwd_re, state_fwd_im, state_bwd_re,
             state_bwd_im, c, c_ctx, norm_g, w_mod, b_mod, w_in, q_norm, k_norm, lam_re, lam_im,
             log_step, b_re, b_im, c_re, c_im, d_skip, w_glu, w_fft, w_out, final_norm_g):
    d, kvw = cfg.d_model, cfg.kv_width
    P = cfg.ssm_state

    w_in_b = w_in.astype(BF16).reshape(cfg.depth, d, -1, cfg.tn_in).transpose(0, 2, 1, 3)
    w_out_b = w_out.astype(BF16)
    w_glu_b = w_glu.astype(BF16)
    w_fft_b = w_fft.astype(BF16)
    gains = jnp.stack([k_norm, q_norm], axis=1).astype(F32)
    norm_g3 = norm_g.astype(F32)[:, None, :]
    ssm_w = _ssm_weights(lam_re, lam_im, log_step, b_re, b_im, c_re, c_im, d_skip, cfg)
    rope_tabs = _rope_tables(cfg)

    cc, sc = _dft_cos_sin(cfg.fft_group)
    cs = jnp.concatenate([cc, sc], axis=1).astype(BF16)
    dls = {}
    for length in {cfg.seq, cfg.dec_seq}:
        dls[length] = _dft_position_table(length)

    def st(re, im):
        re = re.astype(F32).transpose(1, 2, 0, 3)
        im = im.astype(F32).transpose(1, 2, 0, 3)
        return jnp.concatenate([re, im], axis=-1), jnp.concatenate([im, re], axis=-1)

    h0 = jnp.stack(st(state_fwd_re, state_fwd_im) + st(state_bwd_re, state_bwd_im), axis=2)

    nrow = 1 + cfg.dec_batch
    cvecs = jnp.concatenate([c_ctx[None, :], c], axis=0).astype(F32)
    cvecs = jnp.pad(cvecs, ((0, -nrow % 8), (0, 0)))
    mod = _modulation(cvecs, w_mod, b_mod, cfg).reshape(cfg.depth, cvecs.shape[0], 3, d)

    cache_k4 = cache_k.reshape(cfg.dec_batch, cfg.depth, cfg.past_len, kvw)
    cache_v4 = cache_v.reshape(cfg.dec_batch, cfg.depth, cfg.past_len, kvw)

    x = jnp.concatenate([x_prompt.reshape(cfg.p_tokens, d), x_sample.reshape(cfg.s_tokens, d)], axis=0)
    attn = jnp.zeros((cfg.tokens, cfg.attn_width), BF16)
    four = jnp.zeros((cfg.tokens, cfg.fft_width), BF16)
    ks, vs, fins = [], [], []
    for l in range(cfg.depth):
        kv, main = _in_proj(x, mod, norm_g3, w_in_b, gains, rope_tabs, l, cfg)
        attn = _attn_context(attn, main, kv, cfg)
        attn = _attn_latent(attn, main, kv, cache_k4, cache_v4, l, cfg)
        y, fin = _ssm_scan(main, ssm_w, h0, l, cfg)
        ssm = _ssm_glu(y, w_glu_b, main, l, cfg)
        four = _fourier(four, main, dls[cfg.seq], cs, w_fft_b, l, cfg,
                        length=cfg.seq, nbatch=cfg.batch, row0=0)
        four = _fourier(four, main, dls[cfg.dec_seq], cs, w_fft_b, l, cfg,
                        length=cfg.dec_seq, nbatch=cfg.dec_batch, row0=cfg.p_tokens)
        x = _out_proj(attn, ssm, four, w_out_b, x, mod, l, cfg)
        ks.append(kv[:cfg.p_tokens, :kvw].reshape(cfg.batch, cfg.seq, cfg.n_kv, cfg.head_dim))
        vs.append(kv[:cfg.p_tokens, kvw:].reshape(cfg.batch, cfg.seq, cfg.n_kv, cfg.head_dim))
        fins.append(fin)

    g = final_norm_g[None, :].astype(F32)
    y_prompt = _final_norm(x, g, cfg, row0=0, nrows=cfg.p_tokens).reshape(cfg.batch, cfg.seq, d)
    y_sample = _final_norm(x, g, cfg, row0=cfg.p_tokens, nrows=cfg.s_tokens).reshape(
        cfg.dec_batch, cfg.dec_seq, d)
    fin = jnp.stack(fins, axis=0)
    fin = fin.transpose(3, 0, 1, 2, 4)
    return (y_prompt, y_sample, jnp.stack(ks, axis=1), jnp.stack(vs, axis=1),
            fin[:, :, :, 0, :P], fin[:, :, :, 0, P:], fin[:, :, :, 1, :P], fin[:, :, :, 1, P:])


def kernel(x_prompt, x_sample, cache_k, cache_v, state_fwd_re, state_fwd_im, state_bwd_re, state_bwd_im,
           c, c_ctx, norm_g, w_mod, b_mod, w_in, q_norm, k_norm, lam_re, lam_im, log_step,
           b_re, b_im, c_re, c_im, d_skip, w_glu, w_fft, w_out, final_norm_g):
    return _forward(Cfg(), x_prompt, x_sample, cache_k, cache_v, state_fwd_re, state_fwd_im,
                    state_bwd_re, state_bwd_im, c, c_ctx, norm_g, w_mod, b_mod, w_in, q_norm, k_norm,
                    lam_re, lam_im, log_step, b_re, b_im, c_re, c_im, d_skip, w_glu, w_fft, w_out,
                    final_norm_g)
```

```python
import functools
import math
from typing import NamedTuple

import jax
import jax.numpy as jnp
from jax import lax
from jax.experimental import pallas as pl
from jax.experimental.pallas import tpu as pltpu

F32 = jnp.float32
BF16 = jnp.bfloat16
NORM_EPS = 1e-6
ROPE_THETA = 10000.0
LANES = 128
SSM_CHUNK = 16
SSM_ROW_PAD = 8
VMEM_LIMIT = 56 * 1024 * 1024
W_IN_SPLIT = 4
LOG2E = 1.4426950408889634
HIGHEST = lax.Precision.HIGHEST


class Cfg(NamedTuple):
    d_model: int = 4096
    batch: int = 32
    seq: int = 256
    depth: int = 4
    dec_batch: int = 2
    dec_seq: int = 4096
    past_len: int = 512
    grid_w: int = 64
    head_dim: int = 128
    n_heads: int = 16
    n_kv: int = 4
    ssm_width: int = 1024
    ssm_group: int = 16
    ssm_state: int = 64
    fft_width: int = 1024
    fft_group: int = 256
    tm_in: int = 512
    tm_out: int = 512
    tn_out: int = 1024
    tn_mod: int = 512
    tq: int = 512
    rq: int = 128
    tm_fft: int = 256
    rc_fft: int = 512
    tm_post: int = 512
    n_k: int = 16

    @property
    def attn_width(self):
        return self.n_heads * self.head_dim

    @property
    def kv_width(self):
        return self.n_kv * self.head_dim

    @property
    def q_per_kv(self):
        return self.n_heads // self.n_kv

    @property
    def n_groups(self):
        return self.ssm_width // self.ssm_group

    @property
    def tn_in(self):
        return 2 * self.kv_width

    @property
    def main_width(self):
        return 2 * self.attn_width + 2 * self.ssm_width + 2 * self.fft_width

    @property
    def p_tokens(self):
        return self.batch * self.seq

    @property
    def s_tokens(self):
        return self.dec_batch * self.dec_seq

    @property
    def tokens(self):
        return self.p_tokens + self.s_tokens


def _params(*sem):
    return pltpu.CompilerParams(dimension_semantics=sem, vmem_limit_bytes=VMEM_LIMIT)


def _silu(x):
    return x / (1.0 + jnp.exp(-x))


def _row_group(i, cfg, tm):
    npb = cfg.p_tokens // tm
    per = cfg.dec_seq // tm
    return jnp.where(i < npb, 0, 1 + (i - npb) // per)


def _mod_kernel(c_ref, w_ref, b_ref, o_ref):
    s = _silu(c_ref[...]).astype(BF16)
    o_ref[0] = jnp.dot(s, w_ref[0].astype(BF16), preferred_element_type=F32) + b_ref[0]


def _modulation(cvecs, w_mod, b_mod, cfg):
    d, n = cfg.d_model, 3 * cfg.d_model
    nrow = cvecs.shape[0]
    return pl.pallas_call(
        _mod_kernel,
        grid=(cfg.depth, n // cfg.tn_mod),
        in_specs=[
            pl.BlockSpec((nrow, d), lambda l, j: (0, 0)),
            pl.BlockSpec((1, d, cfg.tn_mod), lambda l, j: (l, 0, j)),
            pl.BlockSpec((1, 1, cfg.tn_mod), lambda l, j: (l, 0, j)),
        ],
        out_specs=pl.BlockSpec((1, nrow, cfg.tn_mod), lambda l, j: (l, 0, j)),
        out_shape=jax.ShapeDtypeStruct((cfg.depth, nrow, n), F32),
        compiler_params=_params("parallel", "parallel"),
        name="modulation",
    )(cvecs, w_mod, b_mod.reshape(cfg.depth, 1, n))


def _in_proj_kernel(x_ref, mod_ref, ng_ref, *rest, cfg, kinds):
    w_refs = rest[:W_IN_SPLIT]
    (gains_ref, cos_ref, sa_ref, sb_ref,
     kv_ref, main_ref, h_scr, hn_scr, raw0_scr, raw1_scr) = rest[W_IN_SPLIT:]
    i = pl.program_id(0)
    j = pl.program_id(1)
    tm = x_ref.shape[0]
    hd = cfg.head_dim
    tn = w_refs[0].shape[1]
    kq = w_refs[0].shape[0]
    nj = len(kinds)
    raw = (raw0_scr, raw1_scr)

    def norm_chunk(rows, dst_ref):
        x = x_ref[rows, :]
        ms = jnp.mean(x * x, axis=-1, keepdims=True)
        y = x * lax.rsqrt(ms + NORM_EPS) * ng_ref[...]
        dst_ref[rows, :] = (y * (1.0 + mod_ref[1:2, :]) + mod_ref[0:1, :]).astype(BF16)

    def head_norm_rope(a, gain):
        ms = jnp.mean(a * a, axis=-1, keepdims=True)
        y = a * lax.rsqrt(ms + NORM_EPS) * gain
        return (y * cos_ref[...] + pltpu.roll(y, hd - hd // 4, 1) * sa_ref[...]
                + pltpu.roll(y, hd // 4, 1) * sb_ref[...])

    def epilogue(kind, acc_ref):
        if kind == "q":
            qs = hd ** -0.5 * LOG2E
            for h in range(tn // hd):
                sl = slice(h * hd, (h + 1) * hd)
                main_ref[:, sl] = (head_norm_rope(acc_ref[:, sl], gains_ref[1:2, :]) * qs).astype(BF16)
        elif kind == "kv":
            for h in range(tn // hd):
                sl = slice(h * hd, (h + 1) * hd)
                a = acc_ref[:, sl]
                kv_ref[:, sl] = head_norm_rope(a, gains_ref[0:1, :]) if h * hd < cfg.kv_width else a
        elif kind == "silu":
            main_ref[...] = _silu(acc_ref[...]).astype(BF16)
        else:
            main_ref[...] = acc_ref[...].astype(BF16)

    pr = tm // (nj - 1)
    rc = min(pr, 16)

    def step(jj):
        prev_kind = kinds[jj - 1] if jj else None
        if prev_kind:
            epilogue(prev_kind, raw[(jj + 1) % 2])
        if jj == nj:
            h_scr[...] = hn_scr[...]
            return
        if jj == 0:
            @pl.when(i == 0)
            def _():
                def body(r, carry):
                    norm_chunk(pl.ds(pl.multiple_of(r * rc, rc), rc), h_scr)
                    return carry

                lax.fori_loop(0, tm // rc, body, 0, unroll=min(4, tm // rc))

        acc = jnp.dot(h_scr[:, 0:kq], w_refs[0][...], preferred_element_type=F32)
        for q in range(1, W_IN_SPLIT):
            acc += jnp.dot(h_scr[:, q * kq:(q + 1) * kq], w_refs[q][...], preferred_element_type=F32)
        raw[jj % 2][...] = acc
        if jj:
            for r in range(pr // rc):
                norm_chunk(pl.ds(pl.multiple_of((j - 1) * pr + r * rc, rc), rc), hn_scr)

    groups = {}
    for jj in range(nj + 1):
        key = (kinds[jj - 1] if jj else None, jj % 2, jj == nj)
        groups.setdefault(key, []).append(jj)
    for js in groups.values():
        pred = functools.reduce(lambda a, b: a | b, [j == v for v in js])
        pl.when(pred)(functools.partial(step, js[0]))


def _tile_kinds(cfg):
    tn = cfg.tn_in
    widths = [("q", cfg.attn_width), ("kv", 2 * cfg.kv_width), ("silu", cfg.attn_width),
              ("plain", cfg.ssm_width), ("silu", cfg.ssm_width), ("plain", cfg.fft_width),
              ("silu", cfg.fft_width)]
    kinds = []
    for name, w in widths:
        assert w % tn == 0
        kinds += [name] * (w // tn)
    return tuple(kinds)


def _in_proj(x, mod, norm_g, w_in_b, gains, rope_tabs, layer, cfg):
    tm, tn, d = cfg.tm_in, cfg.tn_in, cfg.d_model
    kinds = _tile_kinds(cfg)
    nj = len(kinds)
    kvj = kinds.index("kv")
    npb = cfg.p_tokens // tm
    per = cfg.dec_seq // tm
    cos_t, sa_t, sb_t = rope_tabs
    nb = cfg.tokens // tm
    assert tm % (nj - 1) == 0
    tab_spec = pl.BlockSpec((tm, cfg.head_dim),
                            lambda i, j: (jnp.where(i < npb, 0, 1 + (i - npb) % per), 0))

    def x_blk(i, j):
        return jnp.minimum(i + jnp.minimum(j, 1), nb - 1)

    def main_blk(i, j):
        t = jnp.maximum(j - 1, 0)
        return (i, jnp.where(t < kvj, t, jnp.maximum(t - 1, kvj - 1)))

    return pl.pallas_call(
        functools.partial(_in_proj_kernel, cfg=cfg, kinds=kinds),
        grid=(nb, nj + 1),
        in_specs=[
            pl.BlockSpec((tm, d), lambda i, j: (x_blk(i, j), 0)),
            pl.BlockSpec((None, None, 3, d),
                         lambda i, j: (layer, _row_group(x_blk(i, j), cfg, tm), 0, 0)),
            pl.BlockSpec((None, 1, d), lambda i, j: (layer, 0, 0)),
            *[pl.BlockSpec((None, d // W_IN_SPLIT, tn),
                           functools.partial(lambda i, j, q: (layer, q, jnp.minimum(j, nj - 1)), q=q))
              for q in range(W_IN_SPLIT)],
            pl.BlockSpec((None, 2, cfg.head_dim), lambda i, j: (layer, 0, 0)),
            tab_spec, tab_spec, tab_spec,
        ],
        out_specs=[
            pl.BlockSpec((tm, tn), lambda i, j: (i, 0)),
            pl.BlockSpec((tm, tn), main_blk),
        ],
        out_shape=[
            jax.ShapeDtypeStruct((cfg.tokens, tn), F32),
            jax.ShapeDtypeStruct((cfg.tokens, cfg.main_width), BF16),
        ],
        scratch_shapes=[pltpu.VMEM((tm, d), BF16), pltpu.VMEM((tm, d), BF16),
                        pltpu.VMEM((tm, tn), F32), pltpu.VMEM((tm, tn), F32)],
        compiler_params=_params("arbitrary", "arbitrary"),
        name="in_proj",
    )(x, mod, norm_g, *([w_in_b] * W_IN_SPLIT), gains, cos_t, sa_t, sb_t)


def _softmax_attend(q, k, v1):
    hd = q.shape[1]
    s = lax.dot_general(q, k, (((1,), (1,)), ((), ())), preferred_element_type=F32)
    m = jnp.max(s, axis=-1, keepdims=True)
    p = jnp.exp2(s - m).astype(BF16)
    o = jnp.dot(p, v1, preferred_element_type=F32)
    return o[:, :hd] / o[:, hd:]


def _attn_ctx_kernel(prev_ref, q_ref, kv_ref, g_ref, o_ref, *, cfg):
    del prev_ref
    hd, kvw = cfg.head_dim, cfg.kv_width
    for kh in range(cfg.n_kv):
        k = kv_ref[:, kh * hd:(kh + 1) * hd].astype(BF16)
        v = kv_ref[:, kvw + kh * hd:kvw + (kh + 1) * hd].astype(BF16)
        v1 = jnp.concatenate([v, jnp.ones_like(v)], axis=1)
        for h in range(kh * cfg.q_per_kv, (kh + 1) * cfg.q_per_kv):
            sl = slice(h * hd, (h + 1) * hd)
            o = _softmax_attend(q_ref[:, sl], k, v1)
            o_ref[:, sl] = (o * g_ref[:, sl].astype(F32)).astype(BF16)


def _attn_context(attn_buf, main, kv, cfg):
    aw = cfg.attn_width
    return pl.pallas_call(
        functools.partial(_attn_ctx_kernel, cfg=cfg),
        grid=(cfg.batch,),
        in_specs=[
            pl.BlockSpec(memory_space=pl.ANY),
            pl.BlockSpec((cfg.seq, aw), lambda b: (b, 0)),
            pl.BlockSpec((cfg.seq, 2 * cfg.kv_width), lambda b: (b, 0)),
            pl.BlockSpec((cfg.seq, aw), lambda b: (b, 1)),
        ],
        out_specs=pl.BlockSpec((cfg.seq, aw), lambda b: (b, 0)),
        out_shape=jax.ShapeDtypeStruct((cfg.tokens, aw), BF16),
        input_output_aliases={0: 0},
        compiler_params=_params("parallel"),
        name="attn_context",
    )(attn_buf, main, kv, main)


def _attn_lat_kernel(prev_ref, q_ref, ck_ref, cv_ref, kn_ref, vn_ref, g_ref, o_ref,
                     k_scr, v_scr, *, cfg):
    del prev_ref
    hd, past = cfg.head_dim, cfg.past_len

    @pl.when(pl.program_id(2) == 0)
    def _():
        k_scr[0:past, :] = ck_ref[...].astype(BF16)
        k_scr[past:, :] = kn_ref[...].astype(BF16)
        v_scr[0:past, 0:hd] = cv_ref[...].astype(BF16)
        v_scr[past:, 0:hd] = vn_ref[...].astype(BF16)
        v_scr[:, hd:] = jnp.ones((v_scr.shape[0], hd), BF16)

    k = k_scr[...]
    v1 = v_scr[...]
    rq = min(cfg.rq, q_ref.shape[0])
    for h in range(cfg.q_per_kv):
        sl = slice(h * hd, (h + 1) * hd)
        for r in range(q_ref.shape[0] // rq):
            rows = slice(r * rq, (r + 1) * rq)
            o = _softmax_attend(q_ref[rows, sl], k, v1)
            o_ref[rows, sl] = (o * g_ref[rows, sl].astype(F32)).astype(BF16)


def _attn_latent(attn_prev, main, kv, cache_k, cache_v, layer, cfg):
    hd, nkv, tq = cfg.head_dim, cfg.n_kv, cfg.tq
    qw = cfg.q_per_kv * hd
    nqb = cfg.dec_seq // tq
    row0 = cfg.p_tokens // tq
    kvrow0 = cfg.p_tokens // cfg.dec_seq
    nkeys = cfg.past_len + cfg.dec_seq

    def q_map(b, h, qi):
        return (row0 + b * nqb + qi, h)

    def g_map(b, h, qi):
        return (row0 + b * nqb + qi, nkv + h)

    cache_spec = pl.BlockSpec((None, None, cfg.past_len, hd), lambda b, h, qi: (b, layer, 0, h))
    return pl.pallas_call(
        functools.partial(_attn_lat_kernel, cfg=cfg),
        grid=(cfg.dec_batch, nkv, nqb),
        in_specs=[
            pl.BlockSpec(memory_space=pl.ANY),
            pl.BlockSpec((tq, qw), q_map),
            cache_spec, cache_spec,
            pl.BlockSpec((cfg.dec_seq, hd), lambda b, h, qi: (kvrow0 + b, h)),
            pl.BlockSpec((cfg.dec_seq, hd), lambda b, h, qi: (kvrow0 + b, nkv + h)),
            pl.BlockSpec((tq, qw), g_map),
        ],
        out_specs=pl.BlockSpec((tq, qw), q_map),
        out_shape=jax.ShapeDtypeStruct((cfg.tokens, cfg.attn_width), BF16),
        scratch_shapes=[pltpu.VMEM((nkeys, hd), BF16), pltpu.VMEM((nkeys, 2 * hd), BF16)],
        input_output_aliases={0: 0},
        compiler_params=_params("parallel", "parallel", "arbitrary"),
        name="attn_latent",
    )(attn_prev, main, cache_k, cache_v, kv, kv, main)


def _ssm_kernel(u_ref, perm_ref, catf_ref, catb_ref, bbt_ref, pa_ref, w2_ref, av_ref,
                pw_ref, dsk_ref, h0_ref, y_ref, fin_ref,
                uf_scr, xcat_scr, u8_scr, m_scr, w1s_scr, s_scr, hin_scr, f_scr, g_scr, *, cfg):
    T, C = SSM_CHUNK, cfg.ssm_group
    tc = T * C
    st = 2 * cfg.ssm_state
    n_k = cfg.n_k
    chain = T * n_k
    nc = cfg.p_tokens // chain
    n_sc = cfg.dec_seq // chain
    gt = LANES // C
    jl_n = LANES // C
    jh_n = T // jl_n
    is_latent = pl.program_id(1) == 1
    y8_scr = xcat_scr
    pitch = chain + SSM_ROW_PAD

    def stage_in(c, carry):
        src = pl.ds(pl.multiple_of(c * chain, chain), chain)
        uf_scr[pl.ds(pl.multiple_of(c * pitch, 8), chain), :] = u_ref[src, :].astype(F32)
        return carry

    lax.fori_loop(0, nc, stage_in, 0)

    def gather(k, carry):
        for j in range(T):
            piece = uf_scr[pl.ds(k * T + j, nc, stride=pitch), :].astype(BF16)
            xcat_scr[j // jl_n, pl.ds(pl.multiple_of(k * nc, nc), nc),
                     (j % jl_n) * LANES:(j % jl_n + 1) * LANES] = piece
        return carry

    lax.fori_loop(0, n_k, gather, 0)
    for jh in range(jh_n):
        t = jnp.dot(xcat_scr[jh], perm_ref[...], preferred_element_type=F32)
        for r in range(gt):
            u8_scr[r, :, jh * LANES:(jh + 1) * LANES] = t[:, r * LANES:(r + 1) * LANES].astype(BF16)

    def cmul(h, hs, c1, c2):
        return h * c1 + hs * c2, hs * c1 - h * c2

    lane = lax.broadcasted_iota(jnp.int32, (C, tc), 1)

    for r in range(gt):
        ktf = jnp.dot(bbt_ref[r, 0], catf_ref[r], preferred_element_type=F32, precision=HIGHEST)
        ktb = jnp.dot(bbt_ref[r, 1], catb_ref[r], preferred_element_type=F32, precision=HIGHEST)
        for j in range(T):
            rf = pltpu.roll(ktf, C * j, 1) if j else ktf
            rf = jnp.where(lane >= C * j, rf, 0.0)
            sh = (T - 1 - j) * C
            rb = pltpu.roll(ktb, tc - sh, 1) if sh else ktb
            rb = jnp.where(lane < tc - sh, rb, 0.0)
            m_scr[j * C:(j + 1) * C, :] = (rf + rb).astype(BF16)
        for d in range(2):
            bb = bbt_ref[r, d]
            bbs = pltpu.roll(bb, st // 2, 1)
            for j in range(T):
                w, ws = cmul(bb, bbs, pa_ref[r, d, j, 0:1, :], pa_ref[r, d, j, 1:2, :])
                w1s_scr[j * C:(j + 1) * C, 2 * d * st:(2 * d + 1) * st] = w.astype(BF16)
                w1s_scr[j * C:(j + 1) * C, (2 * d + 1) * st:(2 * d + 2) * st] = ws.astype(BF16)

        u = u8_scr[r]
        y_intra = jnp.dot(u, m_scr[...], preferred_element_type=F32)
        s_scr[...] = jnp.dot(u, w1s_scr[...], preferred_element_type=F32)
        av = av_ref[r]
        a1 = (av[0:1], av[2:3])
        a2 = (av[1:2], av[3:4])
        b1 = (av[4:5], av[6:7])
        b2 = (av[5:6], av[7:8])

        def level1(k, carry):
            hf, hfs, hb, hbs = carry
            rf_ = pl.ds(pl.multiple_of(k * nc, nc), nc)
            rb_ = pl.ds(pl.multiple_of((n_k - 1 - k) * nc, nc), nc)
            hin_scr[rf_, 0:st] = hf
            hin_scr[rf_, st:2 * st] = hfs
            hin_scr[rb_, 2 * st:3 * st] = hb
            hin_scr[rb_, 3 * st:4 * st] = hbs
            nf, nfs = cmul(hf, hfs, a1[0], a2[0])
            nbk, nbs = cmul(hb, hbs, a1[1], a2[1])
            return (nf + s_scr[rf_, 0:st], nfs + s_scr[rf_, st:2 * st],
                    nbk + s_scr[rb_, 2 * st:3 * st], nbs + s_scr[rb_, 3 * st:4 * st])

        z = jnp.zeros((nc, st), F32)
        ff, ffs, fb, fbs = lax.fori_loop(0, n_k, level1, (z, z, z, z))

        @pl.when(jnp.logical_not(is_latent))
        def _():
            fin_ref[r, 0] = ff
            fin_ref[r, 1] = fb

        @pl.when(is_latent)
        def _():
            for idx, val in enumerate((ff, ffs, fb, fbs)):
                f_scr[idx] = val
            h0 = h0_ref[r]
            gf, gfs, gb, gbs = h0[0], h0[1], h0[2], h0[3]
            for sc in range(n_sc):
                sel = pl.ds(sc, cfg.dec_batch, stride=n_sc)
                g_scr[0, sel, :] = gf
                g_scr[1, sel, :] = gfs
                nf, nfs = cmul(gf, gfs, b1[0], b2[0])
                gf = nf + f_scr[0, sel, :]
                gfs = nfs + f_scr[1, sel, :]
            for sc in range(n_sc - 1, -1, -1):
                sel = pl.ds(sc, cfg.dec_batch, stride=n_sc)
                g_scr[2, sel, :] = gb
                g_scr[3, sel, :] = gbs
                nbk, nbs = cmul(gb, gbs, b1[1], b2[1])
                gb = nbk + f_scr[2, sel, :]
                gbs = nbs + f_scr[3, sel, :]
            gfa, gfsa, gba, gbsa = g_scr[0], g_scr[1], g_scr[2], g_scr[3]
            for k in range(n_k):
                rows_k = slice(k * nc, (k + 1) * nc)
                pf = pw_ref[r, k]
                cf, cfs = cmul(gfa, gfsa, pf[0:1], pf[1:2])
                cb, cbs = cmul(gba, gbsa, pf[2:3], pf[3:4])
                hin_scr[rows_k, 0:st] += cf
                hin_scr[rows_k, st:2 * st] += cfs
                hin_scr[rows_k, 2 * st:3 * st] += cb
                hin_scr[rows_k, 3 * st:4 * st] += cbs

        hsel = jnp.concatenate([hin_scr[:, 0:st], hin_scr[:, 2 * st:3 * st]], axis=1).astype(BF16)
        y = y_intra + jnp.dot(hsel, w2_ref[r], preferred_element_type=F32)
        y = (y + dsk_ref[r] * u.astype(F32)).astype(BF16)
        for jh in range(jh_n):
            y8_scr[jh, :, r * LANES:(r + 1) * LANES] = y[:, jh * LANES:(jh + 1) * LANES]

    for jh in range(jh_n):
        t = lax.dot_general(y8_scr[jh], perm_ref[...], (((1,), (1,)), ((), ())),
                            preferred_element_type=F32)
        for k in range(n_k):
            for jl in range(jl_n):
                uf_scr[pl.ds(k * T + jh * jl_n + jl, nc, stride=pitch), :] = (
                    t[k * nc:(k + 1) * nc, jl * LANES:(jl + 1) * LANES])

    def stage_out(c, carry):
        dst = pl.ds(pl.multiple_of(c * chain, chain), chain)
        y_ref[dst, :] = uf_scr[pl.ds(pl.multiple_of(c * pitch, 8), chain), :].astype(BF16)
        return carry

    lax.fori_loop(0, nc, stage_out, 0)


def _ssm_scan(main, ssm_w, h0, layer, cfg):
    perm, catf, catb, bbt, pa, w2, av, pw, dsk = ssm_w
    T, C = SSM_CHUNK, cfg.ssm_group
    tc = T * C
    st = 2 * cfg.ssm_state
    gt = LANES // C
    nq = cfg.ssm_width // LANES
    chain = T * cfg.n_k
    half = cfg.p_tokens
    assert cfg.seq == chain and cfg.dec_seq % chain == 0 and T % gt == 0 and cfg.s_tokens == half
    nc = half // chain
    rows = nc * cfg.n_k
    pw_n = gt * LANES
    ucol = 2 * cfg.attn_width // LANES

    def wspec(*shape):
        nd = len(shape)
        return pl.BlockSpec((None, gt) + shape, lambda q, s: (layer, q) + (0,) * nd)

    return pl.pallas_call(
        functools.partial(_ssm_kernel, cfg=cfg),
        grid=(nq, 2),
        in_specs=[
            pl.BlockSpec((half, LANES), lambda q, s: (s, ucol + q)),
            pl.BlockSpec((pw_n, pw_n), lambda q, s: (0, 0)),
            wspec(st, tc), wspec(st, tc), wspec(2, C, st), wspec(2, T, 2, st), wspec(2 * st, tc),
            wspec(8, st), wspec(cfg.n_k, 4, st), wspec(1, tc), wspec(4, cfg.dec_batch, st),
        ],
        out_specs=[
            pl.BlockSpec((half, LANES), lambda q, s: (s, q)),
            pl.BlockSpec((gt, 2, cfg.batch, st), lambda q, s: (q, 0, 0, 0)),
        ],
        out_shape=[
            jax.ShapeDtypeStruct((cfg.tokens, cfg.ssm_width), BF16),
            jax.ShapeDtypeStruct((cfg.n_groups, 2, cfg.batch, st), F32),
        ],
        scratch_shapes=[
            pltpu.VMEM((nc * (chain + SSM_ROW_PAD), LANES), F32),
            pltpu.VMEM((T // gt, rows, pw_n), BF16),
            pltpu.VMEM((gt, rows, tc), BF16),
            pltpu.VMEM((tc, tc), BF16),
            pltpu.VMEM((tc, 4 * st), BF16),
            pltpu.VMEM((rows, 4 * st), F32),
            pltpu.VMEM((rows, 4 * st), F32),
            pltpu.VMEM((4, nc, st), F32),
            pltpu.VMEM((4, nc, st), F32),
        ],
        compiler_params=_params("parallel", "arbitrary"),
        name="ssm_scan",
    )(main, perm, catf, catb, bbt, pa, w2, av, pw, dsk, h0)


def _ssm_weights(lam_re, lam_im, log_step, b_re, b_im, c_re, c_im, d_skip, cfg):
    T, C, P, G, nk = SSM_CHUNK, cfg.ssm_group, cfg.ssm_state, cfg.n_groups, cfg.n_k
    lr, li = lam_re.astype(F32), lam_im.astype(F32)
    dt = jnp.exp(log_step.astype(F32))[..., None]

    def powers(js):
        jj = jnp.asarray(js, F32)[:, None]
        mag = jnp.exp(lr[..., None, :] * dt[..., None, :] * jj)
        ang = li[..., None, :] * dt[..., None, :] * jj
        return mag * jnp.cos(ang), mag * jnp.sin(ang)

    pr, pi = powers(range(T + 1))
    ab_re, ab_im = pr[..., 1, :], pi[..., 1, :]
    nr, ni = ab_re - 1.0, ab_im
    den = lr * lr + li * li
    f_re = (nr * lr + ni * li) / den
    f_im = (ni * lr - nr * li) / den
    br, bi = b_re.astype(F32), b_im.astype(F32)
    bb_re = f_re[..., None] * br - f_im[..., None] * bi
    bb_im = f_re[..., None] * bi + f_im[..., None] * br
    cr, ci = c_re.astype(F32), c_im.astype(F32)
    ca_re = cr[..., None, :, :] * pr[..., :, None, :] - ci[..., None, :, :] * pi[..., :, None, :]
    ca_im = cr[..., None, :, :] * pi[..., :, None, :] + ci[..., None, :, :] * pr[..., :, None, :]

    def state_out(d, lo, rev):
        def pick(x):
            x = x[:, d, :, lo:lo + T]
            x = jnp.flip(x, axis=2) if rev else x
            return x.transpose(0, 1, 4, 2, 3).reshape(-1, G, P, T * C)
        return jnp.concatenate([pick(ca_re), -pick(ca_im)], axis=2)

    catf = state_out(0, 0, False)
    catb = state_out(1, 0, True)
    w2 = jnp.concatenate([state_out(0, 1, False), state_out(1, 1, True)], axis=2).astype(BF16)
    bbt = jnp.concatenate([bb_re, bb_im], axis=3).transpose(0, 2, 1, 4, 3)

    def lanes(re, im):
        return jnp.concatenate([re, re], axis=-1), jnp.concatenate([-im, im], axis=-1)

    paf = lanes(jnp.flip(pr[:, 0, :, 0:T], axis=2), jnp.flip(pi[:, 0, :, 0:T], axis=2))
    pab = lanes(pr[:, 1, :, 0:T], pi[:, 1, :, 0:T])
    pa = jnp.stack([jnp.stack(paf, axis=3), jnp.stack(pab, axis=3)], axis=2)

    qr, qi = powers([T * k for k in range(nk + 1)])
    rows = []
    for src_r, src_i, idx in ((pr, pi, T), (qr, qi, nk)):
        for d in range(2):
            rows.extend(lanes(src_r[:, d, :, idx], src_i[:, d, :, idx]))
    av = jnp.stack(rows, axis=2)
    p1f, p2f = lanes(qr[:, 0, :, 0:nk], qi[:, 0, :, 0:nk])
    p1b, p2b = lanes(jnp.flip(qr[:, 1, :, 0:nk], axis=2), jnp.flip(qi[:, 1, :, 0:nk], axis=2))
    pw = jnp.stack([p1f, p2f, p1b, p2b], axis=3)
    dsk = jnp.tile(d_skip.astype(F32).reshape(-1, G, 1, C), (1, 1, 1, T))

    gt = LANES // C
    n = gt * LANES
    src = jnp.arange(n)
    jl, r, c = src // LANES, (src % LANES) // C, src % C
    dst = r * LANES + jl * C + c
    perm = (dst[:, None] == jnp.arange(n)[None, :]).astype(BF16)
    return perm, catf, catb, bbt, pa, w2, av, pw, dsk


def _glu_kernel(y_ref, w_ref, g_ref, o_ref, *, cfg):
    z = jnp.dot(y_ref[...], w_ref[...], preferred_element_type=F32)
    sw = cfg.ssm_width
    a, g = z[:, :sw], z[:, sw:]
    o_ref[...] = (a / (1.0 + jnp.exp(-g)) * g_ref[...].astype(F32)).astype(BF16)


def _ssm_glu(y, w_glu_b, main, layer, cfg):
    tm, sw = cfg.tm_post, cfg.ssm_width
    gcol = (2 * cfg.attn_width + sw) // sw
    assert (2 * cfg.attn_width) % sw == 0
    return pl.pallas_call(
        functools.partial(_glu_kernel, cfg=cfg),
        grid=(cfg.tokens // tm,),
        in_specs=[
            pl.BlockSpec((tm, sw), lambda i: (i, 0)),
            pl.BlockSpec((None, sw, 2 * sw), lambda i: (layer, 0, 0)),
            pl.BlockSpec((tm, sw), lambda i: (i, gcol)),
        ],
        out_specs=pl.BlockSpec((tm, sw), lambda i: (i, 0)),
        out_shape=jax.ShapeDtypeStruct((cfg.tokens, sw), BF16),
        compiler_params=_params("parallel"),
        name="ssm_glu",
    )(y, w_glu_b, main)


def _fourier_kernel(prev_ref, x_ref, dl_ref, cs_ref, w_ref, g_ref, o_ref, z_scr, *, cfg, length):
    del prev_ref
    fg = cfg.fft_group
    rc = min(cfg.rc_fft, length)

    @pl.when(pl.program_id(1) == 0)
    def _():
        def body(ci, carry):
            rows = pl.ds(pl.multiple_of(ci * rc, rc), rc)
            rows2 = pl.ds(pl.multiple_of(length + ci * rc, rc), rc)
            for g in range(cfg.fft_width // fg):
                cols = slice(g * fg, (g + 1) * fg)
                t = jnp.dot(x_ref[rows, cols], cs_ref[...], preferred_element_type=F32)
                z_scr[rows, cols] = t[:, :fg].astype(BF16)
                z_scr[rows2, cols] = t[:, fg:].astype(BF16)
            return carry

        lax.fori_loop(0, length // rc, body, 0)

    mixed = jnp.dot(dl_ref[...], z_scr[...], preferred_element_type=F32).astype(BF16)
    four = jnp.dot(mixed, w_ref[...], preferred_element_type=F32)
    o_ref[...] = (four * g_ref[...].astype(F32)).astype(BF16)


def _fourier(prev, main, dl, cs, w_fft_b, layer, cfg, *, length, nbatch, row0):
    fw = cfg.fft_width
    tm = min(cfg.tm_fft, length)
    nr = length // tm
    xcol = (2 * cfg.attn_width + 2 * cfg.ssm_width) // fw
    assert (2 * cfg.attn_width + 2 * cfg.ssm_width) % fw == 0 and row0 % length == 0
    b0 = row0 // length
    t0 = row0 // tm

    def tile_map(b, r):
        return (t0 + b * nr + r, 0)

    return pl.pallas_call(
        functools.partial(_fourier_kernel, cfg=cfg, length=length),
        grid=(nbatch, nr),
        in_specs=[
            pl.BlockSpec(memory_space=pl.ANY),
            pl.BlockSpec((length, fw), lambda b, r: (b0 + b, xcol)),
            pl.BlockSpec((tm, 2 * length), lambda b, r: (r, 0)),
            pl.BlockSpec((cfg.fft_group, 2 * cfg.fft_group), lambda b, r: (0, 0)),
            pl.BlockSpec((None, fw, fw), lambda b, r: (layer, 0, 0)),
            pl.BlockSpec((tm, fw), lambda b, r: (t0 + b * nr + r, xcol + 1)),
        ],
        out_specs=pl.BlockSpec((tm, fw), tile_map),
        out_shape=jax.ShapeDtypeStruct((cfg.tokens, fw), BF16),
        scratch_shapes=[pltpu.VMEM((2 * length, fw), BF16)],
        input_output_aliases={0: 0},
        compiler_params=_params("parallel", "arbitrary"),
        name=f"fourier_{length}",
    )(prev, main, dl, cs, w_fft_b, main)


def _dft_cos_sin(n):
    k = jnp.arange(n, dtype=jnp.int32)
    w = 2.0 * math.pi / n

    ang = ((k[:, None] * k[None, :]) % n).astype(F32) * w
    scale = n ** -0.5
    return jnp.cos(ang) * scale, jnp.sin(ang) * scale


def _dft_position_table(n):
    s = 1
    while s * s < n:
        s *= 2
    if n <= 512 or n % s:
        c, sn = _dft_cos_sin(n)
        return jnp.concatenate([c, -sn], axis=1).astype(BF16)
    k = jnp.arange(n, dtype=jnp.int32)
    w = 2.0 * math.pi / n

    def cs(j):
        ang = ((j[:, None] * k[None, :]) % n).astype(F32) * w
        return jnp.cos(ang), jnp.sin(ang)

    c1, s1 = cs(jnp.arange(n // s, dtype=jnp.int32) * s)
    c0, s0 = cs(jnp.arange(s, dtype=jnp.int32))
    a1 = jnp.concatenate([c1, -s1], axis=1)[:, None, :]
    a2 = jnp.concatenate([s1, c1], axis=1)[:, None, :]
    b0 = jnp.concatenate([c0, c0], axis=1)[None]
    b1 = jnp.concatenate([s0, s0], axis=1)[None]
    return ((a1 * b0 - a2 * b1) * n ** -0.5).reshape(n, 2 * n).astype(BF16)


def _out_proj_kernel(a_ref, s_ref, f_ref, w_ref, x_ref, mod_ref, o_ref, *, cfg):
    aw, sw = cfg.attn_width, cfg.ssm_width
    acc = jnp.dot(a_ref[...], w_ref[0:aw, :], preferred_element_type=F32)
    acc += jnp.dot(s_ref[...], w_ref[aw:aw + sw, :], preferred_element_type=F32)
    acc += jnp.dot(f_ref[...], w_ref[aw + sw:, :], preferred_element_type=F32)
    o_ref[...] = x_ref[...] + mod_ref[2:3, :] * acc


def _out_proj(attn, ssm, four, w_out_b, x, mod, layer, cfg):
    tm, tn, d = cfg.tm_out, cfg.tn_out, cfg.d_model
    return pl.pallas_call(
        functools.partial(_out_proj_kernel, cfg=cfg),
        grid=(d // tn, cfg.tokens // tm),
        in_specs=[
            pl.BlockSpec((tm, cfg.attn_width), lambda j, i: (i, 0)),
            pl.BlockSpec((tm, cfg.ssm_width), lambda j, i: (i, 0)),
            pl.BlockSpec((tm, cfg.fft_width), lambda j, i: (i, 0)),
            pl.BlockSpec((None, d, tn), lambda j, i: (layer, 0, j)),
            pl.BlockSpec((tm, tn), lambda j, i: (i, j)),
            pl.BlockSpec((None, None, 3, tn), lambda j, i: (layer, _row_group(i, cfg, tm), 0, j)),
        ],
        out_specs=pl.BlockSpec((tm, tn), lambda j, i: (i, j)),
        out_shape=jax.ShapeDtypeStruct((cfg.tokens, d), F32),
        compiler_params=_params("parallel", "parallel"),
        name="out_proj",
    )(attn, ssm, four, w_out_b, x, mod)


def _final_norm_kernel(x_ref, g_ref, o_ref):
    x = x_ref[...]
    ms = jnp.mean(x * x, axis=-1, keepdims=True)
    o_ref[...] = x * lax.rsqrt(ms + NORM_EPS) * g_ref[...]


def _final_norm(x, g, cfg, *, row0, nrows):
    tm, d = 256, cfg.d_model
    tm = min(tm, nrows)
    assert row0 % tm == 0
    return pl.pallas_call(
        _final_norm_kernel,
        grid=(nrows // tm,),
        in_specs=[pl.BlockSpec((tm, d), lambda i: (row0 // tm + i, 0)),
                  pl.BlockSpec((1, d), lambda i: (0, 0))],
        out_specs=pl.BlockSpec((tm, d), lambda i: (i, 0)),
        out_shape=jax.ShapeDtypeStruct((nrows, d), F32),
        compiler_params=_params("parallel"),
        name="final_norm",
    )(x, g)


def _rope_tables(cfg):
    hd = cfg.head_dim
    pairs = hd // 4
    t = jnp.arange(cfg.dec_seq)
    inv = ROPE_THETA ** (-jnp.arange(pairs, dtype=F32) / pairs)
    row_ang = (t // cfg.grid_w).astype(F32)[:, None] * inv[None, :]
    col_ang = (t % cfg.grid_w).astype(F32)[:, None] * inv[None, :]
    zeros = jnp.zeros_like(row_ang)
    cos = jnp.concatenate([jnp.cos(row_ang)] * 2 + [jnp.cos(col_ang)] * 2, axis=1)
    sa = jnp.concatenate([-jnp.sin(row_ang), zeros, -jnp.sin(col_ang), zeros], axis=1)
    sb = jnp.concatenate([zeros, jnp.sin(row_ang), zeros, jnp.sin(col_ang)], axis=1)
    ident = jnp.zeros((cfg.tm_in, hd), F32)
    return (jnp.concatenate([ident + 1.0, cos], axis=0),
            jnp.concatenate([ident, sa], axis=0),
            jnp.concatenate([ident, sb], axis=0))


def _forward(cfg, x_prompt, x_sample, cache_k, cache_v, state_fwd_re, state_fwd_im, state_bwd_re,
             state_bwd_im, c, c_ctx, norm_g, w_mod, b_mod, w_in, q_norm, k_norm, lam_re, lam_im,
             log_step, b_re, b_im, c_re, c_im, d_skip, w_glu, w_fft, w_out, final_norm_g):
    d, kvw = cfg.d_model, cfg.kv_width
    P = cfg.ssm_state

    w_in_b = w_in.astype(BF16)
    w_out_b = w_out.astype(BF16)
    w_glu_b = w_glu.astype(BF16)
    w_fft_b = w_fft.astype(BF16)
    gains = jnp.stack([k_norm, q_norm], axis=1).astype(F32)
    norm_g3 = norm_g.astype(F32)[:, None, :]
    ssm_w = _ssm_weights(lam_re, lam_im, log_step, b_re, b_im, c_re, c_im, d_skip, cfg)
    rope_tabs = _rope_tables(cfg)

    cc, sc = _dft_cos_sin(cfg.fft_group)
    cs = jnp.concatenate([cc, sc], axis=1).astype(BF16)
    dls = {}
    for length in {cfg.seq, cfg.dec_seq}:
        dls[length] = _dft_position_table(length)

    def st(re, im):
        re = re.astype(F32).transpose(1, 2, 0, 3)
        im = im.astype(F32).transpose(1, 2, 0, 3)
        return jnp.concatenate([re, im], axis=-1), jnp.concatenate([im, re], axis=-1)

    h0 = jnp.stack(st(state_fwd_re, state_fwd_im) + st(state_bwd_re, state_bwd_im), axis=2)

    nrow = 1 + cfg.dec_batch
    cvecs = jnp.concatenate([c_ctx[None, :], c], axis=0).astype(F32)
    cvecs = jnp.pad(cvecs, ((0, -nrow % 8), (0, 0)))
    mod = _modulation(cvecs, w_mod, b_mod, cfg).reshape(cfg.depth, cvecs.shape[0], 3, d)

    cache_k4 = cache_k.reshape(cfg.dec_batch, cfg.depth, cfg.past_len, kvw)
    cache_v4 = cache_v.reshape(cfg.dec_batch, cfg.depth, cfg.past_len, kvw)

    x = jnp.concatenate([x_prompt.reshape(cfg.p_tokens, d), x_sample.reshape(cfg.s_tokens, d)], axis=0)
    attn = jnp.zeros((cfg.tokens, cfg.attn_width), BF16)
    four = jnp.zeros((cfg.tokens, cfg.fft_width), BF16)
    ks, vs, fins = [], [], []
    for l in range(cfg.depth):
        kv, main = _in_proj(x, mod, norm_g3, w_in_b, gains, rope_tabs, l, cfg)
        attn = _attn_context(attn, main, kv, cfg)
        attn = _attn_latent(attn, main, kv, cache_k4, cache_v4, l, cfg)
        y, fin = _ssm_scan(main, ssm_w, h0, l, cfg)
        ssm = _ssm_glu(y, w_glu_b, main, l, cfg)
        four = _fourier(four, main, dls[cfg.seq], cs, w_fft_b, l, cfg,
                        length=cfg.seq, nbatch=cfg.batch, row0=0)
        four = _fourier(four, main, dls[cfg.dec_seq], cs, w_fft_b, l, cfg,
                        length=cfg.dec_seq, nbatch=cfg.dec_batch, row0=cfg.p_tokens)
        x = _out_proj(attn, ssm, four, w_out_b, x, mod, l, cfg)
        ks.append(kv[:cfg.p_tokens, :kvw].reshape(cfg.batch, cfg.seq, cfg.n_kv, cfg.head_dim))
        vs.append(kv[:cfg.p_tokens, kvw:].reshape(cfg.batch, cfg.seq, cfg.n_kv, cfg.head_dim))
        fins.append(fin)

    g = final_norm_g[None, :].astype(F32)
    y_prompt = _final_norm(x, g, cfg, row0=0, nrows=cfg.p_tokens).reshape(cfg.batch, cfg.seq, d)
    y_sample = _final_norm(x, g, cfg, row0=cfg.p_tokens, nrows=cfg.s_tokens).reshape(
        cfg.dec_batch, cfg.dec_seq, d)
    fin = jnp.stack(fins, axis=0)
    fin = fin.transpose(3, 0, 1, 2, 4)
    return (y_prompt, y_sample, jnp.stack(ks, axis=1), jnp.stack(vs, axis=1),
            fin[:, :, :, 0, :P], fin[:, :, :, 0, P:], fin[:, :, :, 1, :P], fin[:, :, :, 1, P:])


def kernel(x_prompt, x_sample, cache_k, cache_v, state_fwd_re, state_fwd_im, state_bwd_re, state_bwd_im,
           c, c_ctx, norm_g, w_mod, b_mod, w_in, q_norm, k_norm, lam_re, lam_im, log_step,
           b_re, b_im, c_re, c_im, d_skip, w_glu, w_fft, w_out, final_norm_g):
    return _forward(Cfg(), x_prompt, x_sample, cache_k, cache_v, state_fwd_re, state_fwd_im,
                    state_bwd_re, state_bwd_im, c, c_ctx, norm_g, w_mod, b_mod, w_in, q_norm, k_norm,
                    lam_re, lam_im, log_step, b_re, b_im, c_re, c_im, d_skip, w_glu, w_fft, w_out,
                    final_norm_g)
```

```python
import functools
import math
from typing import NamedTuple

import jax
import jax.numpy as jnp
from jax import lax
from jax.experimental import pallas as pl
from jax.experimental.pallas import tpu as pltpu

F32 = jnp.float32
BF16 = jnp.bfloat16
NORM_EPS = 1e-6
ROPE_THETA = 10000.0
LANES = 128
SSM_CHUNK = 16
SSM_ROW_PAD = 8
VMEM_LIMIT = 56 * 1024 * 1024
LOG2E = 1.4426950408889634
HIGHEST = lax.Precision.HIGHEST


class Cfg(NamedTuple):
    d_model: int = 4096
    batch: int = 32
    seq: int = 256
    depth: int = 4
    dec_batch: int = 2
    dec_seq: int = 4096
    past_len: int = 512
    grid_w: int = 64
    head_dim: int = 128
    n_heads: int = 16
    n_kv: int = 4
    ssm_width: int = 1024
    ssm_group: int = 16
    ssm_state: int = 64
    fft_width: int = 1024
    fft_group: int = 256
    tm_in: int = 512
    tm_out: int = 512
    tn_out: int = 1024
    tn_mod: int = 512
    tq: int = 512
    rq: int = 128
    tm_fft: int = 256
    rc_fft: int = 512
    tm_post: int = 512
    n_k: int = 16

    @property
    def attn_width(self):
        return self.n_heads * self.head_dim

    @property
    def kv_width(self):
        return self.n_kv * self.head_dim

    @property
    def q_per_kv(self):
        return self.n_heads // self.n_kv

    @property
    def n_groups(self):
        return self.ssm_width // self.ssm_group

    @property
    def tn_in(self):
        return 2 * self.kv_width

    @property
    def main_width(self):
        return 2 * self.attn_width + 2 * self.ssm_width + 2 * self.fft_width

    @property
    def p_tokens(self):
        return self.batch * self.seq

    @property
    def s_tokens(self):
        return self.dec_batch * self.dec_seq

    @property
    def tokens(self):
        return self.p_tokens + self.s_tokens


def _params(*sem):
    return pltpu.CompilerParams(dimension_semantics=sem, vmem_limit_bytes=VMEM_LIMIT)


def _silu(x):
    return x / (1.0 + jnp.exp(-x))


def _row_group(i, cfg, tm):
    npb = cfg.p_tokens // tm
    per = cfg.dec_seq // tm
    return jnp.where(i < npb, 0, 1 + (i - npb) // per)


def _mod_kernel(c_ref, w_ref, b_ref, o_ref):
    s = _silu(c_ref[...]).astype(BF16)
    o_ref[0] = jnp.dot(s, w_ref[0].astype(BF16), preferred_element_type=F32) + b_ref[0]


def _modulation(cvecs, w_mod, b_mod, cfg):
    d, n = cfg.d_model, 3 * cfg.d_model
    nrow = cvecs.shape[0]
    return pl.pallas_call(
        _mod_kernel,
        grid=(cfg.depth, n // cfg.tn_mod),
        in_specs=[
            pl.BlockSpec((nrow, d), lambda l, j: (0, 0)),
            pl.BlockSpec((1, d, cfg.tn_mod), lambda l, j: (l, 0, j)),
            pl.BlockSpec((1, 1, cfg.tn_mod), lambda l, j: (l, 0, j)),
        ],
        out_specs=pl.BlockSpec((1, nrow, cfg.tn_mod), lambda l, j: (l, 0, j)),
        out_shape=jax.ShapeDtypeStruct((cfg.depth, nrow, n), F32),
        compiler_params=_params("parallel", "parallel"),
        name="modulation",
    )(cvecs, w_mod, b_mod.reshape(cfg.depth, 1, n))


def _in_proj_kernel(xa_hbm, xb_hbm, mod_ref, ng_ref, w_ref, gains_ref, cos_ref, sa_ref, sb_ref,
                    kv_ref, main_ref, h_scr, x_scr, x_sem, *, cfg, kinds, npb, nb, xb_row0):
    i = pl.program_id(0)
    j = pl.program_id(1)
    tm = h_scr.shape[0]
    hd = cfg.head_dim
    tn = w_ref.shape[1]
    rc = min(tm, 16)

    def x_copy(src_hbm, row0):
        return pltpu.make_async_copy(src_hbm.at[pl.ds(pl.multiple_of(row0, tm), tm), :], x_scr, x_sem)

    def start_x(blk):
        @pl.when(blk < npb)
        def _():
            x_copy(xa_hbm, blk * tm).start()

        @pl.when(blk >= npb)
        def _():
            x_copy(xb_hbm, (blk - npb) * tm + xb_row0).start()

    @pl.when(j == 0)
    def _():
        @pl.when(i == 0)
        def _():
            start_x(i)

        x_copy(xa_hbm, 0).wait()
        shift = mod_ref[0:1, :]
        scale1 = 1.0 + mod_ref[1:2, :]
        g = ng_ref[...]

        def body(r, carry):
            rows = pl.ds(pl.multiple_of(r * rc, rc), rc)
            x = x_scr[rows, :]
            ms = jnp.mean(x * x, axis=-1, keepdims=True)
            y = x * lax.rsqrt(ms + NORM_EPS) * g
            h_scr[rows, :] = (y * scale1 + shift).astype(BF16)
            return carry

        lax.fori_loop(0, tm // rc, body, 0, unroll=min(4, tm // rc))

    @pl.when((j == 1) & (i + 1 < nb))
    def _():
        start_x(i + 1)

    sub = min(2 * hd, tn)

    def sub_dots():
        for s in range(tn // sub):
            yield s * sub, jnp.dot(h_scr[...], w_ref[:, s * sub:(s + 1) * sub],
                                   preferred_element_type=F32)

    def head_norm_rope(a, gain):
        ms = jnp.mean(a * a, axis=-1, keepdims=True)
        y = a * lax.rsqrt(ms + NORM_EPS) * gain
        return (y * cos_ref[...] + pltpu.roll(y, hd - hd // 4, 1) * sa_ref[...]
                + pltpu.roll(y, hd // 4, 1) * sb_ref[...])

    def tile(kind):
        for c0, acc in sub_dots():
            if kind in ("q", "kv"):
                for h in range(sub // hd):
                    sl = slice(c0 + h * hd, c0 + (h + 1) * hd)
                    a = acc[:, h * hd:(h + 1) * hd]
                    if kind == "q":
                        y = head_norm_rope(a, gains_ref[1:2, :]) * (hd ** -0.5 * LOG2E)
                        main_ref[:, sl] = y.astype(BF16)
                    elif c0 + h * hd < cfg.kv_width:
                        kv_ref[:, sl] = head_norm_rope(a, gains_ref[0:1, :])
                    else:
                        kv_ref[:, sl] = a
            elif kind == "silu":
                main_ref[:, c0:c0 + sub] = _silu(acc).astype(BF16)
            else:
                main_ref[:, c0:c0 + sub] = acc.astype(BF16)

    for kind in sorted(set(kinds)):
        pred = functools.reduce(lambda a, b: a | b, [j == t for t, k in enumerate(kinds) if k == kind])
        pl.when(pred)(functools.partial(tile, kind))


def _tile_kinds(cfg):
    tn = cfg.tn_in
    widths = [("q", cfg.attn_width), ("kv", 2 * cfg.kv_width), ("silu", cfg.attn_width),
              ("plain", cfg.ssm_width), ("silu", cfg.ssm_width), ("plain", cfg.fft_width),
              ("silu", cfg.fft_width)]
    kinds = []
    for name, w in widths:
        assert w % tn == 0
        kinds += [name] * (w // tn)
    return tuple(kinds)


def _in_proj(xa, xb, xb_row0, mod, norm_g, w_in_b, gains, rope_tabs, layer, cfg):
    tm, tn, d = cfg.tm_in, cfg.tn_in, cfg.d_model
    kinds = _tile_kinds(cfg)
    nj = len(kinds)
    kv0 = kinds.index("kv")
    nkv = 2 * cfg.kv_width // tn
    npb = cfg.p_tokens // tm
    per = cfg.dec_seq // tm
    cos_t, sa_t, sb_t = rope_tabs
    nb = cfg.tokens // tm
    assert xb_row0 % tm == 0 and cfg.p_tokens % tm == 0 and cfg.dec_seq % tm == 0
    tab_spec = pl.BlockSpec((tm, cfg.head_dim),
                            lambda i, j: (jnp.where(i < npb, 0, 1 + (i - npb) % per), 0))
    return pl.pallas_call(
        functools.partial(_in_proj_kernel, cfg=cfg, kinds=kinds, npb=npb, nb=nb, xb_row0=xb_row0),
        grid=(nb, nj),
        in_specs=[
            pl.BlockSpec(memory_space=pl.ANY),
            pl.BlockSpec(memory_space=pl.ANY),
            pl.BlockSpec((None, None, 3, d), lambda i, j: (layer, _row_group(i, cfg, tm), 0, 0)),
            pl.BlockSpec((None, 1, d), lambda i, j: (layer, 0, 0)),
            pl.BlockSpec((None, d, tn), lambda i, j: (layer, 0, j)),
            pl.BlockSpec((None, 2, cfg.head_dim), lambda i, j: (layer, 0, 0)),
            tab_spec, tab_spec, tab_spec,
        ],
        out_specs=[
            pl.BlockSpec((tm, tn), lambda i, j: (i, jnp.clip(j - kv0, 0, nkv - 1))),
            pl.BlockSpec((tm, tn), lambda i, j: (i, jnp.where(j < kv0, j, jnp.maximum(j - nkv, kv0 - 1)))),
        ],
        out_shape=[
            jax.ShapeDtypeStruct((cfg.tokens, 2 * cfg.kv_width), F32),
            jax.ShapeDtypeStruct((cfg.tokens, cfg.main_width), BF16),
        ],
        scratch_shapes=[pltpu.VMEM((tm, d), BF16), pltpu.VMEM((tm, d), F32),
                        pltpu.SemaphoreType.DMA],
        compiler_params=_params("arbitrary", "arbitrary"),
        name="in_proj",
    )(xa, xb, mod, norm_g, w_in_b, gains, cos_t, sa_t, sb_t)


def _softmax_attend(q, k, v1):
    hd = q.shape[1]
    s = lax.dot_general(q, k, (((1,), (1,)), ((), ())), preferred_element_type=F32)
    m = jnp.max(s, axis=-1, keepdims=True)
    p = jnp.exp2(s - m).astype(BF16)
    o = jnp.dot(p, v1, preferred_element_type=F32)
    return o[:, :hd] / o[:, hd:]


def _attn_ctx_kernel(prev_ref, q_ref, kv_ref, g_ref, o_ref, *, cfg):
    del prev_ref
    hd, kvw = cfg.head_dim, cfg.kv_width
    for kh in range(cfg.n_kv):
        k = kv_ref[:, kh * hd:(kh + 1) * hd].astype(BF16)
        v = kv_ref[:, kvw + kh * hd:kvw + (kh + 1) * hd].astype(BF16)
        v1 = jnp.concatenate([v, jnp.ones_like(v)], axis=1)
        for h in range(kh * cfg.q_per_kv, (kh + 1) * cfg.q_per_kv):
            sl = slice(h * hd, (h + 1) * hd)
            o = _softmax_attend(q_ref[:, sl], k, v1)
            o_ref[:, sl] = (o * g_ref[:, sl].astype(F32)).astype(BF16)


def _attn_context(attn_buf, main, kv, cfg):
    aw = cfg.attn_width
    return pl.pallas_call(
        functools.partial(_attn_ctx_kernel, cfg=cfg),
        grid=(cfg.batch,),
        in_specs=[
            pl.BlockSpec(memory_space=pl.ANY),
            pl.BlockSpec((cfg.seq, aw), lambda b: (b, 0)),
            pl.BlockSpec((cfg.seq, 2 * cfg.kv_width), lambda b: (b, 0)),
            pl.BlockSpec((cfg.seq, aw), lambda b: (b, 1)),
        ],
        out_specs=pl.BlockSpec((cfg.seq, aw), lambda b: (b, 0)),
        out_shape=jax.ShapeDtypeStruct((cfg.tokens, aw), BF16),
        input_output_aliases={0: 0},
        compiler_params=_params("parallel"),
        name="attn_context",
    )(attn_buf, main, kv, main)


def _attn_lat_kernel(prev_ref, q_ref, ck_ref, cv_ref, kn_ref, vn_ref, g_ref, o_ref,
                     k_scr, v_scr, *, cfg):
    del prev_ref
    hd, past = cfg.head_dim, cfg.past_len

    @pl.when(pl.program_id(2) == 0)
    def _():
        k_scr[0:past, :] = ck_ref[...].astype(BF16)
        k_scr[past:, :] = kn_ref[...].astype(BF16)
        v_scr[0:past, 0:hd] = cv_ref[...].astype(BF16)
        v_scr[past:, 0:hd] = vn_ref[...].astype(BF16)
        v_scr[:, hd:] = jnp.ones((v_scr.shape[0], hd), BF16)

    k = k_scr[...]
    v1 = v_scr[...]
    rq = min(cfg.rq, q_ref.shape[0])
    for h in range(cfg.q_per_kv):
        sl = slice(h * hd, (h + 1) * hd)
        for r in range(q_ref.shape[0] // rq):
            rows = slice(r * rq, (r + 1) * rq)
            o = _softmax_attend(q_ref[rows, sl], k, v1)
            o_ref[rows, sl] = (o * g_ref[rows, sl].astype(F32)).astype(BF16)


def _attn_latent(attn_prev, main, kv, cache_k, cache_v, layer, cfg):
    hd, nkv, tq = cfg.head_dim, cfg.n_kv, cfg.tq
    qw = cfg.q_per_kv * hd
    nqb = cfg.dec_seq // tq
    row0 = cfg.p_tokens // tq
    kvrow0 = cfg.p_tokens // cfg.dec_seq
    nkeys = cfg.past_len + cfg.dec_seq

    def q_map(b, h, qi):
        return (row0 + b * nqb + qi, h)

    def g_map(b, h, qi):
        return (row0 + b * nqb + qi, nkv + h)

    cache_spec = pl.BlockSpec((None, None, cfg.past_len, hd), lambda b, h, qi: (b, layer, 0, h))
    return pl.pallas_call(
        functools.partial(_attn_lat_kernel, cfg=cfg),
        grid=(cfg.dec_batch, nkv, nqb),
        in_specs=[
            pl.BlockSpec(memory_space=pl.ANY),
            pl.BlockSpec((tq, qw), q_map),
            cache_spec, cache_spec,
            pl.BlockSpec((cfg.dec_seq, hd), lambda b, h, qi: (kvrow0 + b, h)),
            pl.BlockSpec((cfg.dec_seq, hd), lambda b, h, qi: (kvrow0 + b, nkv + h)),
            pl.BlockSpec((tq, qw), g_map),
        ],
        out_specs=pl.BlockSpec((tq, qw), q_map),
        out_shape=jax.ShapeDtypeStruct((cfg.tokens, cfg.attn_width), BF16),
        scratch_shapes=[pltpu.VMEM((nkeys, hd), BF16), pltpu.VMEM((nkeys, 2 * hd), BF16)],
        input_output_aliases={0: 0},
        compiler_params=_params("parallel", "parallel", "arbitrary"),
        name="attn_latent",
    )(attn_prev, main, cache_k, cache_v, kv, kv, main)


def _ssm_kernel(u_ref, perm_ref, catf_ref, catb_ref, bbt_ref, pa_ref, w2_ref, av_ref,
                pw_ref, dsk_ref, h0_ref, y_ref, fin_ref,
                uf_scr, xcat_scr, u8_scr, m_scr, w1s_scr, s_scr, hin_scr, f_scr, g_scr, *, cfg):
    T, C = SSM_CHUNK, cfg.ssm_group
    tc = T * C
    st = 2 * cfg.ssm_state
    n_k = cfg.n_k
    chain = T * n_k
    nc = cfg.p_tokens // chain
    n_sc = cfg.dec_seq // chain
    gt = LANES // C
    jl_n = LANES // C
    jh_n = T // jl_n
    is_latent = pl.program_id(1) == 1
    y8_scr = xcat_scr
    pitch = chain + SSM_ROW_PAD

    def stage_in(c, carry):
        src = pl.ds(pl.multiple_of(c * chain, chain), chain)
        uf_scr[pl.ds(pl.multiple_of(c * pitch, 8), chain), :] = u_ref[src, :].astype(F32)
        return carry

    lax.fori_loop(0, nc, stage_in, 0)

    def gather(k, carry):
        for j in range(T):
            piece = uf_scr[pl.ds(k * T + j, nc, stride=pitch), :].astype(BF16)
            xcat_scr[j // jl_n, pl.ds(pl.multiple_of(k * nc, nc), nc),
                     (j % jl_n) * LANES:(j % jl_n + 1) * LANES] = piece
        return carry

    lax.fori_loop(0, n_k, gather, 0)
    for jh in range(jh_n):
        t = jnp.dot(xcat_scr[jh], perm_ref[...], preferred_element_type=F32)
        for r in range(gt):
            u8_scr[r, :, jh * LANES:(jh + 1) * LANES] = t[:, r * LANES:(r + 1) * LANES].astype(BF16)

    def cmul(h, hs, c1, c2):
        return h * c1 + hs * c2, hs * c1 - h * c2

    lane = lax.broadcasted_iota(jnp.int32, (C, tc), 1)

    for r in range(gt):
        ktf = jnp.dot(bbt_ref[r, 0], catf_ref[r], preferred_element_type=F32, precision=HIGHEST)
        ktb = jnp.dot(bbt_ref[r, 1], catb_ref[r], preferred_element_type=F32, precision=HIGHEST)
        for j in range(T):
            rf = pltpu.roll(ktf, C * j, 1) if j else ktf
            rf = jnp.where(lane >= C * j, rf, 0.0)
            sh = (T - 1 - j) * C
            rb = pltpu.roll(ktb, tc - sh, 1) if sh else ktb
            rb = jnp.where(lane < tc - sh, rb, 0.0)
            m_scr[j * C:(j + 1) * C, :] = (rf + rb).astype(BF16)
        for d in range(2):
            bb = bbt_ref[r, d]
            bbs = pltpu.roll(bb, st // 2, 1)
            for j in range(T):
                w, ws = cmul(bb, bbs, pa_ref[r, d, j, 0:1, :], pa_ref[r, d, j, 1:2, :])
                w1s_scr[j * C:(j + 1) * C, 2 * d * st:(2 * d + 1) * st] = w.astype(BF16)
                w1s_scr[j * C:(j + 1) * C, (2 * d + 1) * st:(2 * d + 2) * st] = ws.astype(BF16)

        u = u8_scr[r]
        y_intra = jnp.dot(u, m_scr[...], preferred_element_type=F32)
        s_scr[...] = jnp.dot(u, w1s_scr[...], preferred_element_type=F32)
        av = av_ref[r]
        a1 = (av[0:1], av[2:3])
        a2 = (av[1:2], av[3:4])
        b1 = (av[4:5], av[6:7])
        b2 = (av[5:6], av[7:8])

        def level1(k, carry):
            hf, hfs, hb, hbs = carry
            rf_ = pl.ds(pl.multiple_of(k * nc, nc), nc)
            rb_ = pl.ds(pl.multiple_of((n_k - 1 - k) * nc, nc), nc)
            hin_scr[rf_, 0:st] = hf
            hin_scr[rf_, st:2 * st] = hfs
            hin_scr[rb_, 2 * st:3 * st] = hb
            hin_scr[rb_, 3 * st:4 * st] = hbs
            nf, nfs = cmul(hf, hfs, a1[0], a2[0])
            nbk, nbs = cmul(hb, hbs, a1[1], a2[1])
            return (nf + s_scr[rf_, 0:st], nfs + s_scr[rf_, st:2 * st],
                    nbk + s_scr[rb_, 2 * st:3 * st], nbs + s_scr[rb_, 3 * st:4 * st])

        z = jnp.zeros((nc, st), F32)
        ff, ffs, fb, fbs = lax.fori_loop(0, n_k, level1, (z, z, z, z))

        @pl.when(jnp.logical_not(is_latent))
        def _():
            fin_ref[r, 0] = ff
            fin_ref[r, 1] = fb

        @pl.when(is_latent)
        def _():
            for idx, val in enumerate((ff, ffs, fb, fbs)):
                f_scr[idx] = val
            h0 = h0_ref[r]
            gf, gfs, gb, gbs = h0[0], h0[1], h0[2], h0[3]
            for sc in range(n_sc):
                sel = pl.ds(sc, cfg.dec_batch, stride=n_sc)
                g_scr[0, sel, :] = gf
                g_scr[1, sel, :] = gfs
                nf, nfs = cmul(gf, gfs, b1[0], b2[0])
                gf = nf + f_scr[0, sel, :]
                gfs = nfs + f_scr[1, sel, :]
            for sc in range(n_sc - 1, -1, -1):
                sel = pl.ds(sc, cfg.dec_batch, stride=n_sc)
                g_scr[2, sel, :] = gb
                g_scr[3, sel, :] = gbs
                nbk, nbs = cmul(gb, gbs, b1[1], b2[1])
                gb = nbk + f_scr[2, sel, :]
                gbs = nbs + f_scr[3, sel, :]
            gfa, gfsa, gba, gbsa = g_scr[0], g_scr[1], g_scr[2], g_scr[3]
            for k in range(n_k):
                rows_k = slice(k * nc, (k + 1) * nc)
                pf = pw_ref[r, k]
                cf, cfs = cmul(gfa, gfsa, pf[0:1], pf[1:2])
                cb, cbs = cmul(gba, gbsa, pf[2:3], pf[3:4])
                hin_scr[rows_k, 0:st] += cf
                hin_scr[rows_k, st:2 * st] += cfs
                hin_scr[rows_k, 2 * st:3 * st] += cb
                hin_scr[rows_k, 3 * st:4 * st] += cbs

        hsel = jnp.concatenate([hin_scr[:, 0:st], hin_scr[:, 2 * st:3 * st]], axis=1).astype(BF16)
        y = y_intra + jnp.dot(hsel, w2_ref[r], preferred_element_type=F32)
        y = (y + dsk_ref[r] * u.astype(F32)).astype(BF16)
        for jh in range(jh_n):
            y8_scr[jh, :, r * LANES:(r + 1) * LANES] = y[:, jh * LANES:(jh + 1) * LANES]

    for jh in range(jh_n):
        t = lax.dot_general(y8_scr[jh], perm_ref[...], (((1,), (1,)), ((), ())),
                            preferred_element_type=F32)
        for k in range(n_k):
            for jl in range(jl_n):
                uf_scr[pl.ds(k * T + jh * jl_n + jl, nc, stride=pitch), :] = (
                    t[k * nc:(k + 1) * nc, jl * LANES:(jl + 1) * LANES])

    def stage_out(c, carry):
        dst = pl.ds(pl.multiple_of(c * chain, chain), chain)
        y_ref[dst, :] = uf_scr[pl.ds(pl.multiple_of(c * pitch, 8), chain), :].astype(BF16)
        return carry

    lax.fori_loop(0, nc, stage_out, 0)


def _ssm_scan(main, ssm_w, h0, layer, cfg):
    perm, catf, catb, bbt, pa, w2, av, pw, dsk = ssm_w
    T, C = SSM_CHUNK, cfg.ssm_group
    tc = T * C
    st = 2 * cfg.ssm_state
    gt = LANES // C
    nq = cfg.ssm_width // LANES
    chain = T * cfg.n_k
    half = cfg.p_tokens
    assert cfg.seq == chain and cfg.dec_seq % chain == 0 and T % gt == 0 and cfg.s_tokens == half
    nc = half // chain
    rows = nc * cfg.n_k
    pw_n = gt * LANES
    ucol = 2 * cfg.attn_width // LANES

    def wspec(*shape):
        nd = len(shape)
        return pl.BlockSpec((None, gt) + shape, lambda q, s: (layer, q) + (0,) * nd)

    return pl.pallas_call(
        functools.partial(_ssm_kernel, cfg=cfg),
        grid=(nq, 2),
        in_specs=[
            pl.BlockSpec((half, LANES), lambda q, s: (s, ucol + q)),
            pl.BlockSpec((pw_n, pw_n), lambda q, s: (0, 0)),
            wspec(st, tc), wspec(st, tc), wspec(2, C, st), wspec(2, T, 2, st), wspec(2 * st, tc),
            wspec(8, st), wspec(cfg.n_k, 4, st), wspec(1, tc), wspec(4, cfg.dec_batch, st),
        ],
        out_specs=[
            pl.BlockSpec((half, LANES), lambda q, s: (s, q)),
            pl.BlockSpec((gt, 2, cfg.batch, st), lambda q, s: (q, 0, 0, 0)),
        ],
        out_shape=[
            jax.ShapeDtypeStruct((cfg.tokens, cfg.ssm_width), BF16),
            jax.ShapeDtypeStruct((cfg.n_groups, 2, cfg.batch, st), F32),
        ],
        scratch_shapes=[
            pltpu.VMEM((nc * (chain + SSM_ROW_PAD), LANES), F32),
            pltpu.VMEM((T // gt, rows, pw_n), BF16),
            pltpu.VMEM((gt, rows, tc), BF16),
            pltpu.VMEM((tc, tc), BF16),
            pltpu.VMEM((tc, 4 * st), BF16),
            pltpu.VMEM((rows, 4 * st), F32),
            pltpu.VMEM((rows, 4 * st), F32),
            pltpu.VMEM((4, nc, st), F32),
            pltpu.VMEM((4, nc, st), F32),
        ],
        compiler_params=_params("parallel", "arbitrary"),
        name="ssm_scan",
    )(main, perm, catf, catb, bbt, pa, w2, av, pw, dsk, h0)


def _ssm_weights(lam_re, lam_im, log_step, b_re, b_im, c_re, c_im, d_skip, cfg):
    T, C, P, G, nk = SSM_CHUNK, cfg.ssm_group, cfg.ssm_state, cfg.n_groups, cfg.n_k
    lr, li = lam_re.astype(F32), lam_im.astype(F32)
    dt = jnp.exp(log_step.astype(F32))[..., None]

    def powers(js):
        jj = jnp.asarray(js, F32)[:, None]
        mag = jnp.exp(lr[..., None, :] * dt[..., None, :] * jj)
        ang = li[..., None, :] * dt[..., None, :] * jj
        return mag * jnp.cos(ang), mag * jnp.sin(ang)

    pr, pi = powers(range(T + 1))
    ab_re, ab_im = pr[..., 1, :], pi[..., 1, :]
    nr, ni = ab_re - 1.0, ab_im
    den = lr * lr + li * li
    f_re = (nr * lr + ni * li) / den
    f_im = (ni * lr - nr * li) / den
    br, bi = b_re.astype(F32), b_im.astype(F32)
    bb_re = f_re[..., None] * br - f_im[..., None] * bi
    bb_im = f_re[..., None] * bi + f_im[..., None] * br
    cr, ci = c_re.astype(F32), c_im.astype(F32)
    ca_re = cr[..., None, :, :] * pr[..., :, None, :] - ci[..., None, :, :] * pi[..., :, None, :]
    ca_im = cr[..., None, :, :] * pi[..., :, None, :] + ci[..., None, :, :] * pr[..., :, None, :]

    def state_out(d, lo, rev):
        def pick(x):
            x = x[:, d, :, lo:lo + T]
            x = jnp.flip(x, axis=2) if rev else x
            return x.transpose(0, 1, 4, 2, 3).reshape(-1, G, P, T * C)
        return jnp.concatenate([pick(ca_re), -pick(ca_im)], axis=2)

    catf = state_out(0, 0, False)
    catb = state_out(1, 0, True)
    w2 = jnp.concatenate([state_out(0, 1, False), state_out(1, 1, True)], axis=2).astype(BF16)
    bbt = jnp.concatenate([bb_re, bb_im], axis=3).transpose(0, 2, 1, 4, 3)

    def lanes(re, im):
        return jnp.concatenate([re, re], axis=-1), jnp.concatenate([-im, im], axis=-1)

    paf = lanes(jnp.flip(pr[:, 0, :, 0:T], axis=2), jnp.flip(pi[:, 0, :, 0:T], axis=2))
    pab = lanes(pr[:, 1, :, 0:T], pi[:, 1, :, 0:T])
    pa = jnp.stack([jnp.stack(paf, axis=3), jnp.stack(pab, axis=3)], axis=2)

    qr, qi = powers([T * k for k in range(nk + 1)])
    rows = []
    for src_r, src_i, idx in ((pr, pi, T), (qr, qi, nk)):
        for d in range(2):
            rows.extend(lanes(src_r[:, d, :, idx], src_i[:, d, :, idx]))
    av = jnp.stack(rows, axis=2)
    p1f, p2f = lanes(qr[:, 0, :, 0:nk], qi[:, 0, :, 0:nk])
    p1b, p2b = lanes(jnp.flip(qr[:, 1, :, 0:nk], axis=2), jnp.flip(qi[:, 1, :, 0:nk], axis=2))
    pw = jnp.stack([p1f, p2f, p1b, p2b], axis=3)
    dsk = jnp.tile(d_skip.astype(F32).reshape(-1, G, 1, C), (1, 1, 1, T))

    gt = LANES // C
    n = gt * LANES
    src = jnp.arange(n)
    jl, r, c = src // LANES, (src % LANES) // C, src % C
    dst = r * LANES + jl * C + c
    perm = (dst[:, None] == jnp.arange(n)[None, :]).astype(BF16)
    return perm, catf, catb, bbt, pa, w2, av, pw, dsk


def _glu_kernel(y_ref, w_ref, g_ref, o_ref, *, cfg):
    z = jnp.dot(y_ref[...], w_ref[...], preferred_element_type=F32)
    sw = cfg.ssm_width
    a, g = z[:, :sw], z[:, sw:]
    o_ref[...] = (a / (1.0 + jnp.exp(-g)) * g_ref[...].astype(F32)).astype(BF16)


def _ssm_glu(y, w_glu_b, main, layer, cfg):
    tm, sw = cfg.tm_post, cfg.ssm_width
    gcol = (2 * cfg.attn_width + sw) // sw
    assert (2 * cfg.attn_width) % sw == 0
    return pl.pallas_call(
        functools.partial(_glu_kernel, cfg=cfg),
        grid=(cfg.tokens // tm,),
        in_specs=[
            pl.BlockSpec((tm, sw), lambda i: (i, 0)),
            pl.BlockSpec((None, sw, 2 * sw), lambda i: (layer, 0, 0)),
            pl.BlockSpec((tm, sw), lambda i: (i, gcol)),
        ],
        out_specs=pl.BlockSpec((tm, sw), lambda i: (i, 0)),
        out_shape=jax.ShapeDtypeStruct((cfg.tokens, sw), BF16),
        compiler_params=_params("parallel"),
        name="ssm_glu",
    )(y, w_glu_b, main)


def _fourier_kernel(prev_ref, x_ref, dl_ref, cs_ref, w_ref, g_ref, o_ref, z_scr, *, cfg, length):
    del prev_ref
    fg = cfg.fft_group
    rc = min(cfg.rc_fft, length)

    @pl.when(pl.program_id(1) == 0)
    def _():
        def body(ci, carry):
            rows = pl.ds(pl.multiple_of(ci * rc, rc), rc)
            rows2 = pl.ds(pl.multiple_of(length + ci * rc, rc), rc)
            for g in range(cfg.fft_width // fg):
                cols = slice(g * fg, (g + 1) * fg)
                t = jnp.dot(x_ref[rows, cols], cs_ref[...], preferred_element_type=F32)
                z_scr[rows, cols] = t[:, :fg].astype(BF16)
                z_scr[rows2, cols] = t[:, fg:].astype(BF16)
            return carry

        lax.fori_loop(0, length // rc, body, 0)

    mixed = jnp.dot(dl_ref[...], z_scr[...], preferred_element_type=F32).astype(BF16)
    four = jnp.dot(mixed, w_ref[...], preferred_element_type=F32)
    o_ref[...] = (four * g_ref[...].astype(F32)).astype(BF16)


def _fourier(prev, main, dl, cs, w_fft_b, layer, cfg, *, length, nbatch, row0):
    fw = cfg.fft_width
    tm = min(cfg.tm_fft, length)
    nr = length // tm
    xcol = (2 * cfg.attn_width + 2 * cfg.ssm_width) // fw
    assert (2 * cfg.attn_width + 2 * cfg.ssm_width) % fw == 0 and row0 % length == 0
    b0 = row0 // length
    t0 = row0 // tm

    def tile_map(b, r):
        return (t0 + b * nr + r, 0)

    return pl.pallas_call(
        functools.partial(_fourier_kernel, cfg=cfg, length=length),
        grid=(nbatch, nr),
        in_specs=[
            pl.BlockSpec(memory_space=pl.ANY),
            pl.BlockSpec((length, fw), lambda b, r: (b0 + b, xcol)),
            pl.BlockSpec((tm, 2 * length), lambda b, r: (r, 0)),
            pl.BlockSpec((cfg.fft_group, 2 * cfg.fft_group), lambda b, r: (0, 0)),
            pl.BlockSpec((None, fw, fw), lambda b, r: (layer, 0, 0)),
            pl.BlockSpec((tm, fw), lambda b, r: (t0 + b * nr + r, xcol + 1)),
        ],
        out_specs=pl.BlockSpec((tm, fw), tile_map),
        out_shape=jax.ShapeDtypeStruct((cfg.tokens, fw), BF16),
        scratch_shapes=[pltpu.VMEM((2 * length, fw), BF16)],
        input_output_aliases={0: 0},
        compiler_params=_params("parallel", "arbitrary"),
        name=f"fourier_{length}",
    )(prev, main, dl, cs, w_fft_b, main)


def _dft_cos_sin(n):
    k = jnp.arange(n, dtype=jnp.int32)
    w = 2.0 * math.pi / n

    ang = ((k[:, None] * k[None, :]) % n).astype(F32) * w
    scale = n ** -0.5
    return jnp.cos(ang) * scale, jnp.sin(ang) * scale


def _dft_position_table(n):
    s = 1
    while s * s < n:
        s *= 2
    if n <= 512 or n % s:
        c, sn = _dft_cos_sin(n)
        return jnp.concatenate([c, -sn], axis=1).astype(BF16)
    k = jnp.arange(n, dtype=jnp.int32)
    w = 2.0 * math.pi / n

    def cs(j):
        ang = ((j[:, None] * k[None, :]) % n).astype(F32) * w
        return jnp.cos(ang), jnp.sin(ang)

    c1, s1 = cs(jnp.arange(n // s, dtype=jnp.int32) * s)
    c0, s0 = cs(jnp.arange(s, dtype=jnp.int32))
    a1 = jnp.concatenate([c1, -s1], axis=1)[:, None, :]
    a2 = jnp.concatenate([s1, c1], axis=1)[:, None, :]
    b0 = jnp.concatenate([c0, c0], axis=1)[None]
    b1 = jnp.concatenate([s0, s0], axis=1)[None]
    return ((a1 * b0 - a2 * b1) * n ** -0.5).reshape(n, 2 * n).astype(BF16)


def _out_proj_kernel(a_ref, s_ref, f_ref, w_ref, xa_ref, xb_ref, mod_ref, o_ref, *, cfg, npb):
    aw, sw = cfg.attn_width, cfg.ssm_width
    acc = jnp.dot(a_ref[...], w_ref[0:aw, :], preferred_element_type=F32)
    acc += jnp.dot(s_ref[...], w_ref[aw:aw + sw, :], preferred_element_type=F32)
    acc += jnp.dot(f_ref[...], w_ref[aw + sw:, :], preferred_element_type=F32)
    x = jnp.where(pl.program_id(1) < npb, xa_ref[...], xb_ref[...])
    o_ref[...] = x + mod_ref[2:3, :] * acc


def _out_proj(attn, ssm, four, w_out_b, xa, xb, xb_row0, mod, layer, cfg):
    tm, tn, d = cfg.tm_out, cfg.tn_out, cfg.d_model
    npb = cfg.p_tokens // tm
    assert xb_row0 % tm == 0
    return pl.pallas_call(
        functools.partial(_out_proj_kernel, cfg=cfg, npb=npb),
        grid=(d // tn, cfg.tokens // tm),
        in_specs=[
            pl.BlockSpec((tm, cfg.attn_width), lambda j, i: (i, 0)),
            pl.BlockSpec((tm, cfg.ssm_width), lambda j, i: (i, 0)),
            pl.BlockSpec((tm, cfg.fft_width), lambda j, i: (i, 0)),
            pl.BlockSpec((None, d, tn), lambda j, i: (layer, 0, j)),
            pl.BlockSpec((tm, tn), lambda j, i: (jnp.minimum(i, npb - 1), j)),
            pl.BlockSpec((tm, tn), lambda j, i: (jnp.maximum(i, npb) - npb + xb_row0 // tm, j)),
            pl.BlockSpec((None, None, 3, tn), lambda j, i: (layer, _row_group(i, cfg, tm), 0, j)),
        ],
        out_specs=pl.BlockSpec((tm, tn), lambda j, i: (i, j)),
        out_shape=jax.ShapeDtypeStruct((cfg.tokens, d), F32),
        compiler_params=_params("parallel", "arbitrary"),
        name="out_proj",
    )(attn, ssm, four, w_out_b, xa, xb, mod)


def _final_norm_kernel(x_ref, g_ref, o_ref):
    x = x_ref[...]
    ms = jnp.mean(x * x, axis=-1, keepdims=True)
    o_ref[...] = x * lax.rsqrt(ms + NORM_EPS) * g_ref[...]


def _final_norm(x, g, cfg, *, row0, nrows):
    tm, d = 256, cfg.d_model
    tm = min(tm, nrows)
    assert row0 % tm == 0
    return pl.pallas_call(
        _final_norm_kernel,
        grid=(nrows // tm,),
        in_specs=[pl.BlockSpec((tm, d), lambda i: (row0 // tm + i, 0)),
                  pl.BlockSpec((1, d), lambda i: (0, 0))],
        out_specs=pl.BlockSpec((tm, d), lambda i: (i, 0)),
        out_shape=jax.ShapeDtypeStruct((nrows, d), F32),
        compiler_params=_params("parallel"),
        name="final_norm",
    )(x, g)


def _rope_tables(cfg):
    hd = cfg.head_dim
    pairs = hd // 4
    t = jnp.arange(cfg.dec_seq)
    inv = ROPE_THETA ** (-jnp.arange(pairs, dtype=F32) / pairs)
    row_ang = (t // cfg.grid_w).astype(F32)[:, None] * inv[None, :]
    col_ang = (t % cfg.grid_w).astype(F32)[:, None] * inv[None, :]
    zeros = jnp.zeros_like(row_ang)
    cos = jnp.concatenate([jnp.cos(row_ang)] * 2 + [jnp.cos(col_ang)] * 2, axis=1)
    sa = jnp.concatenate([-jnp.sin(row_ang), zeros, -jnp.sin(col_ang), zeros], axis=1)
    sb = jnp.concatenate([zeros, jnp.sin(row_ang), zeros, jnp.sin(col_ang)], axis=1)
    ident = jnp.zeros((cfg.tm_in, hd), F32)
    return (jnp.concatenate([ident + 1.0, cos], axis=0),
            jnp.concatenate([ident, sa], axis=0),
            jnp.concatenate([ident, sb], axis=0))


def _forward(cfg, x_prompt, x_sample, cache_k, cache_v, state_fwd_re, state_fwd_im, state_bwd_re,
             state_bwd_im, c, c_ctx, norm_g, w_mod, b_mod, w_in, q_norm, k_norm, lam_re, lam_im,
             log_step, b_re, b_im, c_re, c_im, d_skip, w_glu, w_fft, w_out, final_norm_g):
    d, kvw = cfg.d_model, cfg.kv_width
    P = cfg.ssm_state

    w_in_b = w_in.astype(BF16)
    w_out_b = w_out.astype(BF16)
    w_glu_b = w_glu.astype(BF16)
    w_fft_b = w_fft.astype(BF16)
    gains = jnp.stack([k_norm, q_norm], axis=1).astype(F32)
    norm_g3 = norm_g.astype(F32)[:, None, :]
    ssm_w = _ssm_weights(lam_re, lam_im, log_step, b_re, b_im, c_re, c_im, d_skip, cfg)
    rope_tabs = _rope_tables(cfg)

    cc, sc = _dft_cos_sin(cfg.fft_group)
    cs = jnp.concatenate([cc, sc], axis=1).astype(BF16)
    dls = {}
    for length in {cfg.seq, cfg.dec_seq}:
        dls[length] = _dft_position_table(length)

    def st(re, im):
        re = re.astype(F32).transpose(1, 2, 0, 3)
        im = im.astype(F32).transpose(1, 2, 0, 3)
        return jnp.concatenate([re, im], axis=-1), jnp.concatenate([im, re], axis=-1)

    h0 = jnp.stack(st(state_fwd_re, state_fwd_im) + st(state_bwd_re, state_bwd_im), axis=2)

    nrow = 1 + cfg.dec_batch
    cvecs = jnp.concatenate([c_ctx[None, :], c], axis=0).astype(F32)
    cvecs = jnp.pad(cvecs, ((0, -nrow % 8), (0, 0)))
    mod = _modulation(cvecs, w_mod, b_mod, cfg).reshape(cfg.depth, cvecs.shape[0], 3, d)

    cache_k4 = cache_k.reshape(cfg.dec_batch, cfg.depth, cfg.past_len, kvw)
    cache_v4 = cache_v.reshape(cfg.dec_batch, cfg.depth, cfg.past_len, kvw)

    xs = (x_prompt.reshape(cfg.p_tokens, d), x_sample.reshape(cfg.s_tokens, d), 0)
    attn = jnp.zeros((cfg.tokens, cfg.attn_width), BF16)
    four = jnp.zeros((cfg.tokens, cfg.fft_width), BF16)
    ks, vs, fins = [], [], []
    for l in range(cfg.depth):
        kv, main = _in_proj(*xs, mod, norm_g3, w_in_b, gains, rope_tabs, l, cfg)
        attn = _attn_context(attn, main, kv, cfg)
        attn = _attn_latent(attn, main, kv, cache_k4, cache_v4, l, cfg)
        y, fin = _ssm_scan(main, ssm_w, h0, l, cfg)
        ssm = _ssm_glu(y, w_glu_b, main, l, cfg)
        four = _fourier(four, main, dls[cfg.seq], cs, w_fft_b, l, cfg,
                        length=cfg.seq, nbatch=cfg.batch, row0=0)
        four = _fourier(four, main, dls[cfg.dec_seq], cs, w_fft_b, l, cfg,
                        length=cfg.dec_seq, nbatch=cfg.dec_batch, row0=cfg.p_tokens)
        x = _out_proj(attn, ssm, four, w_out_b, *xs, mod, l, cfg)
        xs = (x, x, cfg.p_tokens)
        ks.append(kv[:cfg.p_tokens, :kvw].reshape(cfg.batch, cfg.seq, cfg.n_kv, cfg.head_dim))
        vs.append(kv[:cfg.p_tokens, kvw:].reshape(cfg.batch, cfg.seq, cfg.n_kv, cfg.head_dim))
        fins.append(fin)

    g = final_norm_g[None, :].astype(F32)
    y_prompt = _final_norm(x, g, cfg, row0=0, nrows=cfg.p_tokens).reshape(cfg.batch, cfg.seq, d)
    y_sample = _final_norm(x, g, cfg, row0=cfg.p_tokens, nrows=cfg.s_tokens).reshape(
        cfg.dec_batch, cfg.dec_seq, d)
    fin = jnp.stack(fins, axis=0)
    fin = fin.transpose(3, 0, 1, 2, 4)
    return (y_prompt, y_sample, jnp.stack(ks, axis=1), jnp.stack(vs, axis=1),
            fin[:, :, :, 0, :P], fin[:, :, :, 0, P:], fin[:, :, :, 1, :P], fin[:, :, :, 1, P:])


def kernel(x_prompt, x_sample, cache_k, cache_v, state_fwd_re, state_fwd_im, state_bwd_re, state_bwd_im,
           c, c_ctx, norm_g, w_mod, b_mod, w_in, q_norm, k_norm, lam_re, lam_im, log_step,
           b_re, b_im, c_re, c_im, d_skip, w_glu, w_fft, w_out, final_norm_g):
    return _forward(Cfg(), x_prompt, x_sample, cache_k, cache_v, state_fwd_re, state_fwd_im,
                    state_bwd_re, state_bwd_im, c, c_ctx, norm_g, w_mod, b_mod, w_in, q_norm, k_norm,
                    lam_re, lam_im, log_step, b_re, b_im, c_re, c_im, d_skip, w_glu, w_fft, w_out,
                    final_norm_g)
```

```python
import functools
import math
from typing import NamedTuple

import jax
import jax.numpy as jnp
from jax import lax
from jax.experimental import pallas as pl
from jax.experimental.pallas import tpu as pltpu

F32 = jnp.float32
BF16 = jnp.bfloat16
NORM_EPS = 1e-6
ROPE_THETA = 10000.0
LANES = 128
SSM_CHUNK = 16
SSM_ROW_PAD = 8
VMEM_LIMIT = 56 * 1024 * 1024
LOG2E = 1.4426950408889634
HIGHEST = lax.Precision.HIGHEST


class Cfg(NamedTuple):
    d_model: int = 4096
    batch: int = 32
    seq: int = 256
    depth: int = 4
    dec_batch: int = 2
    dec_seq: int = 4096
    past_len: int = 512
    grid_w: int = 64
    head_dim: int = 128
    n_heads: int = 16
    n_kv: int = 4
    ssm_width: int = 1024
    ssm_group: int = 16
    ssm_state: int = 64
    fft_width: int = 1024
    fft_group: int = 256
    tm_in: int = 512
    tm_out: int = 512
    tn_out: int = 1024
    tn_mod: int = 512
    tq: int = 1024
    rq: int = 128
    tm_fft: int = 256
    rc_fft: int = 512
    tm_post: int = 1024
    n_k: int = 16

    @property
    def attn_width(self):
        return self.n_heads * self.head_dim

    @property
    def kv_width(self):
        return self.n_kv * self.head_dim

    @property
    def q_per_kv(self):
        return self.n_heads // self.n_kv

    @property
    def n_groups(self):
        return self.ssm_width // self.ssm_group

    @property
    def tn_in(self):
        return 2 * self.kv_width

    @property
    def main_width(self):
        return 2 * self.attn_width + 2 * self.ssm_width + 2 * self.fft_width

    @property
    def p_tokens(self):
        return self.batch * self.seq

    @property
    def s_tokens(self):
        return self.dec_batch * self.dec_seq

    @property
    def tokens(self):
        return self.p_tokens + self.s_tokens


def _params(*sem):
    return pltpu.CompilerParams(dimension_semantics=sem, vmem_limit_bytes=VMEM_LIMIT)


def _silu(x):
    return x / (1.0 + jnp.exp(-x))


def _row_group(i, cfg, tm):
    npb = cfg.p_tokens // tm
    per = cfg.dec_seq // tm
    return jnp.where(i < npb, 0, 1 + (i - npb) // per)


def _mod_kernel(c_ref, w_ref, b_ref, o_ref):
    s = _silu(c_ref[...]).astype(BF16)
    o_ref[0] = jnp.dot(s, w_ref[0].astype(BF16), preferred_element_type=F32) + b_ref[0]


def _modulation(cvecs, w_mod, b_mod, cfg):
    d, n = cfg.d_model, 3 * cfg.d_model
    nrow = cvecs.shape[0]
    return pl.pallas_call(
        _mod_kernel,
        grid=(cfg.depth, n // cfg.tn_mod),
        in_specs=[
            pl.BlockSpec((nrow, d), lambda l, j: (0, 0)),
            pl.BlockSpec((1, d, cfg.tn_mod), lambda l, j: (l, 0, j)),
            pl.BlockSpec((1, 1, cfg.tn_mod), lambda l, j: (l, 0, j)),
        ],
        out_specs=pl.BlockSpec((1, nrow, cfg.tn_mod), lambda l, j: (l, 0, j)),
        out_shape=jax.ShapeDtypeStruct((cfg.depth, nrow, n), F32),
        compiler_params=_params("parallel", "parallel"),
        name="modulation",
    )(cvecs, w_mod, b_mod.reshape(cfg.depth, 1, n))


def _in_proj_kernel(xa_hbm, xb_hbm, mod_ref, ng_ref, w_ref, gains_ref, cos_ref, sa_ref, sb_ref,
                    kv_ref, main_ref, h_scr, x_scr, x_sem, *, cfg, kinds, npb, nb, xb_row0):
    i = pl.program_id(0)
    j = pl.program_id(1)
    tm = h_scr.shape[0]
    hd = cfg.head_dim
    tn = w_ref.shape[1]
    rc = min(tm, 16)

    def x_copy(src_hbm, row0):
        return pltpu.make_async_copy(src_hbm.at[pl.ds(pl.multiple_of(row0, tm), tm), :], x_scr, x_sem)

    def start_x(blk):
        @pl.when(blk < npb)
        def _():
            x_copy(xa_hbm, blk * tm).start()

        @pl.when(blk >= npb)
        def _():
            x_copy(xb_hbm, (blk - npb) * tm + xb_row0).start()

    @pl.when(j == 0)
    def _():
        @pl.when(i == 0)
        def _():
            start_x(i)

        x_copy(xa_hbm, 0).wait()
        shift = mod_ref[0:1, :]
        scale1 = 1.0 + mod_ref[1:2, :]
        g = ng_ref[...]

        def body(r, carry):
            rows = pl.ds(pl.multiple_of(r * rc, rc), rc)
            x = x_scr[rows, :]
            ms = jnp.mean(x * x, axis=-1, keepdims=True)
            y = x * lax.rsqrt(ms + NORM_EPS) * g
            h_scr[rows, :] = (y * scale1 + shift).astype(BF16)
            return carry

        lax.fori_loop(0, tm // rc, body, 0, unroll=min(4, tm // rc))

    @pl.when((j == 1) & (i + 1 < nb))
    def _():
        start_x(i + 1)

    sub = min(2 * hd, tn)

    def sub_dots():
        for s in range(tn // sub):
            yield s * sub, jnp.dot(h_scr[...], w_ref[:, s * sub:(s + 1) * sub],
                                   preferred_element_type=F32)

    def head_norm_rope(a, gain):
        ms = jnp.mean(a * a, axis=-1, keepdims=True)
        y = a * lax.rsqrt(ms + NORM_EPS) * gain
        return (y * cos_ref[...] + pltpu.roll(y, hd - hd // 4, 1) * sa_ref[...]
                + pltpu.roll(y, hd // 4, 1) * sb_ref[...])

    def tile(kind):
        for c0, acc in sub_dots():
            if kind in ("q", "kv"):
                for h in range(sub // hd):
                    sl = slice(c0 + h * hd, c0 + (h + 1) * hd)
                    a = acc[:, h * hd:(h + 1) * hd]
                    if kind == "q":
                        y = head_norm_rope(a, gains_ref[1:2, :]) * (hd ** -0.5 * LOG2E)
                        main_ref[:, sl] = y.astype(BF16)
                    elif c0 + h * hd < cfg.kv_width:
                        kv_ref[:, sl] = head_norm_rope(a, gains_ref[0:1, :])
                    else:
                        kv_ref[:, sl] = a
            elif kind == "silu":
                main_ref[:, c0:c0 + sub] = _silu(acc).astype(BF16)
            else:
                main_ref[:, c0:c0 + sub] = acc.astype(BF16)

    for kind in sorted(set(kinds)):
        pred = functools.reduce(lambda a, b: a | b, [j == t for t, k in enumerate(kinds) if k == kind])
        pl.when(pred)(functools.partial(tile, kind))


def _tile_kinds(cfg):
    tn = cfg.tn_in
    widths = [("q", cfg.attn_width), ("kv", 2 * cfg.kv_width), ("silu", cfg.attn_width),
              ("plain", cfg.ssm_width), ("silu", cfg.ssm_width), ("plain", cfg.fft_width),
              ("silu", cfg.fft_width)]
    kinds = []
    for name, w in widths:
        assert w % tn == 0
        kinds += [name] * (w // tn)
    return tuple(kinds)


def _in_proj(xa, xb, xb_row0, mod, norm_g, w_in_b, gains, rope_tabs, layer, cfg):
    tm, tn, d = cfg.tm_in, cfg.tn_in, cfg.d_model
    kinds = _tile_kinds(cfg)
    nj = len(kinds)
    kv0 = kinds.index("kv")
    nkv = 2 * cfg.kv_width // tn
    npb = cfg.p_tokens // tm
    per = cfg.dec_seq // tm
    cos_t, sa_t, sb_t = rope_tabs
    nb = cfg.tokens // tm
    assert xb_row0 % tm == 0 and cfg.p_tokens % tm == 0 and cfg.dec_seq % tm == 0
    tab_spec = pl.BlockSpec((tm, cfg.head_dim),
                            lambda i, j: (jnp.where(i < npb, 0, 1 + (i - npb) % per), 0))
    return pl.pallas_call(
        functools.partial(_in_proj_kernel, cfg=cfg, kinds=kinds, npb=npb, nb=nb, xb_row0=xb_row0),
        grid=(nb, nj),
        in_specs=[
            pl.BlockSpec(memory_space=pl.ANY),
            pl.BlockSpec(memory_space=pl.ANY),
            pl.BlockSpec((None, None, 3, d), lambda i, j: (layer, _row_group(i, cfg, tm), 0, 0)),
            pl.BlockSpec((None, 1, d), lambda i, j: (layer, 0, 0)),
            pl.BlockSpec((None, d, tn), lambda i, j: (layer, 0, j)),
            pl.BlockSpec((None, 2, cfg.head_dim), lambda i, j: (layer, 0, 0)),
            tab_spec, tab_spec, tab_spec,
        ],
        out_specs=[
            pl.BlockSpec((tm, tn), lambda i, j: (i, jnp.clip(j - kv0, 0, nkv - 1))),
            pl.BlockSpec((tm, tn), lambda i, j: (i, jnp.where(j < kv0, j, jnp.maximum(j - nkv, kv0 - 1)))),
        ],
        out_shape=[
            jax.ShapeDtypeStruct((cfg.tokens, 2 * cfg.kv_width), F32),
            jax.ShapeDtypeStruct((cfg.tokens, cfg.main_width), BF16),
        ],
        scratch_shapes=[pltpu.VMEM((tm, d), BF16), pltpu.VMEM((tm, d), F32),
                        pltpu.SemaphoreType.DMA],
        compiler_params=_params("arbitrary", "arbitrary"),
        name="in_proj",
    )(xa, xb, mod, norm_g, w_in_b, gains, cos_t, sa_t, sb_t)


def _softmax_attend(q, k, v1):
    hd = q.shape[1]
    s = lax.dot_general(q, k, (((1,), (1,)), ((), ())), preferred_element_type=F32)
    m = jnp.max(s, axis=-1, keepdims=True)
    p = jnp.exp2(s - m).astype(BF16)
    o = jnp.dot(p, v1, preferred_element_type=F32)
    return o[:, :hd] / o[:, hd:]


def _attn_ctx_kernel(prev_ref, q_ref, kv_ref, g_ref, o_ref, *, cfg):
    del prev_ref
    hd, kvw = cfg.head_dim, cfg.kv_width
    for kh in range(cfg.n_kv):
        k = kv_ref[:, kh * hd:(kh + 1) * hd].astype(BF16)
        v = kv_ref[:, kvw + kh * hd:kvw + (kh + 1) * hd].astype(BF16)
        v1 = jnp.concatenate([v, jnp.ones_like(v)], axis=1)
        for h in range(kh * cfg.q_per_kv, (kh + 1) * cfg.q_per_kv):
            sl = slice(h * hd, (h + 1) * hd)
            o = _softmax_attend(q_ref[:, sl], k, v1)
            o_ref[:, sl] = (o * g_ref[:, sl].astype(F32)).astype(BF16)


def _attn_context(attn_buf, main, kv, cfg):
    aw = cfg.attn_width
    return pl.pallas_call(
        functools.partial(_attn_ctx_kernel, cfg=cfg),
        grid=(cfg.batch,),
        in_specs=[
            pl.BlockSpec(memory_space=pl.ANY),
            pl.BlockSpec((cfg.seq, aw), lambda b: (b, 0)),
            pl.BlockSpec((cfg.seq, 2 * cfg.kv_width), lambda b: (b, 0)),
            pl.BlockSpec((cfg.seq, aw), lambda b: (b, 1)),
        ],
        out_specs=pl.BlockSpec((cfg.seq, aw), lambda b: (b, 0)),
        out_shape=jax.ShapeDtypeStruct((cfg.tokens, aw), BF16),
        input_output_aliases={0: 0},
        compiler_params=_params("parallel"),
        name="attn_context",
    )(attn_buf, main, kv, main)


def _attn_lat_kernel(prev_ref, q_ref, ck_ref, cv_ref, kn_ref, vn_ref, g_ref, o_ref,
                     k_scr, v_scr, *, cfg):
    del prev_ref
    hd, past = cfg.head_dim, cfg.past_len

    @pl.when(pl.program_id(2) == 0)
    def _():
        k_scr[0:past, :] = ck_ref[...].astype(BF16)
        k_scr[past:, :] = kn_ref[...].astype(BF16)
        v_scr[0:past, 0:hd] = cv_ref[...].astype(BF16)
        v_scr[past:, 0:hd] = vn_ref[...].astype(BF16)
        v_scr[:, hd:] = jnp.ones((v_scr.shape[0], hd), BF16)

    k = k_scr[...]
    v1 = v_scr[...]
    rq = min(cfg.rq, q_ref.shape[0])
    for h in range(cfg.q_per_kv):
        sl = slice(h * hd, (h + 1) * hd)
        for r in range(q_ref.shape[0] // rq):
            rows = slice(r * rq, (r + 1) * rq)
            o = _softmax_attend(q_ref[rows, sl], k, v1)
            o_ref[rows, sl] = (o * g_ref[rows, sl].astype(F32)).astype(BF16)


def _attn_latent(attn_prev, main, kv, cache_k, cache_v, layer, cfg):
    hd, nkv, tq = cfg.head_dim, cfg.n_kv, cfg.tq
    qw = cfg.q_per_kv * hd
    nqb = cfg.dec_seq // tq
    row0 = cfg.p_tokens // tq
    kvrow0 = cfg.p_tokens // cfg.dec_seq
    nkeys = cfg.past_len + cfg.dec_seq

    def q_map(b, h, qi):
        return (row0 + b * nqb + qi, h)

    def g_map(b, h, qi):
        return (row0 + b * nqb + qi, nkv + h)

    cache_spec = pl.BlockSpec((None, None, cfg.past_len, hd), lambda b, h, qi: (b, layer, 0, h))
    return pl.pallas_call(
        functools.partial(_attn_lat_kernel, cfg=cfg),
        grid=(cfg.dec_batch, nkv, nqb),
        in_specs=[
            pl.BlockSpec(memory_space=pl.ANY),
            pl.BlockSpec((tq, qw), q_map),
            cache_spec, cache_spec,
            pl.BlockSpec((cfg.dec_seq, hd), lambda b, h, qi: (kvrow0 + b, h)),
            pl.BlockSpec((cfg.dec_seq, hd), lambda b, h, qi: (kvrow0 + b, nkv + h)),
            pl.BlockSpec((tq, qw), g_map),
        ],
        out_specs=pl.BlockSpec((tq, qw), q_map),
        out_shape=jax.ShapeDtypeStruct((cfg.tokens, cfg.attn_width), BF16),
        scratch_shapes=[pltpu.VMEM((nkeys, hd), BF16), pltpu.VMEM((nkeys, 2 * hd), BF16)],
        input_output_aliases={0: 0},
        compiler_params=_params("parallel", "parallel", "arbitrary"),
        name="attn_latent",
    )(attn_prev, main, cache_k, cache_v, kv, kv, main)


def _ssm_kernel(u_ref, perm_ref, catf_ref, catb_ref, bbt_ref, pa_ref, w2_ref, av_ref,
                pw_ref, dsk_ref, h0_ref, y_ref, fin_ref,
                uf_scr, xcat_scr, u8_scr, m_scr, w1s_scr, s_scr, hin_scr, f_scr, g_scr, *, cfg):
    T, C = SSM_CHUNK, cfg.ssm_group
    tc = T * C
    st = 2 * cfg.ssm_state
    n_k = cfg.n_k
    chain = T * n_k
    nc = cfg.p_tokens // chain
    n_sc = cfg.dec_seq // chain
    gt = LANES // C
    jl_n = LANES // C
    jh_n = T // jl_n
    is_latent = pl.program_id(1) == 1
    y8_scr = xcat_scr
    pitch = chain + SSM_ROW_PAD

    def stage_in(c, carry):
        src = pl.ds(pl.multiple_of(c * chain, chain), chain)
        uf_scr[pl.ds(pl.multiple_of(c * pitch, 8), chain), :] = u_ref[src, :].astype(F32)
        return carry

    lax.fori_loop(0, nc, stage_in, 0)

    def gather(k, carry):
        for j in range(T):
            piece = uf_scr[pl.ds(k * T + j, nc, stride=pitch), :].astype(BF16)
            xcat_scr[j // jl_n, pl.ds(pl.multiple_of(k * nc, nc), nc),
                     (j % jl_n) * LANES:(j % jl_n + 1) * LANES] = piece
        return carry

    lax.fori_loop(0, n_k, gather, 0)
    for jh in range(jh_n):
        t = jnp.dot(xcat_scr[jh], perm_ref[...], preferred_element_type=F32)
        for r in range(gt):
            u8_scr[r, :, jh * LANES:(jh + 1) * LANES] = t[:, r * LANES:(r + 1) * LANES].astype(BF16)

    def cmul(h, hs, c1, c2):
        return h * c1 + hs * c2, hs * c1 - h * c2

    lane = lax.broadcasted_iota(jnp.int32, (C, tc), 1)

    for r in range(gt):
        ktf = jnp.dot(bbt_ref[r, 0], catf_ref[r], preferred_element_type=F32, precision=HIGHEST)
        ktb = jnp.dot(bbt_ref[r, 1], catb_ref[r], preferred_element_type=F32, precision=HIGHEST)
        for j in range(T):
            rf = pltpu.roll(ktf, C * j, 1) if j else ktf
            rf = jnp.where(lane >= C * j, rf, 0.0)
            sh = (T - 1 - j) * C
            rb = pltpu.roll(ktb, tc - sh, 1) if sh else ktb
            rb = jnp.where(lane < tc - sh, rb, 0.0)
            m_scr[j * C:(j + 1) * C, :] = (rf + rb).astype(BF16)
        for d in range(2):
            bb = bbt_ref[r, d]
            bbs = pltpu.roll(bb, st // 2, 1)
            for j in range(T):
                w, ws = cmul(bb, bbs, pa_ref[r, d, j, 0:1, :], pa_ref[r, d, j, 1:2, :])
                w1s_scr[j * C:(j + 1) * C, 2 * d * st:(2 * d + 1) * st] = w.astype(BF16)
                w1s_scr[j * C:(j + 1) * C, (2 * d + 1) * st:(2 * d + 2) * st] = ws.astype(BF16)

        u = u8_scr[r]
        y_intra = jnp.dot(u, m_scr[...], preferred_element_type=F32)
        s_scr[...] = jnp.dot(u, w1s_scr[...], preferred_element_type=F32)
        av = av_ref[r]
        a1 = (av[0:1], av[2:3])
        a2 = (av[1:2], av[3:4])
        b1 = (av[4:5], av[6:7])
        b2 = (av[5:6], av[7:8])

        def level1(k, carry):
            hf, hfs, hb, hbs = carry
            rf_ = pl.ds(pl.multiple_of(k * nc, nc), nc)
            rb_ = pl.ds(pl.multiple_of((n_k - 1 - k) * nc, nc), nc)
            hin_scr[rf_, 0:st] = hf
            hin_scr[rf_, st:2 * st] = hfs
            hin_scr[rb_, 2 * st:3 * st] = hb
            hin_scr[rb_, 3 * st:4 * st] = hbs
            nf, nfs = cmul(hf, hfs, a1[0], a2[0])
            nbk, nbs = cmul(hb, hbs, a1[1], a2[1])
            return (nf + s_scr[rf_, 0:st], nfs + s_scr[rf_, st:2 * st],
                    nbk + s_scr[rb_, 2 * st:3 * st], nbs + s_scr[rb_, 3 * st:4 * st])

        z = jnp.zeros((nc, st), F32)
        ff, ffs, fb, fbs = lax.fori_loop(0, n_k, level1, (z, z, z, z))

        @pl.when(jnp.logical_not(is_latent))
        def _():
            fin_ref[r, 0] = ff
            fin_ref[r, 1] = fb

        @pl.when(is_latent)
        def _():
            for idx, val in enumerate((ff, ffs, fb, fbs)):
                f_scr[idx] = val
            h0 = h0_ref[r]
            gf, gfs, gb, gbs = h0[0], h0[1], h0[2], h0[3]
            for sc in range(n_sc):
                sel = pl.ds(sc, cfg.dec_batch, stride=n_sc)
                g_scr[0, sel, :] = gf
                g_scr[1, sel, :] = gfs
                nf, nfs = cmul(gf, gfs, b1[0], b2[0])
                gf = nf + f_scr[0, sel, :]
                gfs = nfs + f_scr[1, sel, :]
            for sc in range(n_sc - 1, -1, -1):
                sel = pl.ds(sc, cfg.dec_batch, stride=n_sc)
                g_scr[2, sel, :] = gb
                g_scr[3, sel, :] = gbs
                nbk, nbs = cmul(gb, gbs, b1[1], b2[1])
                gb = nbk + f_scr[2, sel, :]
                gbs = nbs + f_scr[3, sel, :]
            gfa, gfsa, gba, gbsa = g_scr[0], g_scr[1], g_scr[2], g_scr[3]
            for k in range(n_k):
                rows_k = slice(k * nc, (k + 1) * nc)
                pf = pw_ref[r, k]
                cf, cfs = cmul(gfa, gfsa, pf[0:1], pf[1:2])
                cb, cbs = cmul(gba, gbsa, pf[2:3], pf[3:4])
                hin_scr[rows_k, 0:st] += cf
                hin_scr[rows_k, st:2 * st] += cfs
                hin_scr[rows_k, 2 * st:3 * st] += cb
                hin_scr[rows_k, 3 * st:4 * st] += cbs

        hsel = jnp.concatenate([hin_scr[:, 0:st], hin_scr[:, 2 * st:3 * st]], axis=1).astype(BF16)
        y = y_intra + jnp.dot(hsel, w2_ref[r], preferred_element_type=F32)
        y = (y + dsk_ref[r] * u.astype(F32)).astype(BF16)
        for jh in range(jh_n):
            y8_scr[jh, :, r * LANES:(r + 1) * LANES] = y[:, jh * LANES:(jh + 1) * LANES]

    for jh in range(jh_n):
        t = lax.dot_general(y8_scr[jh], perm_ref[...], (((1,), (1,)), ((), ())),
                            preferred_element_type=F32)
        for k in range(n_k):
            for jl in range(jl_n):
                uf_scr[pl.ds(k * T + jh * jl_n + jl, nc, stride=pitch), :] = (
                    t[k * nc:(k + 1) * nc, jl * LANES:(jl + 1) * LANES])

    def stage_out(c, carry):
        dst = pl.ds(pl.multiple_of(c * chain, chain), chain)
        y_ref[dst, :] = uf_scr[pl.ds(pl.multiple_of(c * pitch, 8), chain), :].astype(BF16)
        return carry

    lax.fori_loop(0, nc, stage_out, 0)


def _ssm_scan(main, ssm_w, h0, layer, cfg):
    perm, catf, catb, bbt, pa, w2, av, pw, dsk = ssm_w
    T, C = SSM_CHUNK, cfg.ssm_group
    tc = T * C
    st = 2 * cfg.ssm_state
    gt = LANES // C
    nq = cfg.ssm_width // LANES
    chain = T * cfg.n_k
    half = cfg.p_tokens
    assert cfg.seq == chain and cfg.dec_seq % chain == 0 and T % gt == 0 and cfg.s_tokens == half
    nc = half // chain
    rows = nc * cfg.n_k
    pw_n = gt * LANES
    ucol = 2 * cfg.attn_width // LANES

    def wspec(*shape):
        nd = len(shape)
        return pl.BlockSpec((None, gt) + shape, lambda q, s: (layer, q) + (0,) * nd)

    return pl.pallas_call(
        functools.partial(_ssm_kernel, cfg=cfg),
        grid=(nq, 2),
        in_specs=[
            pl.BlockSpec((half, LANES), lambda q, s: (s, ucol + q)),
            pl.BlockSpec((pw_n, pw_n), lambda q, s: (0, 0)),
            wspec(st, tc), wspec(st, tc), wspec(2, C, st), wspec(2, T, 2, st), wspec(2 * st, tc),
            wspec(8, st), wspec(cfg.n_k, 4, st), wspec(1, tc), wspec(4, cfg.dec_batch, st),
        ],
        out_specs=[
            pl.BlockSpec((half, LANES), lambda q, s: (s, q)),
            pl.BlockSpec((gt, 2, cfg.batch, st), lambda q, s: (q, 0, 0, 0)),
        ],
        out_shape=[
            jax.ShapeDtypeStruct((cfg.tokens, cfg.ssm_width), BF16),
            jax.ShapeDtypeStruct((cfg.n_groups, 2, cfg.batch, st), F32),
        ],
        scratch_shapes=[
            pltpu.VMEM((nc * (chain + SSM_ROW_PAD), LANES), F32),
            pltpu.VMEM((T // gt, rows, pw_n), BF16),
            pltpu.VMEM((gt, rows, tc), BF16),
            pltpu.VMEM((tc, tc), BF16),
            pltpu.VMEM((tc, 4 * st), BF16),
            pltpu.VMEM((rows, 4 * st), F32),
            pltpu.VMEM((rows, 4 * st), F32),
            pltpu.VMEM((4, nc, st), F32),
            pltpu.VMEM((4, nc, st), F32),
        ],
        compiler_params=_params("parallel", "arbitrary"),
        name="ssm_scan",
    )(main, perm, catf, catb, bbt, pa, w2, av, pw, dsk, h0)


def _ssm_weights(lam_re, lam_im, log_step, b_re, b_im, c_re, c_im, d_skip, cfg):
    T, C, P, G, nk = SSM_CHUNK, cfg.ssm_group, cfg.ssm_state, cfg.n_groups, cfg.n_k
    lr, li = lam_re.astype(F32), lam_im.astype(F32)
    dt = jnp.exp(log_step.astype(F32))[..., None]

    def powers(js):
        jj = jnp.asarray(js, F32)[:, None]
        mag = jnp.exp(lr[..., None, :] * dt[..., None, :] * jj)
        ang = li[..., None, :] * dt[..., None, :] * jj
        return mag * jnp.cos(ang), mag * jnp.sin(ang)

    pr, pi = powers(range(T + 1))
    ab_re, ab_im = pr[..., 1, :], pi[..., 1, :]
    nr, ni = ab_re - 1.0, ab_im
    den = lr * lr + li * li
    f_re = (nr * lr + ni * li) / den
    f_im = (ni * lr - nr * li) / den
    br, bi = b_re.astype(F32), b_im.astype(F32)
    bb_re = f_re[..., None] * br - f_im[..., None] * bi
    bb_im = f_re[..., None] * bi + f_im[..., None] * br
    cr, ci = c_re.astype(F32), c_im.astype(F32)
    ca_re = cr[..., None, :, :] * pr[..., :, None, :] - ci[..., None, :, :] * pi[..., :, None, :]
    ca_im = cr[..., None, :, :] * pi[..., :, None, :] + ci[..., None, :, :] * pr[..., :, None, :]

    def state_out(d, lo, rev):
        def pick(x):
            x = x[:, d, :, lo:lo + T]
            x = jnp.flip(x, axis=2) if rev else x
            return x.transpose(0, 1, 4, 2, 3).reshape(-1, G, P, T * C)
        return jnp.concatenate([pick(ca_re), -pick(ca_im)], axis=2)

    catf = state_out(0, 0, False)
    catb = state_out(1, 0, True)
    w2 = jnp.concatenate([state_out(0, 1, False), state_out(1, 1, True)], axis=2).astype(BF16)
    bbt = jnp.concatenate([bb_re, bb_im], axis=3).transpose(0, 2, 1, 4, 3)

    def lanes(re, im):
        return jnp.concatenate([re, re], axis=-1), jnp.concatenate([-im, im], axis=-1)

    paf = lanes(jnp.flip(pr[:, 0, :, 0:T], axis=2), jnp.flip(pi[:, 0, :, 0:T], axis=2))
    pab = lanes(pr[:, 1, :, 0:T], pi[:, 1, :, 0:T])
    pa = jnp.stack([jnp.stack(paf, axis=3), jnp.stack(pab, axis=3)], axis=2)

    qr, qi = powers([T * k for k in range(nk + 1)])
    rows = []
    for src_r, src_i, idx in ((pr, pi, T), (qr, qi, nk)):
        for d in range(2):
            rows.extend(lanes(src_r[:, d, :, idx], src_i[:, d, :, idx]))
    av = jnp.stack(rows, axis=2)
    p1f, p2f = lanes(qr[:, 0, :, 0:nk], qi[:, 0, :, 0:nk])
    p1b, p2b = lanes(jnp.flip(qr[:, 1, :, 0:nk], axis=2), jnp.flip(qi[:, 1, :, 0:nk], axis=2))
    pw = jnp.stack([p1f, p2f, p1b, p2b], axis=3)
    dsk = jnp.tile(d_skip.astype(F32).reshape(-1, G, 1, C), (1, 1, 1, T))

    gt = LANES // C
    n = gt * LANES
    src = jnp.arange(n)
    jl, r, c = src // LANES, (src % LANES) // C, src % C
    dst = r * LANES + jl * C + c
    perm = (dst[:, None] == jnp.arange(n)[None, :]).astype(BF16)
    return perm, catf, catb, bbt, pa, w2, av, pw, dsk


def _glu_kernel(y_ref, w_ref, g_ref, o_ref, *, cfg):
    z = jnp.dot(y_ref[...], w_ref[...], preferred_element_type=F32)
    sw = cfg.ssm_width
    a, g = z[:, :sw], z[:, sw:]
    o_ref[...] = (a / (1.0 + jnp.exp(-g)) * g_ref[...].astype(F32)).astype(BF16)


def _ssm_glu(y, w_glu_b, main, layer, cfg):
    tm, sw = cfg.tm_post, cfg.ssm_width
    gcol = (2 * cfg.attn_width + sw) // sw
    assert (2 * cfg.attn_width) % sw == 0
    return pl.pallas_call(
        functools.partial(_glu_kernel, cfg=cfg),
        grid=(cfg.tokens // tm,),
        in_specs=[
            pl.BlockSpec((tm, sw), lambda i: (i, 0)),
            pl.BlockSpec((None, sw, 2 * sw), lambda i: (layer, 0, 0)),
            pl.BlockSpec((tm, sw), lambda i: (i, gcol)),
        ],
        out_specs=pl.BlockSpec((tm, sw), lambda i: (i, 0)),
        out_shape=jax.ShapeDtypeStruct((cfg.tokens, sw), BF16),
        compiler_params=_params("parallel"),
        name="ssm_glu",
    )(y, w_glu_b, main)


def _fourier_kernel(prev_ref, x_ref, dl_ref, cs_ref, w_ref, g_ref, o_ref, z_scr, *, cfg, length):
    del prev_ref
    fg = cfg.fft_group
    rc = min(cfg.rc_fft, length)

    @pl.when(pl.program_id(1) == 0)
    def _():
        def body(ci, carry):
            rows = pl.ds(pl.multiple_of(ci * rc, rc), rc)
            rows2 = pl.ds(pl.multiple_of(length + ci * rc, rc), rc)
            for g in range(cfg.fft_width // fg):
                cols = slice(g * fg, (g + 1) * fg)
                t = jnp.dot(x_ref[rows, cols], cs_ref[...], preferred_element_type=F32)
                z_scr[rows, cols] = t[:, :fg].astype(BF16)
                z_scr[rows2, cols] = t[:, fg:].astype(BF16)
            return carry

        lax.fori_loop(0, length // rc, body, 0)

    mixed = jnp.dot(dl_ref[...], z_scr[...], preferred_element_type=F32).astype(BF16)
    four = jnp.dot(mixed, w_ref[...], preferred_element_type=F32)
    o_ref[...] = (four * g_ref[...].astype(F32)).astype(BF16)


def _fourier(prev, main, dl, cs, w_fft_b, layer, cfg, *, length, nbatch, row0):
    fw = cfg.fft_width
    tm = min(cfg.tm_fft, length)
    nr = length // tm
    xcol = (2 * cfg.attn_width + 2 * cfg.ssm_width) // fw
    assert (2 * cfg.attn_width + 2 * cfg.ssm_width) % fw == 0 and row0 % length == 0
    b0 = row0 // length
    t0 = row0 // tm

    def tile_map(b, r):
        return (t0 + b * nr + r, 0)

    return pl.pallas_call(
        functools.partial(_fourier_kernel, cfg=cfg, length=length),
        grid=(nbatch, nr),
        in_specs=[
            pl.BlockSpec(memory_space=pl.ANY),
            pl.BlockSpec((length, fw), lambda b, r: (b0 + b, xcol)),
            pl.BlockSpec((tm, 2 * length), lambda b, r: (r, 0)),
            pl.BlockSpec((cfg.fft_group, 2 * cfg.fft_group), lambda b, r: (0, 0)),
            pl.BlockSpec((None, fw, fw), lambda b, r: (layer, 0, 0)),
            pl.BlockSpec((tm, fw), lambda b, r: (t0 + b * nr + r, xcol + 1)),
        ],
        out_specs=pl.BlockSpec((tm, fw), tile_map),
        out_shape=jax.ShapeDtypeStruct((cfg.tokens, fw), BF16),
        scratch_shapes=[pltpu.VMEM((2 * length, fw), BF16)],
        input_output_aliases={0: 0},
        compiler_params=_params("parallel", "arbitrary"),
        name=f"fourier_{length}",
    )(prev, main, dl, cs, w_fft_b, main)


def _dft_cos_sin(n):
    k = jnp.arange(n, dtype=jnp.int32)
    w = 2.0 * math.pi / n

    ang = ((k[:, None] * k[None, :]) % n).astype(F32) * w
    scale = n ** -0.5
    return jnp.cos(ang) * scale, jnp.sin(ang) * scale


def _dft_position_table(n):
    s = 1
    while s * s < n:
        s *= 2
    if n <= 512 or n % s:
        c, sn = _dft_cos_sin(n)
        return jnp.concatenate([c, -sn], axis=1).astype(BF16)
    k = jnp.arange(n, dtype=jnp.int32)
    w = 2.0 * math.pi / n

    def cs(j):
        ang = ((j[:, None] * k[None, :]) % n).astype(F32) * w
        return jnp.cos(ang), jnp.sin(ang)

    c1, s1 = cs(jnp.arange(n // s, dtype=jnp.int32) * s)
    c0, s0 = cs(jnp.arange(s, dtype=jnp.int32))
    a1 = jnp.concatenate([c1, -s1], axis=1)[:, None, :]
    a2 = jnp.concatenate([s1, c1], axis=1)[:, None, :]
    b0 = jnp.concatenate([c0, c0], axis=1)[None]
    b1 = jnp.concatenate([s0, s0], axis=1)[None]
    return ((a1 * b0 - a2 * b1) * n ** -0.5).reshape(n, 2 * n).astype(BF16)


def _out_proj_kernel(a_ref, s_ref, f_ref, w_ref, xa_ref, xb_ref, mod_ref, o_ref, *, cfg, npb):
    aw, sw = cfg.attn_width, cfg.ssm_width
    acc = jnp.dot(a_ref[...], w_ref[0:aw, :], preferred_element_type=F32)
    acc += jnp.dot(s_ref[...], w_ref[aw:aw + sw, :], preferred_element_type=F32)
    acc += jnp.dot(f_ref[...], w_ref[aw + sw:, :], preferred_element_type=F32)
    x = jnp.where(pl.program_id(1) < npb, xa_ref[...], xb_ref[...])
    o_ref[...] = x + mod_ref[2:3, :] * acc


def _out_proj(attn, ssm, four, w_out_b, xa, xb, xb_row0, mod, layer, cfg):
    tm, tn, d = cfg.tm_out, cfg.tn_out, cfg.d_model
    npb = cfg.p_tokens // tm
    assert xb_row0 % tm == 0
    return pl.pallas_call(
        functools.partial(_out_proj_kernel, cfg=cfg, npb=npb),
        grid=(d // tn, cfg.tokens // tm),
        in_specs=[
            pl.BlockSpec((tm, cfg.attn_width), lambda j, i: (i, 0)),
            pl.BlockSpec((tm, cfg.ssm_width), lambda j, i: (i, 0)),
            pl.BlockSpec((tm, cfg.fft_width), lambda j, i: (i, 0)),
            pl.BlockSpec((None, d, tn), lambda j, i: (layer, 0, j)),
            pl.BlockSpec((tm, tn), lambda j, i: (jnp.minimum(i, npb - 1), j)),
            pl.BlockSpec((tm, tn), lambda j, i: (jnp.maximum(i, npb) - npb + xb_row0 // tm, j)),
            pl.BlockSpec((None, None, 3, tn), lambda j, i: (layer, _row_group(i, cfg, tm), 0, j)),
        ],
        out_specs=pl.BlockSpec((tm, tn), lambda j, i: (i, j)),
        out_shape=jax.ShapeDtypeStruct((cfg.tokens, d), F32),
        compiler_params=_params("parallel", "arbitrary"),
        name="out_proj",
    )(attn, ssm, four, w_out_b, xa, xb, mod)


def _final_norm_kernel(x_ref, g_ref, o_ref):
    x = x_ref[...]
    ms = jnp.mean(x * x, axis=-1, keepdims=True)
    o_ref[...] = x * lax.rsqrt(ms + NORM_EPS) * g_ref[...]


def _final_norm(x, g, cfg, *, row0, nrows):
    tm, d = 256, cfg.d_model
    tm = min(tm, nrows)
    assert row0 % tm == 0
    return pl.pallas_call(
        _final_norm_kernel,
        grid=(nrows // tm,),
        in_specs=[pl.BlockSpec((tm, d), lambda i: (row0 // tm + i, 0)),
                  pl.BlockSpec((1, d), lambda i: (0, 0))],
        out_specs=pl.BlockSpec((tm, d), lambda i: (i, 0)),
        out_shape=jax.ShapeDtypeStruct((nrows, d), F32),
        compiler_params=_params("parallel"),
        name="final_norm",
    )(x, g)


def _rope_tables(cfg):
    hd = cfg.head_dim
    pairs = hd // 4
    t = jnp.arange(cfg.dec_seq)
    inv = ROPE_THETA ** (-jnp.arange(pairs, dtype=F32) / pairs)
    row_ang = (t // cfg.grid_w).astype(F32)[:, None] * inv[None, :]
    col_ang = (t % cfg.grid_w).astype(F32)[:, None] * inv[None, :]
    zeros = jnp.zeros_like(row_ang)
    cos = jnp.concatenate([jnp.cos(row_ang)] * 2 + [jnp.cos(col_ang)] * 2, axis=1)
    sa = jnp.concatenate([-jnp.sin(row_ang), zeros, -jnp.sin(col_ang), zeros], axis=1)
    sb = jnp.concatenate([zeros, jnp.sin(row_ang), zeros, jnp.sin(col_ang)], axis=1)
    ident = jnp.zeros((cfg.tm_in, hd), F32)
    return (jnp.concatenate([ident + 1.0, cos], axis=0),
            jnp.concatenate([ident, sa], axis=0),
            jnp.concatenate([ident, sb], axis=0))


def _forward(cfg, x_prompt, x_sample, cache_k, cache_v, state_fwd_re, state_fwd_im, state_bwd_re,
             state_bwd_im, c, c_ctx, norm_g, w_mod, b_mod, w_in, q_norm, k_norm, lam_re, lam_im,
             log_step, b_re, b_im, c_re, c_im, d_skip, w_glu, w_fft, w_out, final_norm_g):
    d, kvw = cfg.d_model, cfg.kv_width
    P = cfg.ssm_state

    w_in_b = w_in.astype(BF16)
    w_out_b = w_out.astype(BF16)
    w_glu_b = w_glu.astype(BF16)
    w_fft_b = w_fft.astype(BF16)
    gains = jnp.stack([k_norm, q_norm], axis=1).astype(F32)
    norm_g3 = norm_g.astype(F32)[:, None, :]
    ssm_w = _ssm_weights(lam_re, lam_im, log_step, b_re, b_im, c_re, c_im, d_skip, cfg)
    rope_tabs = _rope_tables(cfg)

    cc, sc = _dft_cos_sin(cfg.fft_group)
    cs = jnp.concatenate([cc, sc], axis=1).astype(BF16)
    dls = {}
    for length in {cfg.seq, cfg.dec_seq}:
        dls[length] = _dft_position_table(length)

    def st(re, im):
        re = re.astype(F32).transpose(1, 2, 0, 3)
        im = im.astype(F32).transpose(1, 2, 0, 3)
        return jnp.concatenate([re, im], axis=-1), jnp.concatenate([im, re], axis=-1)

    h0 = jnp.stack(st(state_fwd_re, state_fwd_im) + st(state_bwd_re, state_bwd_im), axis=2)

    nrow = 1 + cfg.dec_batch
    cvecs = jnp.concatenate([c_ctx[None, :], c], axis=0).astype(F32)
    cvecs = jnp.pad(cvecs, ((0, -nrow % 8), (0, 0)))
    mod = _modulation(cvecs, w_mod, b_mod, cfg).reshape(cfg.depth, cvecs.shape[0], 3, d)

    cache_k4 = cache_k.reshape(cfg.dec_batch, cfg.depth, cfg.past_len, kvw)
    cache_v4 = cache_v.reshape(cfg.dec_batch, cfg.depth, cfg.past_len, kvw)

    xs = (x_prompt.reshape(cfg.p_tokens, d), x_sample.reshape(cfg.s_tokens, d), 0)
    attn = jnp.zeros((cfg.tokens, cfg.attn_width), BF16)
    four = jnp.zeros((cfg.tokens, cfg.fft_width), BF16)
    ks, vs, fins = [], [], []
    for l in range(cfg.depth):
        kv, main = _in_proj(*xs, mod, norm_g3, w_in_b, gains, rope_tabs, l, cfg)
        attn = _attn_context(attn, main, kv, cfg)
        attn = _attn_latent(attn, main, kv, cache_k4, cache_v4, l, cfg)
        y, fin = _ssm_scan(main, ssm_w, h0, l, cfg)
        ssm = _ssm_glu(y, w_glu_b, main, l, cfg)
        four = _fourier(four, main, dls[cfg.seq], cs, w_fft_b, l, cfg,
                        length=cfg.seq, nbatch=cfg.batch, row0=0)
        four = _fourier(four, main, dls[cfg.dec_seq], cs, w_fft_b, l, cfg,
                        length=cfg.dec_seq, nbatch=cfg.dec_batch, row0=cfg.p_tokens)
        x = _out_proj(attn, ssm, four, w_out_b, *xs, mod, l, cfg)
        xs = (x, x, cfg.p_tokens)
        ks.append(kv[:cfg.p_tokens, :kvw].reshape(cfg.batch, cfg.seq, cfg.n_kv, cfg.head_dim))
        vs.append(kv[:cfg.p_tokens, kvw:].reshape(cfg.batch, cfg.seq, cfg.n_kv, cfg.head_dim))
        fins.append(fin)

    g = final_norm_g[None, :].astype(F32)
    y_prompt = _final_norm(x, g, cfg, row0=0, nrows=cfg.p_tokens).reshape(cfg.batch, cfg.seq, d)
    y_sample = _final_norm(x, g, cfg, row0=cfg.p_tokens, nrows=cfg.s_tokens).reshape(
        cfg.dec_batch, cfg.dec_seq, d)
    fin = jnp.stack(fins, axis=0)
    fin = fin.transpose(3, 0, 1, 2, 4)
    return (y_prompt, y_sample, jnp.stack(ks, axis=1), jnp.stack(vs, axis=1),
            fin[:, :, :, 0, :P], fin[:, :, :, 0, P:], fin[:, :, :, 1, :P], fin[:, :, :, 1, P:])


def kernel(x_prompt, x_sample, cache_k, cache_v, state_fwd_re, state_fwd_im, state_bwd_re, state_bwd_im,
           c, c_ctx, norm_g, w_mod, b_mod, w_in, q_norm, k_norm, lam_re, lam_im, log_step,
           b_re, b_im, c_re, c_im, d_skip, w_glu, w_fft, w_out, final_norm_g):
    return _forward(Cfg(), x_prompt, x_sample, cache_k, cache_v, state_fwd_re, state_fwd_im,
                    state_bwd_re, state_bwd_im, c, c_ctx, norm_g, w_mod, b_mod, w_in, q_norm, k_norm,
                    lam_re, lam_im, log_step, b_re, b_im, c_re, c_im, d_skip, w_glu, w_fft, w_out,
                    final_norm_g)
```

```python
import functools
import math
from typing import NamedTuple

import jax
import jax.numpy as jnp
from jax import lax
from jax.experimental import pallas as pl
from jax.experimental.pallas import tpu as pltpu

F32 = jnp.float32
BF16 = jnp.bfloat16
NORM_EPS = 1e-6
ROPE_THETA = 10000.0
LANES = 128
SSM_CHUNK = 16
SSM_ROW_PAD = 8
VMEM_LIMIT = 56 * 1024 * 1024
LOG2E = 1.4426950408889634
HIGHEST = lax.Precision.HIGHEST


class Cfg(NamedTuple):
    d_model: int = 4096
    batch: int = 32
    seq: int = 256
    depth: int = 4
    dec_batch: int = 2
    dec_seq: int = 4096
    past_len: int = 512
    grid_w: int = 64
    head_dim: int = 128
    n_heads: int = 16
    n_kv: int = 4
    ssm_width: int = 1024
    ssm_group: int = 16
    ssm_state: int = 64
    fft_width: int = 1024
    fft_group: int = 256
    tm_in: int = 512
    tm_out: int = 1024
    tn_out: int = 512
    tn_mod: int = 512
    tq: int = 1024
    rq: int = 128
    tm_fft: int = 256
    rc_fft: int = 512
    tm_post: int = 1024
    n_k: int = 16

    @property
    def attn_width(self):
        return self.n_heads * self.head_dim

    @property
    def kv_width(self):
        return self.n_kv * self.head_dim

    @property
    def q_per_kv(self):
        return self.n_heads // self.n_kv

    @property
    def n_groups(self):
        return self.ssm_width // self.ssm_group

    @property
    def tn_in(self):
        return 2 * self.kv_width

    @property
    def main_width(self):
        return 2 * self.attn_width + 2 * self.ssm_width + 2 * self.fft_width

    @property
    def p_tokens(self):
        return self.batch * self.seq

    @property
    def s_tokens(self):
        return self.dec_batch * self.dec_seq

    @property
    def tokens(self):
        return self.p_tokens + self.s_tokens


def _params(*sem):
    return pltpu.CompilerParams(dimension_semantics=sem, vmem_limit_bytes=VMEM_LIMIT)


def _silu(x):
    return x / (1.0 + jnp.exp(-x))


def _row_group(i, cfg, tm):
    npb = cfg.p_tokens // tm
    per = cfg.dec_seq // tm
    return jnp.where(i < npb, 0, 1 + (i - npb) // per)


def _mod_kernel(c_ref, w_ref, b_ref, o_ref):
    s = _silu(c_ref[...]).astype(BF16)
    o_ref[0] = jnp.dot(s, w_ref[0].astype(BF16), preferred_element_type=F32) + b_ref[0]


def _modulation(cvecs, w_mod, b_mod, cfg):
    d, n = cfg.d_model, 3 * cfg.d_model
    nrow = cvecs.shape[0]
    return pl.pallas_call(
        _mod_kernel,
        grid=(cfg.depth, n // cfg.tn_mod),
        in_specs=[
            pl.BlockSpec((nrow, d), lambda l, j: (0, 0)),
            pl.BlockSpec((1, d, cfg.tn_mod), lambda l, j: (l, 0, j)),
            pl.BlockSpec((1, 1, cfg.tn_mod), lambda l, j: (l, 0, j)),
        ],
        out_specs=pl.BlockSpec((1, nrow, cfg.tn_mod), lambda l, j: (l, 0, j)),
        out_shape=jax.ShapeDtypeStruct((cfg.depth, nrow, n), F32),
        compiler_params=_params("parallel", "parallel"),
        name="modulation",
    )(cvecs, w_mod, b_mod.reshape(cfg.depth, 1, n))


def _in_proj_kernel(xa_hbm, xb_hbm, mod_ref, ng_ref, w_ref, gains_ref, cos_ref, sa_ref, sb_ref,
                    kv_ref, main_ref, h_scr, x_scr, x_sem, *, cfg, kinds, npb, nb, xb_row0):
    i = pl.program_id(0)
    j = pl.program_id(1)
    tm = h_scr.shape[0]
    hd = cfg.head_dim
    tn = w_ref.shape[1]
    rc = min(tm, 16)

    def x_copy(src_hbm, row0):
        return pltpu.make_async_copy(src_hbm.at[pl.ds(pl.multiple_of(row0, tm), tm), :], x_scr, x_sem)

    def start_x(blk):
        @pl.when(blk < npb)
        def _():
            x_copy(xa_hbm, blk * tm).start()

        @pl.when(blk >= npb)
        def _():
            x_copy(xb_hbm, (blk - npb) * tm + xb_row0).start()

    @pl.when(j == 0)
    def _():
        @pl.when(i == 0)
        def _():
            start_x(i)

        x_copy(xa_hbm, 0).wait()
        shift = mod_ref[0:1, :]
        scale1 = 1.0 + mod_ref[1:2, :]
        g = ng_ref[...]

        def body(r, carry):
            rows = pl.ds(pl.multiple_of(r * rc, rc), rc)
            x = x_scr[rows, :]
            ms = jnp.mean(x * x, axis=-1, keepdims=True)
            y = x * lax.rsqrt(ms + NORM_EPS) * g
            h_scr[rows, :] = (y * scale1 + shift).astype(BF16)
            return carry

        lax.fori_loop(0, tm // rc, body, 0, unroll=min(4, tm // rc))

    @pl.when((j == 1) & (i + 1 < nb))
    def _():
        start_x(i + 1)

    sub = min(2 * hd, tn)

    def sub_dots():
        for s in range(tn // sub):
            yield s * sub, jnp.dot(h_scr[...], w_ref[:, s * sub:(s + 1) * sub],
                                   preferred_element_type=F32)

    def head_norm_rope(a, gain):
        ms = jnp.mean(a * a, axis=-1, keepdims=True)
        y = a * lax.rsqrt(ms + NORM_EPS) * gain
        return (y * cos_ref[...] + pltpu.roll(y, hd - hd // 4, 1) * sa_ref[...]
                + pltpu.roll(y, hd // 4, 1) * sb_ref[...])

    def tile(kind):
        for c0, acc in sub_dots():
            if kind in ("q", "kv"):
                for h in range(sub // hd):
                    sl = slice(c0 + h * hd, c0 + (h + 1) * hd)
                    a = acc[:, h * hd:(h + 1) * hd]
                    if kind == "q":
                        y = head_norm_rope(a, gains_ref[1:2, :]) * (hd ** -0.5 * LOG2E)
                        main_ref[:, sl] = y.astype(BF16)
                    elif c0 + h * hd < cfg.kv_width:
                        kv_ref[:, sl] = head_norm_rope(a, gains_ref[0:1, :])
                    else:
                        kv_ref[:, sl] = a
            elif kind == "silu":
                main_ref[:, c0:c0 + sub] = _silu(acc).astype(BF16)
            else:
                main_ref[:, c0:c0 + sub] = acc.astype(BF16)

    for kind in sorted(set(kinds)):
        pred = functools.reduce(lambda a, b: a | b, [j == t for t, k in enumerate(kinds) if k == kind])
        pl.when(pred)(functools.partial(tile, kind))


def _tile_kinds(cfg):
    tn = cfg.tn_in
    widths = [("q", cfg.attn_width), ("kv", 2 * cfg.kv_width), ("silu", cfg.attn_width),
              ("plain", cfg.ssm_width), ("silu", cfg.ssm_width), ("plain", cfg.fft_width),
              ("silu", cfg.fft_width)]
    kinds = []
    for name, w in widths:
        assert w % tn == 0
        kinds += [name] * (w // tn)
    return tuple(kinds)


def _in_proj(xa, xb, xb_row0, mod, norm_g, w_in_b, gains, rope_tabs, layer, cfg):
    tm, tn, d = cfg.tm_in, cfg.tn_in, cfg.d_model
    kinds = _tile_kinds(cfg)
    nj = len(kinds)
    kv0 = kinds.index("kv")
    nkv = 2 * cfg.kv_width // tn
    npb = cfg.p_tokens // tm
    per = cfg.dec_seq // tm
    cos_t, sa_t, sb_t = rope_tabs
    nb = cfg.tokens // tm
    assert xb_row0 % tm == 0 and cfg.p_tokens % tm == 0 and cfg.dec_seq % tm == 0
    tab_spec = pl.BlockSpec((tm, cfg.head_dim),
                            lambda i, j: (jnp.where(i < npb, 0, 1 + (i - npb) % per), 0))
    return pl.pallas_call(
        functools.partial(_in_proj_kernel, cfg=cfg, kinds=kinds, npb=npb, nb=nb, xb_row0=xb_row0),
        grid=(nb, nj),
        in_specs=[
            pl.BlockSpec(memory_space=pl.ANY),
            pl.BlockSpec(memory_space=pl.ANY),
            pl.BlockSpec((None, None, 3, d), lambda i, j: (layer, _row_group(i, cfg, tm), 0, 0)),
            pl.BlockSpec((None, 1, d), lambda i, j: (layer, 0, 0)),
            pl.BlockSpec((None, d, tn), lambda i, j: (layer, 0, j)),
            pl.BlockSpec((None, 2, cfg.head_dim), lambda i, j: (layer, 0, 0)),
            tab_spec, tab_spec, tab_spec,
        ],
        out_specs=[
            pl.BlockSpec((tm, tn), lambda i, j: (i, jnp.clip(j - kv0, 0, nkv - 1))),
            pl.BlockSpec((tm, tn), lambda i, j: (i, jnp.where(j < kv0, j, jnp.maximum(j - nkv, kv0 - 1)))),
        ],
        out_shape=[
            jax.ShapeDtypeStruct((cfg.tokens, 2 * cfg.kv_width), F32),
            jax.ShapeDtypeStruct((cfg.tokens, cfg.main_width), BF16),
        ],
        scratch_shapes=[pltpu.VMEM((tm, d), BF16), pltpu.VMEM((tm, d), F32),
                        pltpu.SemaphoreType.DMA],
        compiler_params=_params("arbitrary", "arbitrary"),
        name="in_proj",
    )(xa, xb, mod, norm_g, w_in_b, gains, cos_t, sa_t, sb_t)


def _softmax_attend(q, k, v1):
    hd = q.shape[1]
    s = lax.dot_general(q, k, (((1,), (1,)), ((), ())), preferred_element_type=F32)
    m = jnp.max(s, axis=-1, keepdims=True)
    p = jnp.exp2(s - m).astype(BF16)
    o = jnp.dot(p, v1, preferred_element_type=F32)
    return o[:, :hd] / o[:, hd:]


def _attn_ctx_kernel(prev_ref, q_ref, kv_ref, g_ref, o_ref, *, cfg):
    del prev_ref
    hd, kvw = cfg.head_dim, cfg.kv_width
    for kh in range(cfg.n_kv):
        k = kv_ref[:, kh * hd:(kh + 1) * hd].astype(BF16)
        v = kv_ref[:, kvw + kh * hd:kvw + (kh + 1) * hd].astype(BF16)
        v1 = jnp.concatenate([v, jnp.ones_like(v)], axis=1)
        for h in range(kh * cfg.q_per_kv, (kh + 1) * cfg.q_per_kv):
            sl = slice(h * hd, (h + 1) * hd)
            o = _softmax_attend(q_ref[:, sl], k, v1)
            o_ref[:, sl] = (o * g_ref[:, sl].astype(F32)).astype(BF16)


def _attn_context(attn_buf, main, kv, cfg):
    aw = cfg.attn_width
    return pl.pallas_call(
        functools.partial(_attn_ctx_kernel, cfg=cfg),
        grid=(cfg.batch,),
        in_specs=[
            pl.BlockSpec(memory_space=pl.ANY),
            pl.BlockSpec((cfg.seq, aw), lambda b: (b, 0)),
            pl.BlockSpec((cfg.seq, 2 * cfg.kv_width), lambda b: (b, 0)),
            pl.BlockSpec((cfg.seq, aw), lambda b: (b, 1)),
        ],
        out_specs=pl.BlockSpec((cfg.seq, aw), lambda b: (b, 0)),
        out_shape=jax.ShapeDtypeStruct((cfg.tokens, aw), BF16),
        input_output_aliases={0: 0},
        compiler_params=_params("parallel"),
        name="attn_context",
    )(attn_buf, main, kv, main)


def _attn_lat_kernel(prev_ref, q_ref, ck_ref, cv_ref, kn_ref, vn_ref, g_ref, o_ref,
                     k_scr, v_scr, *, cfg):
    del prev_ref
    hd, past = cfg.head_dim, cfg.past_len

    @pl.when(pl.program_id(2) == 0)
    def _():
        k_scr[0:past, :] = ck_ref[...].astype(BF16)
        k_scr[past:, :] = kn_ref[...].astype(BF16)
        v_scr[0:past, 0:hd] = cv_ref[...].astype(BF16)
        v_scr[past:, 0:hd] = vn_ref[...].astype(BF16)
        v_scr[:, hd:] = jnp.ones((v_scr.shape[0], hd), BF16)

    k = k_scr[...]
    v1 = v_scr[...]
    rq = min(cfg.rq, q_ref.shape[0])
    for h in range(cfg.q_per_kv):
        sl = slice(h * hd, (h + 1) * hd)
        for r in range(q_ref.shape[0] // rq):
            rows = slice(r * rq, (r + 1) * rq)
            o = _softmax_attend(q_ref[rows, sl], k, v1)
            o_ref[rows, sl] = (o * g_ref[rows, sl].astype(F32)).astype(BF16)


def _attn_latent(attn_prev, main, kv, cache_k, cache_v, layer, cfg):
    hd, nkv, tq = cfg.head_dim, cfg.n_kv, cfg.tq
    qw = cfg.q_per_kv * hd
    nqb = cfg.dec_seq // tq
    row0 = cfg.p_tokens // tq
    kvrow0 = cfg.p_tokens // cfg.dec_seq
    nkeys = cfg.past_len + cfg.dec_seq

    def q_map(b, h, qi):
        return (row0 + b * nqb + qi, h)

    def g_map(b, h, qi):
        return (row0 + b * nqb + qi, nkv + h)

    cache_spec = pl.BlockSpec((None, None, cfg.past_len, hd), lambda b, h, qi: (b, layer, 0, h))
    return pl.pallas_call(
        functools.partial(_attn_lat_kernel, cfg=cfg),
        grid=(cfg.dec_batch, nkv, nqb),
        in_specs=[
            pl.BlockSpec(memory_space=pl.ANY),
            pl.BlockSpec((tq, qw), q_map),
            cache_spec, cache_spec,
            pl.BlockSpec((cfg.dec_seq, hd), lambda b, h, qi: (kvrow0 + b, h)),
            pl.BlockSpec((cfg.dec_seq, hd), lambda b, h, qi: (kvrow0 + b, nkv + h)),
            pl.BlockSpec((tq, qw), g_map),
        ],
        out_specs=pl.BlockSpec((tq, qw), q_map),
        out_shape=jax.ShapeDtypeStruct((cfg.tokens, cfg.attn_width), BF16),
        scratch_shapes=[pltpu.VMEM((nkeys, hd), BF16), pltpu.VMEM((nkeys, 2 * hd), BF16)],
        input_output_aliases={0: 0},
        compiler_params=_params("parallel", "parallel", "arbitrary"),
        name="attn_latent",
    )(attn_prev, main, cache_k, cache_v, kv, kv, main)


def _ssm_kernel(u_ref, perm_ref, catf_ref, catb_ref, bbt_ref, pa_ref, w2_ref, av_ref,
                pw_ref, dsk_ref, h0_ref, y_ref, fin_ref,
                uf_scr, xcat_scr, u8_scr, m_scr, w1s_scr, s_scr, hin_scr, f_scr, g_scr, *, cfg):
    T, C = SSM_CHUNK, cfg.ssm_group
    tc = T * C
    st = 2 * cfg.ssm_state
    n_k = cfg.n_k
    chain = T * n_k
    nc = cfg.p_tokens // chain
    n_sc = cfg.dec_seq // chain
    gt = LANES // C
    jl_n = LANES // C
    jh_n = T // jl_n
    is_latent = pl.program_id(1) == 1
    y8_scr = xcat_scr
    pitch = chain + SSM_ROW_PAD

    def stage_in(c, carry):
        src = pl.ds(pl.multiple_of(c * chain, chain), chain)
        uf_scr[pl.ds(pl.multiple_of(c * pitch, 8), chain), :] = u_ref[src, :].astype(F32)
        return carry

    lax.fori_loop(0, nc, stage_in, 0)

    def gather(k, carry):
        for j in range(T):
            piece = uf_scr[pl.ds(k * T + j, nc, stride=pitch), :].astype(BF16)
            xcat_scr[j // jl_n, pl.ds(pl.multiple_of(k * nc, nc), nc),
                     (j % jl_n) * LANES:(j % jl_n + 1) * LANES] = piece
        return carry

    lax.fori_loop(0, n_k, gather, 0)
    for jh in range(jh_n):
        t = jnp.dot(xcat_scr[jh], perm_ref[...], preferred_element_type=F32)
        for r in range(gt):
            u8_scr[r, :, jh * LANES:(jh + 1) * LANES] = t[:, r * LANES:(r + 1) * LANES].astype(BF16)

    def cmul(h, hs, c1, c2):
        return h * c1 + hs * c2, hs * c1 - h * c2

    lane = lax.broadcasted_iota(jnp.int32, (C, tc), 1)

    for r in range(gt):
        ktf = jnp.dot(bbt_ref[r, 0], catf_ref[r], preferred_element_type=F32, precision=HIGHEST)
        ktb = jnp.dot(bbt_ref[r, 1], catb_ref[r], preferred_element_type=F32, precision=HIGHEST)
        for j in range(T):
            rf = pltpu.roll(ktf, C * j, 1) if j else ktf
            rf = jnp.where(lane >= C * j, rf, 0.0)
            sh = (T - 1 - j) * C
            rb = pltpu.roll(ktb, tc - sh, 1) if sh else ktb
            rb = jnp.where(lane < tc - sh, rb, 0.0)
            m_scr[j * C:(j + 1) * C, :] = (rf + rb).astype(BF16)
        for d in range(2):
            bb = bbt_ref[r, d]
            bbs = pltpu.roll(bb, st // 2, 1)
            for j in range(T):
                w, ws = cmul(bb, bbs, pa_ref[r, d, j, 0:1, :], pa_ref[r, d, j, 1:2, :])
                w1s_scr[j * C:(j + 1) * C, 2 * d * st:(2 * d + 1) * st] = w.astype(BF16)
                w1s_scr[j * C:(j + 1) * C, (2 * d + 1) * st:(2 * d + 2) * st] = ws.astype(BF16)

        u = u8_scr[r]
        y_intra = jnp.dot(u, m_scr[...], preferred_element_type=F32)
        s_scr[...] = jnp.dot(u, w1s_scr[...], preferred_element_type=F32)
        av = av_ref[r]
        a1 = (av[0:1], av[2:3])
        a2 = (av[1:2], av[3:4])
        b1 = (av[4:5], av[6:7])
        b2 = (av[5:6], av[7:8])

        def level1(k, carry):
            hf, hfs, hb, hbs = carry
            rf_ = pl.ds(pl.multiple_of(k * nc, nc), nc)
            rb_ = pl.ds(pl.multiple_of((n_k - 1 - k) * nc, nc), nc)
            hin_scr[rf_, 0:st] = hf
            hin_scr[rf_, st:2 * st] = hfs
            hin_scr[rb_, 2 * st:3 * st] = hb
            hin_scr[rb_, 3 * st:4 * st] = hbs
            nf, nfs = cmul(hf, hfs, a1[0], a2[0])
            nbk, nbs = cmul(hb, hbs, a1[1], a2[1])
            return (nf + s_scr[rf_, 0:st], nfs + s_scr[rf_, st:2 * st],
                    nbk + s_scr[rb_, 2 * st:3 * st], nbs + s_scr[rb_, 3 * st:4 * st])

        z = jnp.zeros((nc, st), F32)
        ff, ffs, fb, fbs = lax.fori_loop(0, n_k, level1, (z, z, z, z))

        @pl.when(jnp.logical_not(is_latent))
        def _():
            fin_ref[r, 0] = ff
            fin_ref[r, 1] = fb

        @pl.when(is_latent)
        def _():
            for idx, val in enumerate((ff, ffs, fb, fbs)):
                f_scr[idx] = val
            h0 = h0_ref[r]
            gf, gfs, gb, gbs = h0[0], h0[1], h0[2], h0[3]
            for sc in range(n_sc):
                sel = pl.ds(sc, cfg.dec_batch, stride=n_sc)
                g_scr[0, sel, :] = gf
                g_scr[1, sel, :] = gfs
                nf, nfs = cmul(gf, gfs, b1[0], b2[0])
                gf = nf + f_scr[0, sel, :]
                gfs = nfs + f_scr[1, sel, :]
            for sc in range(n_sc - 1, -1, -1):
                sel = pl.ds(sc, cfg.dec_batch, stride=n_sc)
                g_scr[2, sel, :] = gb
                g_scr[3, sel, :] = gbs
                nbk, nbs = cmul(gb, gbs, b1[1], b2[1])
                gb = nbk + f_scr[2, sel, :]
                gbs = nbs + f_scr[3, sel, :]
            gfa, gfsa, gba, gbsa = g_scr[0], g_scr[1], g_scr[2], g_scr[3]
            for k in range(n_k):
                rows_k = slice(k * nc, (k + 1) * nc)
                pf = pw_ref[r, k]
                cf, cfs = cmul(gfa, gfsa, pf[0:1], pf[1:2])
                cb, cbs = cmul(gba, gbsa, pf[2:3], pf[3:4])
                hin_scr[rows_k, 0:st] += cf
                hin_scr[rows_k, st:2 * st] += cfs
                hin_scr[rows_k, 2 * st:3 * st] += cb
                hin_scr[rows_k, 3 * st:4 * st] += cbs

        hsel = jnp.concatenate([hin_scr[:, 0:st], hin_scr[:, 2 * st:3 * st]], axis=1).astype(BF16)
        y = y_intra + jnp.dot(hsel, w2_ref[r], preferred_element_type=F32)
        y = (y + dsk_ref[r] * u.astype(F32)).astype(BF16)
        for jh in range(jh_n):
            y8_scr[jh, :, r * LANES:(r + 1) * LANES] = y[:, jh * LANES:(jh + 1) * LANES]

    for jh in range(jh_n):
        t = lax.dot_general(y8_scr[jh], perm_ref[...], (((1,), (1,)), ((), ())),
                            preferred_element_type=F32)
        for k in range(n_k):
            for jl in range(jl_n):
                uf_scr[pl.ds(k * T + jh * jl_n + jl, nc, stride=pitch), :] = (
                    t[k * nc:(k + 1) * nc, jl * LANES:(jl + 1) * LANES])

    def stage_out(c, carry):
        dst = pl.ds(pl.multiple_of(c * chain, chain), chain)
        y_ref[dst, :] = uf_scr[pl.ds(pl.multiple_of(c * pitch, 8), chain), :].astype(BF16)
        return carry

    lax.fori_loop(0, nc, stage_out, 0)


def _ssm_scan(main, ssm_w, h0, layer, cfg):
    perm, catf, catb, bbt, pa, w2, av, pw, dsk = ssm_w
    T, C = SSM_CHUNK, cfg.ssm_group
    tc = T * C
    st = 2 * cfg.ssm_state
    gt = LANES // C
    nq = cfg.ssm_width // LANES
    chain = T * cfg.n_k
    half = cfg.p_tokens
    assert cfg.seq == chain and cfg.dec_seq % chain == 0 and T % gt == 0 and cfg.s_tokens == half
    nc = half // chain
    rows = nc * cfg.n_k
    pw_n = gt * LANES
    ucol = 2 * cfg.attn_width // LANES

    def wspec(*shape):
        nd = len(shape)
        return pl.BlockSpec((None, gt) + shape, lambda q, s: (layer, q) + (0,) * nd)

    return pl.pallas_call(
        functools.partial(_ssm_kernel, cfg=cfg),
        grid=(nq, 2),
        in_specs=[
            pl.BlockSpec((half, LANES), lambda q, s: (s, ucol + q)),
            pl.BlockSpec((pw_n, pw_n), lambda q, s: (0, 0)),
            wspec(st, tc), wspec(st, tc), wspec(2, C, st), wspec(2, T, 2, st), wspec(2 * st, tc),
            wspec(8, st), wspec(cfg.n_k, 4, st), wspec(1, tc), wspec(4, cfg.dec_batch, st),
        ],
        out_specs=[
            pl.BlockSpec((half, LANES), lambda q, s: (s, q)),
            pl.BlockSpec((gt, 2, cfg.batch, st), lambda q, s: (q, 0, 0, 0)),
        ],
        out_shape=[
            jax.ShapeDtypeStruct((cfg.tokens, cfg.ssm_width), BF16),
            jax.ShapeDtypeStruct((cfg.n_groups, 2, cfg.batch, st), F32),
        ],
        scratch_shapes=[
            pltpu.VMEM((nc * (chain + SSM_ROW_PAD), LANES), F32),
            pltpu.VMEM((T // gt, rows, pw_n), BF16),
            pltpu.VMEM((gt, rows, tc), BF16),
            pltpu.VMEM((tc, tc), BF16),
            pltpu.VMEM((tc, 4 * st), BF16),
            pltpu.VMEM((rows, 4 * st), F32),
            pltpu.VMEM((rows, 4 * st), F32),
            pltpu.VMEM((4, nc, st), F32),
            pltpu.VMEM((4, nc, st), F32),
        ],
        compiler_params=_params("parallel", "arbitrary"),
        name="ssm_scan",
    )(main, perm, catf, catb, bbt, pa, w2, av, pw, dsk, h0)


def _ssm_weights(lam_re, lam_im, log_step, b_re, b_im, c_re, c_im, d_skip, cfg):
    T, C, P, G, nk = SSM_CHUNK, cfg.ssm_group, cfg.ssm_state, cfg.n_groups, cfg.n_k
    lr, li = lam_re.astype(F32), lam_im.astype(F32)
    dt = jnp.exp(log_step.astype(F32))[..., None]

    def powers(js):
        jj = jnp.asarray(js, F32)[:, None]
        mag = jnp.exp(lr[..., None, :] * dt[..., None, :] * jj)
        ang = li[..., None, :] * dt[..., None, :] * jj
        return mag * jnp.cos(ang), mag * jnp.sin(ang)

    pr, pi = powers(range(T + 1))
    ab_re, ab_im = pr[..., 1, :], pi[..., 1, :]
    nr, ni = ab_re - 1.0, ab_im
    den = lr * lr + li * li
    f_re = (nr * lr + ni * li) / den
    f_im = (ni * lr - nr * li) / den
    br, bi = b_re.astype(F32), b_im.astype(F32)
    bb_re = f_re[..., None] * br - f_im[..., None] * bi
    bb_im = f_re[..., None] * bi + f_im[..., None] * br
    cr, ci = c_re.astype(F32), c_im.astype(F32)
    ca_re = cr[..., None, :, :] * pr[..., :, None, :] - ci[..., None, :, :] * pi[..., :, None, :]
    ca_im = cr[..., None, :, :] * pi[..., :, None, :] + ci[..., None, :, :] * pr[..., :, None, :]

    def state_out(d, lo, rev):
        def pick(x):
            x = x[:, d, :, lo:lo + T]
            x = jnp.flip(x, axis=2) if rev else x
            return x.transpose(0, 1, 4, 2, 3).reshape(-1, G, P, T * C)
        return jnp.concatenate([pick(ca_re), -pick(ca_im)], axis=2)

    catf = state_out(0, 0, False)
    catb = state_out(1, 0, True)
    w2 = jnp.concatenate([state_out(0, 1, False), state_out(1, 1, True)], axis=2).astype(BF16)
    bbt = jnp.concatenate([bb_re, bb_im], axis=3).transpose(0, 2, 1, 4, 3)

    def lanes(re, im):
        return jnp.concatenate([re, re], axis=-1), jnp.concatenate([-im, im], axis=-1)

    paf = lanes(jnp.flip(pr[:, 0, :, 0:T], axis=2), jnp.flip(pi[:, 0, :, 0:T], axis=2))
    pab = lanes(pr[:, 1, :, 0:T], pi[:, 1, :, 0:T])
    pa = jnp.stack([jnp.stack(paf, axis=3), jnp.stack(pab, axis=3)], axis=2)

    qr, qi = powers([T * k for k in range(nk + 1)])
    rows = []
    for src_r, src_i, idx in ((pr, pi, T), (qr, qi, nk)):
        for d in range(2):
            rows.extend(lanes(src_r[:, d, :, idx], src_i[:, d, :, idx]))
    av = jnp.stack(rows, axis=2)
    p1f, p2f = lanes(qr[:, 0, :, 0:nk], qi[:, 0, :, 0:nk])
    p1b, p2b = lanes(jnp.flip(qr[:, 1, :, 0:nk], axis=2), jnp.flip(qi[:, 1, :, 0:nk], axis=2))
    pw = jnp.stack([p1f, p2f, p1b, p2b], axis=3)
    dsk = jnp.tile(d_skip.astype(F32).reshape(-1, G, 1, C), (1, 1, 1, T))

    gt = LANES // C
    n = gt * LANES
    src = jnp.arange(n)
    jl, r, c = src // LANES, (src % LANES) // C, src % C
    dst = r * LANES + jl * C + c
    perm = (dst[:, None] == jnp.arange(n)[None, :]).astype(BF16)
    return perm, catf, catb, bbt, pa, w2, av, pw, dsk


def _glu_kernel(y_ref, w_ref, g_ref, o_ref, *, cfg):
    z = jnp.dot(y_ref[...], w_ref[...], preferred_element_type=F32)
    sw = cfg.ssm_width
    a, g = z[:, :sw], z[:, sw:]
    o_ref[...] = (a / (1.0 + jnp.exp(-g)) * g_ref[...].astype(F32)).astype(BF16)


def _ssm_glu(y, w_glu_b, main, layer, cfg):
    tm, sw = cfg.tm_post, cfg.ssm_width
    gcol = (2 * cfg.attn_width + sw) // sw
    assert (2 * cfg.attn_width) % sw == 0
    return pl.pallas_call(
        functools.partial(_glu_kernel, cfg=cfg),
        grid=(cfg.tokens // tm,),
        in_specs=[
            pl.BlockSpec((tm, sw), lambda i: (i, 0)),
            pl.BlockSpec((None, sw, 2 * sw), lambda i: (layer, 0, 0)),
            pl.BlockSpec((tm, sw), lambda i: (i, gcol)),
        ],
        out_specs=pl.BlockSpec((tm, sw), lambda i: (i, 0)),
        out_shape=jax.ShapeDtypeStruct((cfg.tokens, sw), BF16),
        compiler_params=_params("parallel"),
        name="ssm_glu",
    )(y, w_glu_b, main)


def _fourier_kernel(prev_ref, x_ref, dl_ref, cs_ref, w_ref, g_ref, o_ref, z_scr, *, cfg, length):
    del prev_ref
    fg = cfg.fft_group
    rc = min(cfg.rc_fft, length)

    @pl.when(pl.program_id(1) == 0)
    def _():
        def body(ci, carry):
            rows = pl.ds(pl.multiple_of(ci * rc, rc), rc)
            rows2 = pl.ds(pl.multiple_of(length + ci * rc, rc), rc)
            for g in range(cfg.fft_width // fg):
                cols = slice(g * fg, (g + 1) * fg)
                t = jnp.dot(x_ref[rows, cols], cs_ref[...], preferred_element_type=F32)
                z_scr[rows, cols] = t[:, :fg].astype(BF16)
                z_scr[rows2, cols] = t[:, fg:].astype(BF16)
            return carry

        lax.fori_loop(0, length // rc, body, 0)

    mixed = jnp.dot(dl_ref[...], z_scr[...], preferred_element_type=F32).astype(BF16)
    four = jnp.dot(mixed, w_ref[...], preferred_element_type=F32)
    o_ref[...] = (four * g_ref[...].astype(F32)).astype(BF16)


def _fourier(prev, main, dl, cs, w_fft_b, layer, cfg, *, length, nbatch, row0):
    fw = cfg.fft_width
    tm = min(cfg.tm_fft, length)
    nr = length // tm
    xcol = (2 * cfg.attn_width + 2 * cfg.ssm_width) // fw
    assert (2 * cfg.attn_width + 2 * cfg.ssm_width) % fw == 0 and row0 % length == 0
    b0 = row0 // length
    t0 = row0 // tm

    def tile_map(b, r):
        return (t0 + b * nr + r, 0)

    return pl.pallas_call(
        functools.partial(_fourier_kernel, cfg=cfg, length=length),
        grid=(nbatch, nr),
        in_specs=[
            pl.BlockSpec(memory_space=pl.ANY),
            pl.BlockSpec((length, fw), lambda b, r: (b0 + b, xcol)),
            pl.BlockSpec((tm, 2 * length), lambda b, r: (r, 0)),
            pl.BlockSpec((cfg.fft_group, 2 * cfg.fft_group), lambda b, r: (0, 0)),
            pl.BlockSpec((None, fw, fw), lambda b, r: (layer, 0, 0)),
            pl.BlockSpec((tm, fw), lambda b, r: (t0 + b * nr + r, xcol + 1)),
        ],
        out_specs=pl.BlockSpec((tm, fw), tile_map),
        out_shape=jax.ShapeDtypeStruct((cfg.tokens, fw), BF16),
        scratch_shapes=[pltpu.VMEM((2 * length, fw), BF16)],
        input_output_aliases={0: 0},
        compiler_params=_params("parallel", "arbitrary"),
        name=f"fourier_{length}",
    )(prev, main, dl, cs, w_fft_b, main)


def _dft_cos_sin(n):
    k = jnp.arange(n, dtype=jnp.int32)
    w = 2.0 * math.pi / n

    ang = ((k[:, None] * k[None, :]) % n).astype(F32) * w
    scale = n ** -0.5
    return jnp.cos(ang) * scale, jnp.sin(ang) * scale


def _dft_position_table(n):
    s = 1
    while s * s < n:
        s *= 2
    if n <= 512 or n % s:
        c, sn = _dft_cos_sin(n)
        return jnp.concatenate([c, -sn], axis=1).astype(BF16)
    k = jnp.arange(n, dtype=jnp.int32)
    w = 2.0 * math.pi / n

    def cs(j):
        ang = ((j[:, None] * k[None, :]) % n).astype(F32) * w
        return jnp.cos(ang), jnp.sin(ang)

    c1, s1 = cs(jnp.arange(n // s, dtype=jnp.int32) * s)
    c0, s0 = cs(jnp.arange(s, dtype=jnp.int32))
    a1 = jnp.concatenate([c1, -s1], axis=1)[:, None, :]
    a2 = jnp.concatenate([s1, c1], axis=1)[:, None, :]
    b0 = jnp.concatenate([c0, c0], axis=1)[None]
    b1 = jnp.concatenate([s0, s0], axis=1)[None]
    return ((a1 * b0 - a2 * b1) * n ** -0.5).reshape(n, 2 * n).astype(BF16)


def _out_proj_kernel(a_ref, s_ref, f_ref, wf_ref, xa_ref, xb_ref, mod_ref, o_ref, w_ref, *, cfg, npb):
    aw, sw = cfg.attn_width, cfg.ssm_width

    @pl.when(pl.program_id(1) == 0)
    def _():
        w_ref[...] = wf_ref[...].astype(BF16)

    acc = jnp.dot(a_ref[...], w_ref[0:aw, :], preferred_element_type=F32)
    acc += jnp.dot(s_ref[...], w_ref[aw:aw + sw, :], preferred_element_type=F32)
    acc += jnp.dot(f_ref[...], w_ref[aw + sw:, :], preferred_element_type=F32)
    x = jnp.where(pl.program_id(1) < npb, xa_ref[...], xb_ref[...])
    o_ref[...] = x + mod_ref[2:3, :] * acc


def _out_proj(attn, ssm, four, w_out_b, xa, xb, xb_row0, mod, layer, cfg):
    tm, tn, d = cfg.tm_out, cfg.tn_out, cfg.d_model
    npb = cfg.p_tokens // tm
    assert xb_row0 % tm == 0
    return pl.pallas_call(
        functools.partial(_out_proj_kernel, cfg=cfg, npb=npb),
        grid=(d // tn, cfg.tokens // tm),
        in_specs=[
            pl.BlockSpec((tm, cfg.attn_width), lambda j, i: (i, 0)),
            pl.BlockSpec((tm, cfg.ssm_width), lambda j, i: (i, 0)),
            pl.BlockSpec((tm, cfg.fft_width), lambda j, i: (i, 0)),
            pl.BlockSpec((None, d, tn), lambda j, i: (layer, 0, j)),
            pl.BlockSpec((tm, tn), lambda j, i: (jnp.minimum(i, npb - 1), j)),
            pl.BlockSpec((tm, tn), lambda j, i: (jnp.maximum(i, npb) - npb + xb_row0 // tm, j)),
            pl.BlockSpec((None, None, 3, tn), lambda j, i: (layer, _row_group(i, cfg, tm), 0, j)),
        ],
        out_specs=pl.BlockSpec((tm, tn), lambda j, i: (i, j)),
        out_shape=jax.ShapeDtypeStruct((cfg.tokens, d), F32),
        scratch_shapes=[pltpu.VMEM((d, tn), BF16)],
        compiler_params=_params("parallel", "arbitrary"),
        name="out_proj",
    )(attn, ssm, four, w_out_b, xa, xb, mod)


def _final_norm_kernel(x_ref, g_ref, o_ref):
    x = x_ref[...]
    ms = jnp.mean(x * x, axis=-1, keepdims=True)
    o_ref[...] = x * lax.rsqrt(ms + NORM_EPS) * g_ref[...]


def _final_norm(x, g, cfg, *, row0, nrows):
    tm, d = 256, cfg.d_model
    tm = min(tm, nrows)
    assert row0 % tm == 0
    return pl.pallas_call(
        _final_norm_kernel,
        grid=(nrows // tm,),
        in_specs=[pl.BlockSpec((tm, d), lambda i: (row0 // tm + i, 0)),
                  pl.BlockSpec((1, d), lambda i: (0, 0))],
        out_specs=pl.BlockSpec((tm, d), lambda i: (i, 0)),
        out_shape=jax.ShapeDtypeStruct((nrows, d), F32),
        compiler_params=_params("parallel"),
        name="final_norm",
    )(x, g)


def _rope_tables(cfg):
    hd = cfg.head_dim
    pairs = hd // 4
    t = jnp.arange(cfg.dec_seq)
    inv = ROPE_THETA ** (-jnp.arange(pairs, dtype=F32) / pairs)
    row_ang = (t // cfg.grid_w).astype(F32)[:, None] * inv[None, :]
    col_ang = (t % cfg.grid_w).astype(F32)[:, None] * inv[None, :]
    zeros = jnp.zeros_like(row_ang)
    cos = jnp.concatenate([jnp.cos(row_ang)] * 2 + [jnp.cos(col_ang)] * 2, axis=1)
    sa = jnp.concatenate([-jnp.sin(row_ang), zeros, -jnp.sin(col_ang), zeros], axis=1)
    sb = jnp.concatenate([zeros, jnp.sin(row_ang), zeros, jnp.sin(col_ang)], axis=1)
    ident = jnp.zeros((cfg.tm_in, hd), F32)
    return (jnp.concatenate([ident + 1.0, cos], axis=0),
            jnp.concatenate([ident, sa], axis=0),
            jnp.concatenate([ident, sb], axis=0))


def _forward(cfg, x_prompt, x_sample, cache_k, cache_v, state_fwd_re, state_fwd_im, state_bwd_re,
             state_bwd_im, c, c_ctx, norm_g, w_mod, b_mod, w_in, q_norm, k_norm, lam_re, lam_im,
             log_step, b_re, b_im, c_re, c_im, d_skip, w_glu, w_fft, w_out, final_norm_g):
    d, kvw = cfg.d_model, cfg.kv_width
    P = cfg.ssm_state

    w_in_b = w_in.astype(BF16)
    w_out_b = w_out.astype(F32)
    w_glu_b = w_glu.astype(BF16)
    w_fft_b = w_fft.astype(BF16)
    gains = jnp.stack([k_norm, q_norm], axis=1).astype(F32)
    norm_g3 = norm_g.astype(F32)[:, None, :]
    ssm_w = _ssm_weights(lam_re, lam_im, log_step, b_re, b_im, c_re, c_im, d_skip, cfg)
    rope_tabs = _rope_tables(cfg)

    cc, sc = _dft_cos_sin(cfg.fft_group)
    cs = jnp.concatenate([cc, sc], axis=1).astype(BF16)
    dls = {}
    for length in {cfg.seq, cfg.dec_seq}:
        dls[length] = _dft_position_table(length)

    def st(re, im):
        re = re.astype(F32).transpose(1, 2, 0, 3)
        im = im.astype(F32).transpose(1, 2, 0, 3)
        return jnp.concatenate([re, im], axis=-1), jnp.concatenate([im, re], axis=-1)

    h0 = jnp.stack(st(state_fwd_re, state_fwd_im) + st(state_bwd_re, state_bwd_im), axis=2)

    nrow = 1 + cfg.dec_batch
    cvecs = jnp.concatenate([c_ctx[None, :], c], axis=0).astype(F32)
    cvecs = jnp.pad(cvecs, ((0, -nrow % 8), (0, 0)))
    mod = _modulation(cvecs, w_mod, b_mod, cfg).reshape(cfg.depth, cvecs.shape[0], 3, d)

    cache_k4 = cache_k.reshape(cfg.dec_batch, cfg.depth, cfg.past_len, kvw)
    cache_v4 = cache_v.reshape(cfg.dec_batch, cfg.depth, cfg.past_len, kvw)

    xs = (x_prompt.reshape(cfg.p_tokens, d), x_sample.reshape(cfg.s_tokens, d), 0)
    attn = jnp.zeros((cfg.tokens, cfg.attn_width), BF16)
    four = jnp.zeros((cfg.tokens, cfg.fft_width), BF16)
    ks, vs, fins = [], [], []
    for l in range(cfg.depth):
        kv, main = _in_proj(*xs, mod, norm_g3, w_in_b, gains, rope_tabs, l, cfg)
        attn = _attn_context(attn, main, kv, cfg)
        attn = _attn_latent(attn, main, kv, cache_k4, cache_v4, l, cfg)
        y, fin = _ssm_scan(main, ssm_w, h0, l, cfg)
        ssm = _ssm_glu(y, w_glu_b, main, l, cfg)
        four = _fourier(four, main, dls[cfg.seq], cs, w_fft_b, l, cfg,
                        length=cfg.seq, nbatch=cfg.batch, row0=0)
        four = _fourier(four, main, dls[cfg.dec_seq], cs, w_fft_b, l, cfg,
                        length=cfg.dec_seq, nbatch=cfg.dec_batch, row0=cfg.p_tokens)
        x = _out_proj(attn, ssm, four, w_out_b, *xs, mod, l, cfg)
        xs = (x, x, cfg.p_tokens)
        ks.append(kv[:cfg.p_tokens, :kvw].reshape(cfg.batch, cfg.seq, cfg.n_kv, cfg.head_dim))
        vs.append(kv[:cfg.p_tokens, kvw:].reshape(cfg.batch, cfg.seq, cfg.n_kv, cfg.head_dim))
        fins.append(fin)

    g = final_norm_g[None, :].astype(F32)
    y_prompt = _final_norm(x, g, cfg, row0=0, nrows=cfg.p_tokens).reshape(cfg.batch, cfg.seq, d)
    y_sample = _final_norm(x, g, cfg, row0=cfg.p_tokens, nrows=cfg.s_tokens).reshape(
        cfg.dec_batch, cfg.dec_seq, d)
    fin = jnp.stack(fins, axis=0)
    fin = fin.transpose(3, 0, 1, 2, 4)
    return (y_prompt, y_sample, jnp.stack(ks, axis=1), jnp.stack(vs, axis=1),
            fin[:, :, :, 0, :P], fin[:, :, :, 0, P:], fin[:, :, :, 1, :P], fin[:, :, :, 1, P:])


def kernel(x_prompt, x_sample, cache_k, cache_v, state_fwd_re, state_fwd_im, state_bwd_re, state_bwd_im,
           c, c_ctx, norm_g, w_mod, b_mod, w_in, q_norm, k_norm, lam_re, lam_im, log_step,
           b_re, b_im, c_re, c_im, d_skip, w_glu, w_fft, w_out, final_norm_g):
    return _forward(Cfg(), x_prompt, x_sample, cache_k, cache_v, state_fwd_re, state_fwd_im,
                    state_bwd_re, state_bwd_im, c, c_ctx, norm_g, w_mod, b_mod, w_in, q_norm, k_norm,
                    lam_re, lam_im, log_step, b_re, b_im, c_re, c_im, d_skip, w_glu, w_fft, w_out,
                    final_norm_g)
```
